```python
import math
import jax, jax.numpy as jnp
from jax import lax
import numpy as np

D_MODEL = 4096
BATCH = 16
SEQ = 2048
DEPTH = 1

MIX_WIDTH = D_MODEL
POOL_WIDTH = MIX_WIDTH // 2
SSM_WIDTH = MIX_WIDTH - POOL_WIDTH
POOL_WINDOWS = (2, 4, 8, 16)
POOL_GROUPS = len(POOL_WINDOWS)
POOL_GROUP_WIDTH = POOL_WIDTH // POOL_GROUPS
SSM_GROUP_CH = 16
SSM_GROUPS = SSM_WIDTH // SSM_GROUP_CH
SSM_STATE = 64
D_FF = ((8 * D_MODEL // 3 + 255) // 256) * 256
DT_MIN = 1e-3
DT_MAX = 1e-1
NORM_EPS = 1e-6

kernel_name = "macaron_pool_s5_hybrid_block"


def rms_norm(x, g):
    xf = x.astype(jnp.float32)
    y = xf * lax.rsqrt(jnp.mean(xf * xf, axis=-1, keepdims=True) + NORM_EPS)
    return (y * g.astype(jnp.float32)).astype(x.dtype)


def swiglu_ffn(h, w_gate, w_up, w_down):
    return (jax.nn.silu(h @ w_gate) * (h @ w_up)) @ w_down


def causal_multiscale_pool(z, w_pool, pool_scale):
    b, s, _ = z.shape
    zf = z.astype(jnp.float32).reshape(b, s, POOL_GROUPS, POOL_GROUP_WIDTH)
    cs = jnp.cumsum(zf, axis=1)
    t = jnp.arange(s)
    diffs = []
    for g, w in enumerate(POOL_WINDOWS):
        c = cs[:, :, g]
        lagged = jnp.pad(c[:, : s - w], ((0, 0), (w, 0), (0, 0)))
        cnt = jnp.minimum(t + 1, w).astype(jnp.float32)[None, :, None]
        diffs.append((c - lagged) / cnt - zf[:, :, g])
    d = jnp.stack(diffs, axis=2).astype(z.dtype)
    out = jnp.einsum("bsgc,gcd->bsgd", d, w_pool).reshape(b, s, POOL_WIDTH)
    return out * pool_scale


def _ssm_combine(e1, e2):
    a1, b1 = e1
    a2, b2 = e2
    return a1 * a2, a2 * b1 + b2


def s5_mixer(z, lam_re, lam_im, log_dt, b_re, b_im, c_re, c_im, d_skip, w_glu, b_glu):
    b, s, _ = z.shape
    u = z.astype(jnp.float32).reshape(b, s, SSM_GROUPS, SSM_GROUP_CH)
    lam = lax.complex(lam_re.astype(jnp.float32), lam_im.astype(jnp.float32))
    dt = jnp.exp(log_dt.astype(jnp.float32))[:, None]
    lam_bar = jnp.exp(lam * dt)
    b_mat = lax.complex(b_re.astype(jnp.float32), b_im.astype(jnp.float32))
    b_bar = ((lam_bar - 1.0) / lam)[:, :, None] * b_mat
    c_mat = lax.complex(c_re.astype(jnp.float32), c_im.astype(jnp.float32))
    bu = jnp.einsum("bsgh,gph->bsgp", u.astype(jnp.complex64), b_bar)
    a = jnp.broadcast_to(lam_bar, bu.shape)
    _, states = lax.associative_scan(_ssm_combine, (a, bu), axis=1)
    y = jnp.einsum("ghp,bsgp->bsgh", c_mat, states).real
    y = y + d_skip.astype(jnp.float32).reshape(SSM_GROUPS, SSM_GROUP_CH) * u
    y = jax.nn.gelu(y.reshape(b, s, SSM_WIDTH)).astype(z.dtype)
    return y * jax.nn.sigmoid(y @ w_glu + b_glu)


def _fwd_setup_inputs(seed: int = 0) -> dict:
    key = jax.random.key(seed)
    ks = jax.random.split(key, 32)
    f32 = jnp.float32
    nrm = lambda k, shape, scale: jax.random.normal(k, shape, f32) * scale
    gain = lambda k, n: 1.0 + 0.02 * jax.random.normal(k, (n,), f32)
    n_idx = jnp.arange(SSM_STATE, dtype=f32)[None, :]
    return {
        "x": jax.random.normal(ks[0], (BATCH, SEQ, D_MODEL), f32),
        "ffn1_norm": gain(ks[1], D_MODEL),
        "ffn1_gate": nrm(ks[2], (D_MODEL, D_FF), D_MODEL ** -0.5),
        "ffn1_up": nrm(ks[3], (D_MODEL, D_FF), D_MODEL ** -0.5),
        "ffn1_down": nrm(ks[4], (D_FF, D_MODEL), D_FF ** -0.5),
        "mix_norm": gain(ks[5], D_MODEL),
        "w_in": nrm(ks[6], (D_MODEL, MIX_WIDTH), D_MODEL ** -0.5),
        "w_pool": nrm(ks[7], (POOL_GROUPS, POOL_GROUP_WIDTH, POOL_GROUP_WIDTH), POOL_GROUP_WIDTH ** -0.5),
        "pool_scale": gain(ks[8], POOL_WIDTH),
        "lam_re": -0.5 + 0.01 * jax.random.normal(ks[9], (SSM_GROUPS, SSM_STATE), f32),
        "lam_im": math.pi * n_idx + 0.01 * jax.random.normal(ks[10], (SSM_GROUPS, SSM_STATE), f32),
        "log_dt": jax.random.uniform(ks[11], (SSM_GROUPS,), f32, math.log(DT_MIN), math.log(DT_MAX)),
        "b_re": nrm(ks[12], (SSM_GROUPS, SSM_STATE, SSM_GROUP_CH), (2.0 * SSM_GROUP_CH) ** -0.5),
        "b_im": nrm(ks[13], (SSM_GROUPS, SSM_STATE, SSM_GROUP_CH), (2.0 * SSM_GROUP_CH) ** -0.5),
        "c_re": nrm(ks[14], (SSM_GROUPS, SSM_GROUP_CH, SSM_STATE), (2.0 * SSM_STATE) ** -0.5),
        "c_im": nrm(ks[15], (SSM_GROUPS, SSM_GROUP_CH, SSM_STATE), (2.0 * SSM_STATE) ** -0.5),
        "d_skip": jax.random.normal(ks[16], (SSM_WIDTH,), f32),
        "w_glu": nrm(ks[17], (SSM_WIDTH, SSM_WIDTH), SSM_WIDTH ** -0.5),
        "b_glu": nrm(ks[18], (SSM_WIDTH,), 0.01),
        "pool_out_norm": gain(ks[19], POOL_WIDTH),
        "ssm_out_norm": gain(ks[20], SSM_WIDTH),
        "w_out": nrm(ks[21], (MIX_WIDTH, D_MODEL), MIX_WIDTH ** -0.5),
        "ffn2_norm": gain(ks[22], D_MODEL),
        "ffn2_gate": nrm(ks[23], (D_MODEL, D_FF), D_MODEL ** -0.5),
        "ffn2_up": nrm(ks[24], (D_MODEL, D_FF), D_MODEL ** -0.5),
        "ffn2_down": nrm(ks[25], (D_FF, D_MODEL), D_FF ** -0.5),
        "final_norm": gain(ks[26], D_MODEL),
    }


def _fwd_reference(x, ffn1_norm, ffn1_gate, ffn1_up, ffn1_down, mix_norm, w_in, w_pool, pool_scale,
              lam_re, lam_im, log_dt, b_re, b_im, c_re, c_im, d_skip, w_glu, b_glu,
              pool_out_norm, ssm_out_norm, w_out, ffn2_norm, ffn2_gate, ffn2_up, ffn2_down,
              final_norm):
    h = x
    for _ in range(DEPTH):
        h = h + 0.5 * swiglu_ffn(rms_norm(h, ffn1_norm), ffn1_gate, ffn1_up, ffn1_down)
        z = rms_norm(h, mix_norm) @ w_in
        z_pool = z[..., :POOL_WIDTH]
        z_ssm = z[..., POOL_WIDTH:]
        y_pool = causal_multiscale_pool(z_pool, w_pool, pool_scale)
        y_ssm = s5_mixer(z_ssm, lam_re, lam_im, log_dt, b_re, b_im, c_re, c_im,
                         d_skip, w_glu, b_glu)
        merged = jnp.concatenate(
            [rms_norm(y_pool, pool_out_norm), rms_norm(y_ssm, ssm_out_norm)], axis=-1)
        h = h + merged @ w_out
        h = h + 0.5 * swiglu_ffn(rms_norm(h, ffn2_norm), ffn2_gate, ffn2_up, ffn2_down)
    return rms_norm(h, final_norm)


import jax as _jax
import jax.numpy as _jnp

TWIN_FORMAT = 'train_step'
FWD_PARAMS = ['x', 'ffn1_norm', 'ffn1_gate', 'ffn1_up', 'ffn1_down', 'mix_norm', 'w_in', 'w_pool', 'pool_scale', 'lam_re', 'lam_im', 'log_dt', 'b_re', 'b_im', 'c_re', 'c_im', 'd_skip', 'w_glu', 'b_glu', 'pool_out_norm', 'ssm_out_norm', 'w_out', 'ffn2_norm', 'ffn2_gate', 'ffn2_up', 'ffn2_down', 'final_norm']
TWIN_WEIGHTS = ['ffn1_norm', 'ffn1_gate', 'ffn1_up', 'ffn1_down', 'mix_norm', 'w_in', 'w_pool', 'pool_scale', 'lam_re', 'lam_im', 'log_dt', 'b_re', 'b_im', 'c_re', 'c_im', 'd_skip', 'w_glu', 'b_glu', 'pool_out_norm', 'ssm_out_norm', 'w_out', 'ffn2_norm', 'ffn2_gate', 'ffn2_up', 'ffn2_down', 'final_norm']
TWIN_DIFF_INPUT = 'x'
TWIN_INPUTS = ['x', 'ffn1_norm', 'ffn1_gate', 'ffn1_up', 'ffn1_down', 'mix_norm', 'w_in', 'w_pool', 'pool_scale', 'lam_re', 'lam_im', 'log_dt', 'b_re', 'b_im', 'c_re', 'c_im', 'd_skip', 'w_glu', 'b_glu', 'pool_out_norm', 'ssm_out_norm', 'w_out', 'ffn2_norm', 'ffn2_gate', 'ffn2_up', 'ffn2_down', 'final_norm', 'loss_target', 'm_ffn1_norm', 'm_ffn1_gate', 'm_ffn1_up', 'm_ffn1_down', 'm_mix_norm', 'm_w_in', 'm_w_pool', 'm_pool_scale', 'm_lam_re', 'm_lam_im', 'm_log_dt', 'm_b_re', 'm_b_im', 'm_c_re', 'm_c_im', 'm_d_skip', 'm_w_glu', 'm_b_glu', 'm_pool_out_norm', 'm_ssm_out_norm', 'm_w_out', 'm_ffn2_norm', 'm_ffn2_gate', 'm_ffn2_up', 'm_ffn2_down', 'm_final_norm', 'v_ffn1_norm', 'v_ffn1_gate', 'v_ffn1_up', 'v_ffn1_down', 'v_mix_norm', 'v_w_in', 'v_w_pool', 'v_pool_scale', 'v_lam_re', 'v_lam_im', 'v_log_dt', 'v_b_re', 'v_b_im', 'v_c_re', 'v_c_im', 'v_d_skip', 'v_w_glu', 'v_b_glu', 'v_pool_out_norm', 'v_ssm_out_norm', 'v_w_out', 'v_ffn2_norm', 'v_ffn2_gate', 'v_ffn2_up', 'v_ffn2_down', 'v_final_norm']
TWIN_OUTPUTS = ['loss', 'grad_x', 'grad_ffn1_norm', 'grad_ffn1_gate', 'grad_ffn1_up', 'grad_ffn1_down', 'grad_mix_norm', 'grad_w_in', 'grad_w_pool', 'grad_pool_scale', 'grad_lam_re', 'grad_lam_im', 'grad_log_dt', 'grad_b_re', 'grad_b_im', 'grad_c_re', 'grad_c_im', 'grad_d_skip', 'grad_w_glu', 'grad_b_glu', 'grad_pool_out_norm', 'grad_ssm_out_norm', 'grad_w_out', 'grad_ffn2_norm', 'grad_ffn2_gate', 'grad_ffn2_up', 'grad_ffn2_down', 'grad_final_norm', 'delta_ffn1_norm', 'delta_ffn1_gate', 'delta_ffn1_up', 'delta_ffn1_down', 'delta_mix_norm', 'delta_w_in', 'delta_w_pool', 'delta_pool_scale', 'delta_lam_re', 'delta_lam_im', 'delta_log_dt', 'delta_b_re', 'delta_b_im', 'delta_c_re', 'delta_c_im', 'delta_d_skip', 'delta_w_glu', 'delta_b_glu', 'delta_pool_out_norm', 'delta_ssm_out_norm', 'delta_w_out', 'delta_ffn2_norm', 'delta_ffn2_gate', 'delta_ffn2_up', 'delta_ffn2_down', 'delta_final_norm', 'new_m_ffn1_norm', 'new_m_ffn1_gate', 'new_m_ffn1_up', 'new_m_ffn1_down', 'new_m_mix_norm', 'new_m_w_in', 'new_m_w_pool', 'new_m_pool_scale', 'new_m_lam_re', 'new_m_lam_im', 'new_m_log_dt', 'new_m_b_re', 'new_m_b_im', 'new_m_c_re', 'new_m_c_im', 'new_m_d_skip', 'new_m_w_glu', 'new_m_b_glu', 'new_m_pool_out_norm', 'new_m_ssm_out_norm', 'new_m_w_out', 'new_m_ffn2_norm', 'new_m_ffn2_gate', 'new_m_ffn2_up', 'new_m_ffn2_down', 'new_m_final_norm', 'new_v_ffn1_norm', 'new_v_ffn1_gate', 'new_v_ffn1_up', 'new_v_ffn1_down', 'new_v_mix_norm', 'new_v_w_in', 'new_v_w_pool', 'new_v_pool_scale', 'new_v_lam_re', 'new_v_lam_im', 'new_v_log_dt', 'new_v_b_re', 'new_v_b_im', 'new_v_c_re', 'new_v_c_im', 'new_v_d_skip', 'new_v_w_glu', 'new_v_b_glu', 'new_v_pool_out_norm', 'new_v_ssm_out_norm', 'new_v_w_out', 'new_v_ffn2_norm', 'new_v_ffn2_gate', 'new_v_ffn2_up', 'new_v_ffn2_down', 'new_v_final_norm']
TWIN_LEAF_KINDS = {'loss': 'loss', 'grad_x': 'grad_x', 'grad_ffn1_norm': 'grad_w', 'grad_ffn1_gate': 'grad_w', 'grad_ffn1_up': 'grad_w', 'grad_ffn1_down': 'grad_w', 'grad_mix_norm': 'grad_w', 'grad_w_in': 'grad_w', 'grad_w_pool': 'grad_w', 'grad_pool_scale': 'grad_w', 'grad_lam_re': 'grad_w', 'grad_lam_im': 'grad_w', 'grad_log_dt': 'grad_w', 'grad_b_re': 'grad_w', 'grad_b_im': 'grad_w', 'grad_c_re': 'grad_w', 'grad_c_im': 'grad_w', 'grad_d_skip': 'grad_w', 'grad_w_glu': 'grad_w', 'grad_b_glu': 'grad_w', 'grad_pool_out_norm': 'grad_w', 'grad_ssm_out_norm': 'grad_w', 'grad_w_out': 'grad_w', 'grad_ffn2_norm': 'grad_w', 'grad_ffn2_gate': 'grad_w', 'grad_ffn2_up': 'grad_w', 'grad_ffn2_down': 'grad_w', 'grad_final_norm': 'grad_w', 'delta_ffn1_norm': 'delta_w', 'delta_ffn1_gate': 'delta_w', 'delta_ffn1_up': 'delta_w', 'delta_ffn1_down': 'delta_w', 'delta_mix_norm': 'delta_w', 'delta_w_in': 'delta_w', 'delta_w_pool': 'delta_w', 'delta_pool_scale': 'delta_w', 'delta_lam_re': 'delta_w', 'delta_lam_im': 'delta_w', 'delta_log_dt': 'delta_w', 'delta_b_re': 'delta_w', 'delta_b_im': 'delta_w', 'delta_c_re': 'delta_w', 'delta_c_im': 'delta_w', 'delta_d_skip': 'delta_w', 'delta_w_glu': 'delta_w', 'delta_b_glu': 'delta_w', 'delta_pool_out_norm': 'delta_w', 'delta_ssm_out_norm': 'delta_w', 'delta_w_out': 'delta_w', 'delta_ffn2_norm': 'delta_w', 'delta_ffn2_gate': 'delta_w', 'delta_ffn2_up': 'delta_w', 'delta_ffn2_down': 'delta_w', 'delta_final_norm': 'delta_w', 'new_m_ffn1_norm': 'new_m', 'new_m_ffn1_gate': 'new_m', 'new_m_ffn1_up': 'new_m', 'new_m_ffn1_down': 'new_m', 'new_m_mix_norm': 'new_m', 'new_m_w_in': 'new_m', 'new_m_w_pool': 'new_m', 'new_m_pool_scale': 'new_m', 'new_m_lam_re': 'new_m', 'new_m_lam_im': 'new_m', 'new_m_log_dt': 'new_m', 'new_m_b_re': 'new_m', 'new_m_b_im': 'new_m', 'new_m_c_re': 'new_m', 'new_m_c_im': 'new_m', 'new_m_d_skip': 'new_m', 'new_m_w_glu': 'new_m', 'new_m_b_glu': 'new_m', 'new_m_pool_out_norm': 'new_m', 'new_m_ssm_out_norm': 'new_m', 'new_m_w_out': 'new_m', 'new_m_ffn2_norm': 'new_m', 'new_m_ffn2_gate': 'new_m', 'new_m_ffn2_up': 'new_m', 'new_m_ffn2_down': 'new_m', 'new_m_final_norm': 'new_m', 'new_v_ffn1_norm': 'new_v', 'new_v_ffn1_gate': 'new_v', 'new_v_ffn1_up': 'new_v', 'new_v_ffn1_down': 'new_v', 'new_v_mix_norm': 'new_v', 'new_v_w_in': 'new_v', 'new_v_w_pool': 'new_v', 'new_v_pool_scale': 'new_v', 'new_v_lam_re': 'new_v', 'new_v_lam_im': 'new_v', 'new_v_log_dt': 'new_v', 'new_v_b_re': 'new_v', 'new_v_b_im': 'new_v', 'new_v_c_re': 'new_v', 'new_v_c_im': 'new_v', 'new_v_d_skip': 'new_v', 'new_v_w_glu': 'new_v', 'new_v_b_glu': 'new_v', 'new_v_pool_out_norm': 'new_v', 'new_v_ssm_out_norm': 'new_v', 'new_v_w_out': 'new_v', 'new_v_ffn2_norm': 'new_v', 'new_v_ffn2_gate': 'new_v', 'new_v_ffn2_up': 'new_v', 'new_v_ffn2_down': 'new_v', 'new_v_final_norm': 'new_v'}


def _forward(args):
    return _fwd_reference(*[args[k] for k in FWD_PARAMS])


def _output_shape():
    def fwd():
        inp = _fwd_setup_inputs(0)
        return _fwd_reference(*[inp[k] for k in FWD_PARAMS])
    out = _jax.eval_shape(fwd)
    return out.shape, out.dtype

N_MICROBATCH = 1
ADAM_LR = 0.001
ADAM_B1 = 0.9
ADAM_B2 = 0.999
ADAM_EPS = 1e-08
ADAM_WD = 0.01
ADAM_STEP = 10
PER_EXAMPLE_BATCH_AXIS = {'x': 0, 'loss_target': 0}
SHARED_INPUTS = []
_WEIGHT_DTYPES = {'ffn1_norm': _jnp.float32, 'ffn1_gate': _jnp.float32, 'ffn1_up': _jnp.float32, 'ffn1_down': _jnp.float32, 'mix_norm': _jnp.float32, 'w_in': _jnp.float32, 'w_pool': _jnp.float32, 'pool_scale': _jnp.float32, 'lam_re': _jnp.float32, 'lam_im': _jnp.float32, 'log_dt': _jnp.float32, 'b_re': _jnp.float32, 'b_im': _jnp.float32, 'c_re': _jnp.float32, 'c_im': _jnp.float32, 'd_skip': _jnp.float32, 'w_glu': _jnp.float32, 'b_glu': _jnp.float32, 'pool_out_norm': _jnp.float32, 'ssm_out_norm': _jnp.float32, 'w_out': _jnp.float32, 'ffn2_norm': _jnp.float32, 'ffn2_gate': _jnp.float32, 'ffn2_up': _jnp.float32, 'ffn2_down': _jnp.float32, 'final_norm': _jnp.float32}
MOMENT_SCALE = {'ffn1_norm': 1.900573e-02, 'ffn1_gate': 8.293669e-03, 'ffn1_up': 8.036058e-03, 'ffn1_down': 1.318229e-02, 'mix_norm': 3.297457e-02, 'w_in': 3.256982e-02, 'w_pool': 3.140403e-02, 'pool_scale': 3.108812e-02, 'lam_re': 1.639200e-03, 'lam_im': 1.536569e-03, 'log_dt': 8.906442e-01, 'b_re': 1.075960e-03, 'b_im': 1.079045e-03, 'c_re': 2.164730e-03, 'c_im': 2.165635e-03, 'd_skip': 3.998829e-02, 'w_glu': 9.518673e-03, 'b_glu': 1.555813e-02, 'pool_out_norm': 3.115774e-02, 'ssm_out_norm': 3.703471e-02, 'w_out': 3.310920e-02, 'ffn2_norm': 1.316566e-02, 'ffn2_gate': 5.633653e-03, 'ffn2_up': 5.480576e-03, 'ffn2_down': 8.969562e-03, 'final_norm': 8.018514e+00}


def _to_microbatches(a, axis):
    t = _jnp.moveaxis(a, axis, 0)
    t = t.reshape((N_MICROBATCH, t.shape[0] // N_MICROBATCH) + t.shape[1:])
    return _jnp.moveaxis(t, 1, axis + 1)


def setup_inputs(seed: int = 0) -> dict:
    inp = _fwd_setup_inputs(seed)
    key = _jax.random.fold_in(_jax.random.key(seed), 7919)
    shape, _ = _output_shape()
    out = dict(inp)
    out["loss_target"] = _jax.random.normal(_jax.random.fold_in(key, 0), shape, _jnp.float32)
    for i, name in enumerate(TWIN_WEIGHTS):
        w = inp[name].astype(_jnp.float32)
        if MOMENT_SCALE is None:
            s = _jnp.sqrt(_jnp.mean(_jnp.square(w)) + 1e-30)
        else:
            s = MOMENT_SCALE[name]
        km, kv = _jax.random.split(_jax.random.fold_in(key, i + 1))
        out[name] = w
        out["m_" + name] = s * _jax.random.normal(km, w.shape, _jnp.float32)
        out["v_" + name] = (s * s) * _jax.random.uniform(kv, w.shape, _jnp.float32, 0.5, 1.5)
    if N_MICROBATCH > 1:
        for name, axis in PER_EXAMPLE_BATCH_AXIS.items():
            out[name] = _to_microbatches(out[name], axis)
    return {'x': out['x'], 'ffn1_norm': out['ffn1_norm'], 'ffn1_gate': out['ffn1_gate'], 'ffn1_up': out['ffn1_up'], 'ffn1_down': out['ffn1_down'], 'mix_norm': out['mix_norm'], 'w_in': out['w_in'], 'w_pool': out['w_pool'], 'pool_scale': out['pool_scale'], 'lam_re': out['lam_re'], 'lam_im': out['lam_im'], 'log_dt': out['log_dt'], 'b_re': out['b_re'], 'b_im': out['b_im'], 'c_re': out['c_re'], 'c_im': out['c_im'], 'd_skip': out['d_skip'], 'w_glu': out['w_glu'], 'b_glu': out['b_glu'], 'pool_out_norm': out['pool_out_norm'], 'ssm_out_norm': out['ssm_out_norm'], 'w_out': out['w_out'], 'ffn2_norm': out['ffn2_norm'], 'ffn2_gate': out['ffn2_gate'], 'ffn2_up': out['ffn2_up'], 'ffn2_down': out['ffn2_down'], 'final_norm': out['final_norm'], 'loss_target': out['loss_target'], 'm_ffn1_norm': out['m_ffn1_norm'], 'm_ffn1_gate': out['m_ffn1_gate'], 'm_ffn1_up': out['m_ffn1_up'], 'm_ffn1_down': out['m_ffn1_down'], 'm_mix_norm': out['m_mix_norm'], 'm_w_in': out['m_w_in'], 'm_w_pool': out['m_w_pool'], 'm_pool_scale': out['m_pool_scale'], 'm_lam_re': out['m_lam_re'], 'm_lam_im': out['m_lam_im'], 'm_log_dt': out['m_log_dt'], 'm_b_re': out['m_b_re'], 'm_b_im': out['m_b_im'], 'm_c_re': out['m_c_re'], 'm_c_im': out['m_c_im'], 'm_d_skip': out['m_d_skip'], 'm_w_glu': out['m_w_glu'], 'm_b_glu': out['m_b_glu'], 'm_pool_out_norm': out['m_pool_out_norm'], 'm_ssm_out_norm': out['m_ssm_out_norm'], 'm_w_out': out['m_w_out'], 'm_ffn2_norm': out['m_ffn2_norm'], 'm_ffn2_gate': out['m_ffn2_gate'], 'm_ffn2_up': out['m_ffn2_up'], 'm_ffn2_down': out['m_ffn2_down'], 'm_final_norm': out['m_final_norm'], 'v_ffn1_norm': out['v_ffn1_norm'], 'v_ffn1_gate': out['v_ffn1_gate'], 'v_ffn1_up': out['v_ffn1_up'], 'v_ffn1_down': out['v_ffn1_down'], 'v_mix_norm': out['v_mix_norm'], 'v_w_in': out['v_w_in'], 'v_w_pool': out['v_w_pool'], 'v_pool_scale': out['v_pool_scale'], 'v_lam_re': out['v_lam_re'], 'v_lam_im': out['v_lam_im'], 'v_log_dt': out['v_log_dt'], 'v_b_re': out['v_b_re'], 'v_b_im': out['v_b_im'], 'v_c_re': out['v_c_re'], 'v_c_im': out['v_c_im'], 'v_d_skip': out['v_d_skip'], 'v_w_glu': out['v_w_glu'], 'v_b_glu': out['v_b_glu'], 'v_pool_out_norm': out['v_pool_out_norm'], 'v_ssm_out_norm': out['v_ssm_out_norm'], 'v_w_out': out['v_w_out'], 'v_ffn2_norm': out['v_ffn2_norm'], 'v_ffn2_gate': out['v_ffn2_gate'], 'v_ffn2_up': out['v_ffn2_up'], 'v_ffn2_down': out['v_ffn2_down'], 'v_final_norm': out['v_final_norm']}


def _loss(weights, diff, rest, loss_target):
    with _jax.named_scope("forward"):
        args = {**rest, TWIN_DIFF_INPUT: diff, **{k: w.astype(_WEIGHT_DTYPES[k]) for k, w in weights.items()}}
        y = _forward(args)
    with _jax.named_scope("loss_head"):
        err = _jnp.square(y.astype(_jnp.float32) - loss_target)
        return 0.5 * _jnp.sum(_jnp.mean(err, axis=-1)) if err.ndim else 0.5 * err


def _adamw(w, g, m, v):
    m = ADAM_B1 * m + (1.0 - ADAM_B1) * g
    v = ADAM_B2 * v + (1.0 - ADAM_B2) * _jnp.square(g)
    m_hat = m / (1.0 - ADAM_B1 ** ADAM_STEP)
    v_hat = v / (1.0 - ADAM_B2 ** ADAM_STEP)
    delta = -ADAM_LR * (m_hat / (_jnp.sqrt(v_hat) + ADAM_EPS) + ADAM_WD * w)
    return delta, m, v


def reference(x, ffn1_norm, ffn1_gate, ffn1_up, ffn1_down, mix_norm, w_in, w_pool, pool_scale, lam_re, lam_im, log_dt, b_re, b_im, c_re, c_im, d_skip, w_glu, b_glu, pool_out_norm, ssm_out_norm, w_out, ffn2_norm, ffn2_gate, ffn2_up, ffn2_down, final_norm, loss_target, m_ffn1_norm, m_ffn1_gate, m_ffn1_up, m_ffn1_down, m_mix_norm, m_w_in, m_w_pool, m_pool_scale, m_lam_re, m_lam_im, m_log_dt, m_b_re, m_b_im, m_c_re, m_c_im, m_d_skip, m_w_glu, m_b_glu, m_pool_out_norm, m_ssm_out_norm, m_w_out, m_ffn2_norm, m_ffn2_gate, m_ffn2_up, m_ffn2_down, m_final_norm, v_ffn1_norm, v_ffn1_gate, v_ffn1_up, v_ffn1_down, v_mix_norm, v_w_in, v_w_pool, v_pool_scale, v_lam_re, v_lam_im, v_log_dt, v_b_re, v_b_im, v_c_re, v_c_im, v_d_skip, v_w_glu, v_b_glu, v_pool_out_norm, v_ssm_out_norm, v_w_out, v_ffn2_norm, v_ffn2_gate, v_ffn2_up, v_ffn2_down, v_final_norm):
    given = dict(x=x, ffn1_norm=ffn1_norm, ffn1_gate=ffn1_gate, ffn1_up=ffn1_up, ffn1_down=ffn1_down, mix_norm=mix_norm, w_in=w_in, w_pool=w_pool, pool_scale=pool_scale, lam_re=lam_re, lam_im=lam_im, log_dt=log_dt, b_re=b_re, b_im=b_im, c_re=c_re, c_im=c_im, d_skip=d_skip, w_glu=w_glu, b_glu=b_glu, pool_out_norm=pool_out_norm, ssm_out_norm=ssm_out_norm, w_out=w_out, ffn2_norm=ffn2_norm, ffn2_gate=ffn2_gate, ffn2_up=ffn2_up, ffn2_down=ffn2_down, final_norm=final_norm, loss_target=loss_target, m_ffn1_norm=m_ffn1_norm, m_ffn1_gate=m_ffn1_gate, m_ffn1_up=m_ffn1_up, m_ffn1_down=m_ffn1_down, m_mix_norm=m_mix_norm, m_w_in=m_w_in, m_w_pool=m_w_pool, m_pool_scale=m_pool_scale, m_lam_re=m_lam_re, m_lam_im=m_lam_im, m_log_dt=m_log_dt, m_b_re=m_b_re, m_b_im=m_b_im, m_c_re=m_c_re, m_c_im=m_c_im, m_d_skip=m_d_skip, m_w_glu=m_w_glu, m_b_glu=m_b_glu, m_pool_out_norm=m_pool_out_norm, m_ssm_out_norm=m_ssm_out_norm, m_w_out=m_w_out, m_ffn2_norm=m_ffn2_norm, m_ffn2_gate=m_ffn2_gate, m_ffn2_up=m_ffn2_up, m_ffn2_down=m_ffn2_down, m_final_norm=m_final_norm, v_ffn1_norm=v_ffn1_norm, v_ffn1_gate=v_ffn1_gate, v_ffn1_up=v_ffn1_up, v_ffn1_down=v_ffn1_down, v_mix_norm=v_mix_norm, v_w_in=v_w_in, v_w_pool=v_w_pool, v_pool_scale=v_pool_scale, v_lam_re=v_lam_re, v_lam_im=v_lam_im, v_log_dt=v_log_dt, v_b_re=v_b_re, v_b_im=v_b_im, v_c_re=v_c_re, v_c_im=v_c_im, v_d_skip=v_d_skip, v_w_glu=v_w_glu, v_b_glu=v_b_glu, v_pool_out_norm=v_pool_out_norm, v_ssm_out_norm=v_ssm_out_norm, v_w_out=v_w_out, v_ffn2_norm=v_ffn2_norm, v_ffn2_gate=v_ffn2_gate, v_ffn2_up=v_ffn2_up, v_ffn2_down=v_ffn2_down, v_final_norm=v_final_norm)
    weights = {n: given[n] for n in TWIN_WEIGHTS}
    shared = {n: given[n] for n in SHARED_INPUTS}
    per_example = {n: given[n] for n in ['x']}
    grad_fn = _jax.value_and_grad(_loss, argnums=(0, 1))

    def one_microbatch(ex, loss_target):
        ex = dict(ex)
        diff = ex.pop(TWIN_DIFF_INPUT)
        return grad_fn(weights, diff, {**shared, **ex}, loss_target)

    if N_MICROBATCH == 1:
        loss, (grad_w, grad_x) = one_microbatch(per_example, given["loss_target"])
    else:
        def body(carry, xs):
            loss_sum, grad_sum = carry
            l_k, (gw_k, gx_k) = one_microbatch(xs[0], xs[1])
            with _jax.named_scope("update"):
                return (loss_sum + l_k, _jax.tree.map(_jnp.add, grad_sum, gw_k)), gx_k

        init = (_jnp.zeros((), _jnp.float32), _jax.tree.map(_jnp.zeros_like, weights))
        (loss, grad_w), grad_x = _jax.lax.scan(body, init, (per_example, given["loss_target"]))
    with _jax.named_scope("update"):
        delta_w, new_m, new_v = {}, {}, {}
        for n in TWIN_WEIGHTS:
            delta_w[n], new_m[n], new_v[n] = _adamw(weights[n], grad_w[n], given["m_" + n], given["v_" + n])
    return (loss, grad_x, *[grad_w[n] for n in TWIN_WEIGHTS], *[delta_w[n] for n in TWIN_WEIGHTS],
            *[new_m[n] for n in TWIN_WEIGHTS], *[new_v[n] for n in TWIN_WEIGHTS])
```

```python
import functools
import math

import jax
import jax.numpy as jnp
from jax import lax
from jax.experimental import pallas as pl
from jax.experimental.pallas import tpu as pltpu

F32 = jnp.float32
BF16 = jnp.bfloat16
MESH = pl.DeviceIdType.MESH
ANY = pl.BlockSpec(memory_space=pl.ANY)

N_DEV = 8
V7X_LANES = 128
V7X_SUBLANES = 8
V7X_VMEM_LIMIT_BYTES = 56 * 1024 * 1024
FF_PAD_MULTIPLE = 1024
ADAMW_BLOCK_BYTES = 1 << 20

NORM_EPS = 1e-6
POOL_WINDOWS = (2, 4, 8, 16)
ADAM_LR, ADAM_B1, ADAM_B2, ADAM_EPS, ADAM_WD, ADAM_STEP = 0.001, 0.9, 0.999, 1e-08, 0.01, 10
GELU_C = math.sqrt(2.0 / math.pi)
GELU_A = 0.044715
HI = lax.Precision.HIGHEST


def _tile(dim, pref, mult):
    t = min(pref, dim)
    t -= t % mult
    while t >= mult:
        if dim % t == 0:
            return t
        t -= mult
    return dim


def _params(sem):
    return pltpu.CompilerParams(dimension_semantics=sem, vmem_limit_bytes=V7X_VMEM_LIMIT_BYTES)


def _matmul(pairs, out_dtypes, *, name, ta=False, tb=False, separate=False, epilogue=None,
            extras=(), bcast=(), tm=512, tn=512, tk=4096, precision=None):
    a0, b0 = pairs[0]
    m_dim, k_dim = (a0.shape[1], a0.shape[0]) if ta else a0.shape
    n_dim = b0.shape[0] if tb else b0.shape[1]
    tm = _tile(m_dim, tm, V7X_LANES if ta else 16)
    tn = _tile(n_dim, tn, V7X_LANES)
    tk = _tile(k_dim, tk, V7X_LANES)
    nk = k_dim // tk
    n_acc = len(pairs) if separate else 1
    n_ex, n_bc, n_out = len(extras), len(bcast), len(out_dtypes)
    dims = (((0 if ta else 1,), (1 if tb else 0,)), ((), ()))
    if epilogue is None:
        epilogue = lambda accs, ex, bc: tuple(accs)
    a_spec = pl.BlockSpec((tk, tm), lambda m, n, k: (k, m)) if ta else pl.BlockSpec((tm, tk), lambda m, n, k: (m, k))
    b_spec = pl.BlockSpec((tn, tk), lambda m, n, k: (n, k)) if tb else pl.BlockSpec((tk, tn), lambda m, n, k: (k, n))
    operands, operand_specs, pair_slots = [], [], []
    for pair in pairs:
        slots = []
        for arr, spec in zip(pair, (a_spec, b_spec)):
            found = [i for i, o in enumerate(operands) if o is arr]
            if not found:
                operands.append(arr)
                operand_specs.append(spec)
                found = [len(operands) - 1]
            slots.append(found[0])
        pair_slots.append(slots)
    n_ops = len(operands)

    def body(*refs):
        a_refs = [refs[sa] for sa, _ in pair_slots]
        b_refs = [refs[sb] for _, sb in pair_slots]
        ex_refs = refs[n_ops:n_ops + n_ex]
        bc_refs = refs[n_ops + n_ex:n_ops + n_ex + n_bc]
        out_refs = refs[n_ops + n_ex + n_bc:n_ops + n_ex + n_bc + n_out]
        acc_refs = refs[n_ops + n_ex + n_bc + n_out:]
        parts = [lax.dot_general(a[...], b[...], dims, preferred_element_type=F32, precision=precision)
                 for a, b in zip(a_refs, b_refs)]
        if not separate:
            parts = [functools.reduce(lambda p, q: p + q, parts)]

        def finish(accs):
            outs = epilogue(accs, [e[...] for e in ex_refs], [c[...] for c in bc_refs])
            for o_ref, o in zip(out_refs, outs):
                o_ref[...] = o.astype(o_ref.dtype)

        if nk == 1:
            finish(parts)
        else:
            k = pl.program_id(2)

            @pl.when(k == 0)
            def _():
                for acc, p in zip(acc_refs, parts):
                    acc[...] = p

            @pl.when(k > 0)
            def _():
                for acc, p in zip(acc_refs, parts):
                    acc[...] += p

            @pl.when(k == nk - 1)
            def _():
                finish([acc[...] for acc in acc_refs])

    mn_spec = pl.BlockSpec((tm, tn), lambda m, n, k: (m, n))
    bc_spec = pl.BlockSpec((1, tn), lambda m, n, k: (0, n))
    outs = pl.pallas_call(
        body,
        name=name,
        grid=(m_dim // tm, n_dim // tn, nk),
        in_specs=operand_specs + [mn_spec] * n_ex + [bc_spec] * n_bc,
        out_specs=[mn_spec] * n_out,
        out_shape=[jax.ShapeDtypeStruct((m_dim, n_dim), d) for d in out_dtypes],
        scratch_shapes=[pltpu.VMEM((tm, tn), F32)] * (n_acc if nk > 1 else 0),
        compiler_params=_params(("parallel", "parallel", "arbitrary")),
    )(*operands, *extras, *bcast)
    return outs[0] if n_out == 1 else outs


def _bd(pairs, out_dtype, *, name, offs=None, tm=512, precision=None):
    n_blocks, ka, kb = pairs[0][1].shape
    t_dim = pairs[0][0].shape[0]
    tm = _tile(t_dim, tm, 16)
    offs = offs or [0] * len(pairs)
    n_pairs = len(pairs)

    def body(*refs):
        acc = None
        for i in range(n_pairs):
            p = jnp.dot(refs[2 * i][...], refs[2 * i + 1][...], preferred_element_type=F32, precision=precision)
            acc = p if acc is None else acc + p
        refs[2 * n_pairs][...] = acc.astype(out_dtype)

    in_specs = []
    for off in offs:
        in_specs.append(pl.BlockSpec((tm, ka), lambda j, t, off=off: (t, j + off)))
        in_specs.append(pl.BlockSpec((None, ka, kb), lambda j, t: (j, 0, 0)))
    return pl.pallas_call(
        body,
        name=name,
        grid=(n_blocks, t_dim // tm),
        in_specs=in_specs,
        out_specs=pl.BlockSpec((tm, kb), lambda j, t: (t, j)),
        out_shape=jax.ShapeDtypeStruct((t_dim, n_blocks * kb), out_dtype),
        compiler_params=_params(("parallel", "parallel")),
    )(*[t for p in pairs for t in p])


def _bd_wgrad(a, b, ka, kb, n_blocks, *, name, off_a=0, off_b=0, tm=512, precision=None, sign=1.0):
    t_dim = a.shape[0]
    tm = _tile(t_dim, tm, 16)

    def body(a_ref, b_ref, o_ref):
        p = lax.dot_general(a_ref[...], b_ref[...], (((0,), (0,)), ((), ())), preferred_element_type=F32,
                            precision=precision)
        if sign != 1.0:
            p = p * sign

        @pl.when(pl.program_id(1) == 0)
        def _():
            o_ref[...] = p

        @pl.when(pl.program_id(1) > 0)
        def _():
            o_ref[...] += p

    return pl.pallas_call(
        body,
        name=name,
        grid=(n_blocks, t_dim // tm),
        in_specs=[pl.BlockSpec((tm, ka), lambda j, t: (t, j + off_a)),
                  pl.BlockSpec((tm, kb), lambda j, t: (t, j + off_b))],
        out_specs=pl.BlockSpec((None, ka, kb), lambda j, t: (j, 0, 0)),
        out_shape=jax.ShapeDtypeStruct((n_blocks, ka, kb), F32),
        compiler_params=_params(("parallel", "arbitrary")),
    )(a, b)


def _rowk(fn, rows, bcast, outs, *, name, tm=256):
    rows = [r if isinstance(r, tuple) else (r, r.shape[1], 0) for r in rows]
    t_dim = rows[0][0].shape[0]
    tm = _tile(t_dim, tm, 16)
    n_rows, n_bc = len(rows), len(bcast)

    def body(*refs):
        ins = [r[...] for r in refs[:n_rows + n_bc]]
        vals = fn(*ins)
        first = pl.program_id(0) == 0
        for o_ref, v, spec in zip(refs[n_rows + n_bc:], vals, outs):
            if spec[0] == "row":
                o_ref[...] = v.astype(o_ref.dtype)
            else:
                @pl.when(first)
                def _(o_ref=o_ref, v=v):
                    o_ref[...] = v

                @pl.when(jnp.logical_not(first))
                def _(o_ref=o_ref, v=v):
                    o_ref[...] += v

    in_specs = [pl.BlockSpec((tm, w), lambda i, cb=cb: (i, cb)) for _, w, cb in rows]
    in_specs += [pl.BlockSpec((1, b.shape[1]), lambda i: (0, 0)) for b in bcast]
    out_specs, out_shape = [], []
    for spec in outs:
        if spec[0] == "row":
            out_specs.append(pl.BlockSpec((tm, spec[1]), lambda i: (i, 0)))
            out_shape.append(jax.ShapeDtypeStruct((t_dim, spec[1]), spec[2]))
        else:
            out_specs.append(pl.BlockSpec((1, spec[1]), lambda i: (0, 0)))
            out_shape.append(jax.ShapeDtypeStruct((1, spec[1]), F32))
    return pl.pallas_call(
        body,
        name=name,
        grid=(t_dim // tm,),
        in_specs=in_specs,
        out_specs=out_specs,
        out_shape=out_shape,
        compiler_params=_params(("arbitrary",)),
    )(*[r[0] for r in rows], *bcast)


def _colsum(v):
    return jnp.sum(v, axis=0, keepdims=True)


def _rms_fwd(x, g, *, name):
    def fn(x, g):
        r = lax.rsqrt(jnp.mean(x * x, axis=-1, keepdims=True) + NORM_EPS)
        return (x * r * g).astype(BF16), r
    d = x.shape[1]
    return _rowk(fn, [x], [g], [("row", d, BF16), ("row", 1, F32)], name=name)


def _rms_bwd_rows(dn, x, r, g):
    xh = x * r
    dy = dn * g
    dx = r * (dy - xh * jnp.mean(dy * xh, axis=-1, keepdims=True))
    return dx, _colsum(dn * xh)


def _silu_parts(g):
    sg = jax.nn.sigmoid(g)
    return g * sg, sg * (1.0 + g * (1.0 - sg))


def _ffn_fwd(h, norm_g, w_gate_t, w_up_t, w_down, tag):
    n, r = _rms_fwd(h, norm_g, name=f"{tag}_rms")

    def gate_up(accs, ex, bc):
        g, u = accs
        return g, u, _silu_parts(g)[0] * u

    g, u, a = _matmul([(n, w_gate_t), (n, w_up_t)], [BF16, BF16, BF16], name=f"{tag}_gateup", tb=True,
                      separate=True, epilogue=gate_up)
    out = _matmul([(a, w_down)], [F32], name=f"{tag}_down", extras=[h],
                  epilogue=lambda accs, ex, bc: (ex[0] + 0.5 * accs[0],))
    return out, (h, n, r, g, u, a)


def _ffn_bwd(dh, dyb, saved, norm_g, w_gate_t, w_up_t, w_down, tag):
    h, n, r, g, u, a = saved

    def act_bwd(accs, ex, bc):
        da, g, u = accs[0], ex[0].astype(F32), ex[1].astype(F32)
        silu, dsilu = _silu_parts(g)
        return da * u * dsilu, da * silu

    dg, du = _matmul([(dyb, w_down)], [BF16, BF16], name=f"{tag}_dact", tb=True, extras=[g, u], epilogue=act_bwd)
    dn = _matmul([(dg, w_gate_t), (du, w_up_t)], [F32], name=f"{tag}_dn")
    d_gate_t = _matmul([(dg, n)], [BF16], name=f"{tag}_dwgate", ta=True)
    d_up_t = _matmul([(du, n)], [BF16], name=f"{tag}_dwup", ta=True)
    d_down = _matmul([(a, dyb)], [BF16], name=f"{tag}_dwdown", ta=True)

    def fn(dh, dn, h, r, gain):
        dx, dgain = _rms_bwd_rows(dn, h, r, gain)
        dx = dh + dx
        return dx, (0.5 * dx).astype(BF16), dgain

    d = h.shape[1]
    dx, dxb, d_norm = _rowk(fn, [dh, dn, h, r], [norm_g], [("row", d, F32), ("row", d, BF16), ("sum", d)],
                            name=f"{tag}_rms_bwd")
    return dx, dxb, d_norm, d_gate_t, d_up_t, d_down


def _pool_window(z, seq, width, group_width, *, name, transpose):
    assert POOL_WINDOWS == (2, 4, 8, 16)
    t_dim = z.shape[0]
    tc = _tile(group_width, 256, V7X_LANES)
    per_group = group_width // tc

    def body(z_ref, o_ref):
        gid = pl.program_id(1) // per_group
        v = z_ref[...]
        t = lax.broadcasted_iota(jnp.int32, v.shape, 0)
        win = jnp.where(gid == 0, 2, jnp.where(gid == 1, 4, jnp.where(gid == 2, 8, 16)))
        cnt = jnp.minimum(t + 1, win).astype(F32)
        if transpose:
            e = v / cnt
            shift = lambda q, k: jnp.where(t < seq - k, pltpu.roll(q, seq - k, 0), 0.0)
        else:
            e = v
            shift = lambda q, k: jnp.where(t >= k, pltpu.roll(q, k, 0), 0.0)
        s1 = e + shift(e, 1)
        s2 = s1 + shift(s1, 2)
        s3 = s2 + shift(s2, 4)
        s4 = s3 + shift(s3, 8)
        s = jnp.where(gid == 0, s1, jnp.where(gid == 1, s2, jnp.where(gid == 2, s3, s4)))
        if transpose:
            o_ref[...] = (s - v).astype(o_ref.dtype)
        else:
            o_ref[...] = (s / cnt - v).astype(o_ref.dtype)

    spec = pl.BlockSpec((seq, tc), lambda b, c: (b, c))
    return pl.pallas_call(
        body,
        name=name,
        grid=(t_dim // seq, width // tc),
        in_specs=[spec],
        out_specs=spec,
        out_shape=jax.ShapeDtypeStruct((t_dim, width), F32 if transpose else BF16),
        compiler_params=_params(("parallel", "parallel")),
    )(z)


def _discretize(lam_re, lam_im, log_dt, b_re_t, b_im_t):
    dt = jnp.exp(log_dt)
    mag = jnp.exp(lam_re * dt)
    a_re = mag * jnp.cos(lam_im * dt)
    a_im = mag * jnp.sin(lam_im * dt)
    den = lam_re * lam_re + lam_im * lam_im
    f_re = ((a_re - 1.0) * lam_re + a_im * lam_im) / den
    f_im = (a_im * lam_re - (a_re - 1.0) * lam_im) / den
    f_re, f_im = f_re[:, None, :], f_im[:, None, :]
    return a_re, a_im, f_re * b_re_t - f_im * b_im_t, f_re * b_im_t + f_im * b_re_t


def _discretize_fwd(params):
    shapes = [jax.ShapeDtypeStruct(params[0].shape, F32)] * 2 + [jax.ShapeDtypeStruct(params[3].shape, F32)] * 2

    def body(*refs):
        outs = _discretize(*[r[...] for r in refs[:5]])
        for o_ref, o in zip(refs[5:], outs):
            o_ref[...] = o

    return pl.pallas_call(body, name="s5_discretize", out_shape=shapes,
                          compiler_params=pltpu.CompilerParams(vmem_limit_bytes=V7X_VMEM_LIMIT_BYTES))(*params)


def _discretize_bwd(params, cots):
    shapes = [jax.ShapeDtypeStruct(p.shape, F32) for p in params]

    def body(*refs):
        _, vjp = jax.vjp(_discretize, *[r[...] for r in refs[:5]])
        grads = vjp(tuple(r[...] for r in refs[5:9]))
        for o_ref, o in zip(refs[9:], grads):
            o_ref[...] = o

    return pl.pallas_call(body, name="s5_discretize_bwd", out_shape=shapes,
                          compiler_params=pltpu.CompilerParams(vmem_limit_bytes=V7X_VMEM_LIMIT_BYTES))(*params, *cots)


def _scan_tiles(t_dim, n_dim, seq):
    return _tile(seq, 256, V7X_SUBLANES), _tile(n_dim, 1024, V7X_LANES)


def _scan_fwd(bu_re, bu_im, a_re, a_im, seq):
    t_dim, n_dim = bu_re.shape
    tt, tl = _scan_tiles(t_dim, n_dim, seq)
    per_seq = seq // tt

    def body(br_ref, bi_ref, ar_ref, ai_ref, xr_ref, xi_ref, cr_ref, ci_ref):
        @pl.when(pl.program_id(1) % per_seq == 0)
        def _():
            cr_ref[...] = jnp.zeros_like(cr_ref)
            ci_ref[...] = jnp.zeros_like(ci_ref)

        ar, ai = ar_ref[...], ai_ref[...]

        def step(i, carry):
            xr, xi = carry
            row = pl.ds(i, 1)
            nr = ar * xr - ai * xi + br_ref[row, :]
            ni = ai * xr + ar * xi + bi_ref[row, :]
            xr_ref[row, :] = nr
            xi_ref[row, :] = ni
            return nr, ni

        xr, xi = lax.fori_loop(0, tt, step, (cr_ref[...], ci_ref[...]), unroll=8)
        cr_ref[...] = xr
        ci_ref[...] = xi

    blk = pl.BlockSpec((tt, tl), lambda l, t: (t, l))
    vec = pl.BlockSpec((1, tl), lambda l, t: (0, l))
    return pl.pallas_call(
        body,
        name="s5_scan",
        grid=(n_dim // tl, t_dim // tt),
        in_specs=[blk, blk, vec, vec],
        out_specs=[blk, blk],
        out_shape=[jax.ShapeDtypeStruct((t_dim, n_dim), F32)] * 2,
        scratch_shapes=[pltpu.VMEM((1, tl), F32)] * 2,
        compiler_params=_params(("parallel", "arbitrary")),
    )(bu_re, bu_im, a_re, a_im)


def _scan_bwd(gx_re, gx_im, x_re, x_im, a_re, a_im, seq):
    t_dim, n_dim = gx_re.shape
    tt, tl = _scan_tiles(t_dim, n_dim, seq)
    per_seq = seq // tt
    n_t = t_dim // tt
    prev_rows = V7X_SUBLANES

    def body(gr_ref, gi_ref, xr_ref, xi_ref, pr_ref, pi_ref, ar_ref, ai_ref,
             lr_ref, li_ref, dar_ref, dai_ref, cr_ref, ci_ref):
        step_id = pl.program_id(1)
        blk_id = n_t - 1 - step_id

        @pl.when((blk_id + 1) % per_seq == 0)
        def _():
            cr_ref[...] = jnp.zeros_like(cr_ref)
            ci_ref[...] = jnp.zeros_like(ci_ref)

        ar, ai = ar_ref[...], ai_ref[...]

        def step(j, carry):
            lr, li = carry
            row = pl.ds(tt - 1 - j, 1)
            nr = gr_ref[row, :] + ar * lr + ai * li
            ni = gi_ref[row, :] - ai * lr + ar * li
            lr_ref[row, :] = nr
            li_ref[row, :] = ni
            return nr, ni

        lr, li = lax.fori_loop(0, tt, step, (cr_ref[...], ci_ref[...]), unroll=8)
        cr_ref[...] = lr
        ci_ref[...] = li

        first_of_seq = blk_id % per_seq == 0
        keep = jnp.where(first_of_seq, 0.0, 1.0)
        t = lax.broadcasted_iota(jnp.int32, (tt, tl), 0)
        xr_prev = jnp.where(t == 0, pr_ref[prev_rows - 1:prev_rows, :] * keep, pltpu.roll(xr_ref[...], 1, 0))
        xi_prev = jnp.where(t == 0, pi_ref[prev_rows - 1:prev_rows, :] * keep, pltpu.roll(xi_ref[...], 1, 0))
        lam_r, lam_i = lr_ref[...], li_ref[...]
        d_re = _colsum(lam_r * xr_prev + lam_i * xi_prev)
        d_im = _colsum(lam_i * xr_prev - lam_r * xi_prev)

        @pl.when(step_id == 0)
        def _():
            dar_ref[...] = d_re
            dai_ref[...] = d_im

        @pl.when(step_id > 0)
        def _():
            dar_ref[...] += d_re
            dai_ref[...] += d_im

    blk = pl.BlockSpec((tt, tl), lambda l, t: (n_t - 1 - t, l))
    prev = pl.BlockSpec((prev_rows, tl), lambda l, t: (jnp.maximum((n_t - 1 - t) * (tt // prev_rows) - 1, 0), l))
    vec = pl.BlockSpec((1, tl), lambda l, t: (0, l))
    return pl.pallas_call(
        body,
        name="s5_scan_bwd",
        grid=(n_dim // tl, n_t),
        in_specs=[blk, blk, blk, blk, prev, prev, vec, vec],
        out_specs=[blk, blk, vec, vec],
        out_shape=[jax.ShapeDtypeStruct((t_dim, n_dim), F32)] * 2 + [jax.ShapeDtypeStruct((1, n_dim), F32)] * 2,
        scratch_shapes=[pltpu.VMEM((1, tl), F32)] * 2,
        compiler_params=_params(("parallel", "arbitrary")),
    )(gx_re, gx_im, x_re, x_im, x_re, x_im, a_re, a_im)


def _block_diag(w, per_block):
    g, ka, kb = w.shape
    eye = jnp.eye(per_block, dtype=w.dtype)
    out = jnp.einsum("jakb,ac->jakcb", w.reshape(g // per_block, per_block, ka, kb), eye)
    return out.reshape(g // per_block, per_block * ka, per_block * kb)


def _block_diag_t(d, per_block, ka, kb):
    j = d.shape[0]
    d5 = d.reshape(j, per_block, ka, per_block, kb)
    idx = jnp.arange(per_block)
    picked = d5[:, idx, :, idx, :]
    return jnp.transpose(picked, (1, 0, 2, 3)).reshape(j * per_block, ka, kb)


def _gelu_parts(y):
    inner = GELU_C * (y + GELU_A * y * y * y)
    th = jnp.tanh(inner)
    val = 0.5 * y * (1.0 + th)
    grad = 0.5 * (1.0 + th) + 0.5 * y * (1.0 - th * th) * GELU_C * (1.0 + 3.0 * GELU_A * y * y)
    return val, grad


def _my_place():
    return lax.axis_index("x"), lax.axis_index("y"), lax.axis_index("c")


def _block_index(px, py, pc):
    return 4 * px + 2 * py + pc


def _all_gather_weights(shards, pads):
    n_arr = len(shards)

    def full_shape(s, pad):
        if s.ndim == 2:
            return (N_DEV * s.shape[0] + pad, s.shape[1])
        return (s.shape[0], N_DEV * s.shape[1], s.shape[2])

    out_shape = [jax.ShapeDtypeStruct(full_shape(s, p), s.dtype) for s, p in zip(shards, pads)]
    zero_blocks = [jnp.zeros((p, s.shape[1]), s.dtype) for s, p in zip(shards, pads) if p]
    n_zero = len(zero_blocks)

    def body(*refs):
        in_refs = refs[:n_arr]
        zero_refs = refs[n_arr:n_arr + n_zero]
        out_refs = refs[n_arr + n_zero:2 * n_arr + n_zero]
        send_sems, recv_sems, local_sems = refs[2 * n_arr + n_zero:]
        x, y, c = _my_place()
        me, sibling = (x, y, c), (x, y, 1 - c)
        chips = [(1 - x, y), (x, 1 - y), (1 - x, 1 - y)]

        def rows(i, place):
            idx = _block_index(*place)
            r = in_refs[i].shape[-2]
            if in_refs[i].ndim == 2:
                return out_refs[i].at[pl.ds(idx * r, r), :]
            return out_refs[i].at[:, pl.ds(idx * r, r), :]

        def copy(i, k, block, to, src=None):
            return pltpu.make_async_remote_copy(
                src_ref=rows(i, block) if src is None else src, dst_ref=rows(i, block),
                send_sem=send_sems.at[7 * i + k], recv_sem=recv_sems.at[7 * i + k],
                device_id=to, device_id_type=MESH)

        local = [pltpu.make_async_copy(in_refs[i], rows(i, me), local_sems.at[i]) for i in range(n_arr)]
        zi = 0
        for i in range(n_arr):
            if pads[i]:
                start = N_DEV * in_refs[i].shape[0]
                local.append(pltpu.make_async_copy(zero_refs[zi], out_refs[i].at[pl.ds(start, pads[i]), :],
                                                   local_sems.at[n_arr + zi]))
                zi += 1
        for cp in local:
            cp.start()
        first = []
        for i in range(n_arr):
            first.append(copy(i, 0, me, sibling, src=in_refs[i]))
            first += [copy(i, 1 + j, me, (*chip, c), src=in_refs[i]) for j, chip in enumerate(chips)]
        for cp in first:
            cp.start()
        passed = []
        for j, chip in enumerate(chips):
            for i in range(n_arr):
                copy(i, 1 + j, (*chip, c), me).wait_recv()
                fwd = copy(i, 4 + j, (*chip, c), sibling)
                fwd.start()
                passed.append(fwd)
        for i in range(n_arr):
            copy(i, 0, sibling, me).wait_recv()
        for j, chip in enumerate(chips):
            for i in range(n_arr):
                copy(i, 4 + j, (*chip, 1 - c), me).wait_recv()
        for cp in first + passed:
            cp.wait_send()
        for cp in local:
            cp.wait()

    return pl.pallas_call(
        body,
        name="weights_all_gather",
        in_specs=[ANY] * (n_arr + n_zero),
        out_specs=[ANY] * n_arr,
        out_shape=out_shape,
        scratch_shapes=[pltpu.SemaphoreType.DMA((7 * n_arr,)), pltpu.SemaphoreType.DMA((7 * n_arr,)),
                        pltpu.SemaphoreType.DMA((n_arr + n_zero,))],
    )(*shards, *zero_blocks)


_FLIPS = [(0, 0, 1), (0, 1, 0), (0, 1, 1), (1, 0, 0), (1, 0, 1), (1, 1, 0), (1, 1, 1)]


def _exchange(arrays, shard_rows, *, name):
    n_arr = len(arrays)
    rows_of = [a.shape[0] if r is None else r for a, r in zip(arrays, shard_rows)]
    out_shape = [jax.ShapeDtypeStruct((N_DEV, r, a.shape[1]), a.dtype) for a, r in zip(arrays, rows_of)]

    def body(*refs):
        in_refs = refs[:n_arr]
        out_refs = refs[n_arr:2 * n_arr]
        send_sems, recv_sems, local_sems = refs[2 * n_arr:]
        x, y, c = _my_place()
        my_idx = _block_index(x, y, c)

        def part(i, idx):
            if shard_rows[i] is None:
                return in_refs[i]
            return in_refs[i].at[pl.ds(idx * rows_of[i], rows_of[i]), :]

        local = [pltpu.make_async_copy(part(i, my_idx), out_refs[i].at[my_idx], local_sems.at[i]) for i in range(n_arr)]
        for cp in local:
            cp.start()
        sends = []
        for i in range(n_arr):
            for k, (fx, fy, fc) in enumerate(_FLIPS):
                peer = (x ^ fx, y ^ fy, c ^ fc)
                sends.append(pltpu.make_async_remote_copy(
                    src_ref=part(i, _block_index(*peer)), dst_ref=out_refs[i].at[my_idx],
                    send_sem=send_sems.at[7 * i + k], recv_sem=recv_sems.at[7 * i + k],
                    device_id=peer, device_id_type=MESH))
        for cp in sends:
            cp.start()
        for i in range(n_arr):
            for k, (fx, fy, fc) in enumerate(_FLIPS):
                peer = (x ^ fx, y ^ fy, c ^ fc)
                pltpu.make_async_remote_copy(
                    src_ref=part(i, my_idx), dst_ref=out_refs[i].at[_block_index(*peer)],
                    send_sem=send_sems.at[7 * i + k], recv_sem=recv_sems.at[7 * i + k],
                    device_id=peer, device_id_type=MESH).wait_recv()
        for cp in sends:
            cp.wait_send()
        for cp in local:
            cp.wait()

    return pl.pallas_call(
        body,
        name=name,
        in_specs=[ANY] * n_arr,
        out_specs=[ANY] * n_arr,
        out_shape=out_shape,
        scratch_shapes=[pltpu.SemaphoreType.DMA((7 * n_arr,)), pltpu.SemaphoreType.DMA((7 * n_arr,)),
                        pltpu.SemaphoreType.DMA((n_arr,))],
    )(*arrays)


def _sum_parts(parts, *, name):
    _, r, c = parts.shape
    tr = _tile(r, 704, 16)
    tc = _tile(c, 1024, V7X_LANES)

    def body(p_ref, o_ref):
        acc = p_ref[0].astype(F32)
        for s in range(1, N_DEV):
            acc = acc + p_ref[s].astype(F32)
        o_ref[...] = acc

    return pl.pallas_call(
        body,
        name=name,
        grid=(r // tr, c // tc),
        in_specs=[pl.BlockSpec((N_DEV, tr, tc), lambda i, j: (0, i, j))],
        out_specs=pl.BlockSpec((tr, tc), lambda i, j: (i, j)),
        out_shape=jax.ShapeDtypeStruct((r, c), F32),
        compiler_params=_params(("parallel", "parallel")),
    )(parts)


def _adamw(w, g, m, v, *, name):
    r, c = w.shape
    tr = _tile(r, max(V7X_SUBLANES, ADAMW_BLOCK_BYTES // (4 * c)), V7X_SUBLANES)
    c1 = 1.0 / (1.0 - ADAM_B1 ** ADAM_STEP)
    c2 = 1.0 / (1.0 - ADAM_B2 ** ADAM_STEP)

    def body(w_ref, g_ref, m_ref, v_ref, d_ref, nm_ref, nv_ref):
        g = g_ref[...]
        nm = ADAM_B1 * m_ref[...] + (1.0 - ADAM_B1) * g
        nv = ADAM_B2 * v_ref[...] + (1.0 - ADAM_B2) * (g * g)
        m_hat = nm * c1
        v_hat = nv * c2
        d_ref[...] = -ADAM_LR * (m_hat / (jnp.sqrt(v_hat) + ADAM_EPS) + ADAM_WD * w_ref[...])
        nm_ref[...] = nm
        nv_ref[...] = nv

    spec = pl.BlockSpec((tr, c), lambda i: (i, 0))
    return pl.pallas_call(
        body,
        name=name,
        grid=(r // tr,),
        in_specs=[spec] * 4,
        out_specs=[spec] * 3,
        out_shape=[jax.ShapeDtypeStruct((r, c), F32)] * 3,
        compiler_params=_params(("parallel",)),
    )(w, g, m, v)


def _pack(arrays, width=V7X_LANES):
    tile = V7X_SUBLANES * width
    parts, layout, row = [], [], 0
    for a in arrays:
        n = a.size
        rows = -(-n // tile) * V7X_SUBLANES
        flat = jnp.pad(a.reshape(-1).astype(F32), (0, rows * width - n))
        parts.append(flat.reshape(rows, width))
        layout.append((row, rows, n, a.shape))
        row += rows
    return jnp.concatenate(parts, axis=0), layout


def _unpack(packed, layout):
    return [packed[row:row + rows].reshape(-1)[:n].reshape(shape) for row, rows, n, shape in layout]


def kernel(x, ffn1_norm, ffn1_gate, ffn1_up, ffn1_down, mix_norm, w_in, w_pool, pool_scale, lam_re, lam_im, log_dt, b_re, b_im, c_re, c_im, d_skip, w_glu, b_glu, pool_out_norm, ssm_out_norm, w_out, ffn2_norm, ffn2_gate, ffn2_up, ffn2_down, final_norm, loss_target, m_ffn1_norm, m_ffn1_gate, m_ffn1_up, m_ffn1_down, m_mix_norm, m_w_in, m_w_pool, m_pool_scale, m_lam_re, m_lam_im, m_log_dt, m_b_re, m_b_im, m_c_re, m_c_im, m_d_skip, m_w_glu, m_b_glu, m_pool_out_norm, m_ssm_out_norm, m_w_out, m_ffn2_norm, m_ffn2_gate, m_ffn2_up, m_ffn2_down, m_final_norm, v_ffn1_norm, v_ffn1_gate, v_ffn1_up, v_ffn1_down, v_mix_norm, v_w_in, v_w_pool, v_pool_scale, v_lam_re, v_lam_im, v_log_dt, v_b_re, v_b_im, v_c_re, v_c_im, v_d_skip, v_w_glu, v_b_glu, v_pool_out_norm, v_ssm_out_norm, v_w_out, v_ffn2_norm, v_ffn2_gate, v_ffn2_up, v_ffn2_down, v_final_norm):
    weights = dict(ffn1_norm=ffn1_norm, ffn1_gate=ffn1_gate, ffn1_up=ffn1_up, ffn1_down=ffn1_down, mix_norm=mix_norm, w_in=w_in, w_pool=w_pool, pool_scale=pool_scale, lam_re=lam_re, lam_im=lam_im, log_dt=log_dt, b_re=b_re, b_im=b_im, c_re=c_re, c_im=c_im, d_skip=d_skip, w_glu=w_glu, b_glu=b_glu, pool_out_norm=pool_out_norm, ssm_out_norm=ssm_out_norm, w_out=w_out, ffn2_norm=ffn2_norm, ffn2_gate=ffn2_gate, ffn2_up=ffn2_up, ffn2_down=ffn2_down, final_norm=final_norm)
    moments_m = dict(ffn1_norm=m_ffn1_norm, ffn1_gate=m_ffn1_gate, ffn1_up=m_ffn1_up, ffn1_down=m_ffn1_down, mix_norm=m_mix_norm, w_in=m_w_in, w_pool=m_w_pool, pool_scale=m_pool_scale, lam_re=m_lam_re, lam_im=m_lam_im, log_dt=m_log_dt, b_re=m_b_re, b_im=m_b_im, c_re=m_c_re, c_im=m_c_im, d_skip=m_d_skip, w_glu=m_w_glu, b_glu=m_b_glu, pool_out_norm=m_pool_out_norm, ssm_out_norm=m_ssm_out_norm, w_out=m_w_out, ffn2_norm=m_ffn2_norm, ffn2_gate=m_ffn2_gate, ffn2_up=m_ffn2_up, ffn2_down=m_ffn2_down, final_norm=m_final_norm)
    moments_v = dict(ffn1_norm=v_ffn1_norm, ffn1_gate=v_ffn1_gate, ffn1_up=v_ffn1_up, ffn1_down=v_ffn1_down, mix_norm=v_mix_norm, w_in=v_w_in, w_pool=v_w_pool, pool_scale=v_pool_scale, lam_re=v_lam_re, lam_im=v_lam_im, log_dt=v_log_dt, b_re=v_b_re, b_im=v_b_im, c_re=v_c_re, c_im=v_c_im, d_skip=v_d_skip, w_glu=v_w_glu, b_glu=v_b_glu, pool_out_norm=v_pool_out_norm, ssm_out_norm=v_ssm_out_norm, w_out=v_w_out, ffn2_norm=v_ffn2_norm, ffn2_gate=v_ffn2_gate, ffn2_up=v_ffn2_up, ffn2_down=v_ffn2_down, final_norm=v_final_norm)
    names = list(weights)

    n_seq, seq, d_model = x.shape
    t_dim = n_seq * seq
    ff_shard = ffn1_gate.shape[1]
    ff = N_DEV * ff_shard
    ff_pad = -(-ff // FF_PAD_MULTIPLE) * FF_PAD_MULTIPLE - ff
    n_pool, pool_gw = w_pool.shape[0], w_pool.shape[2]
    pool_w = n_pool * pool_gw
    n_grp, n_state, n_ch = b_re.shape
    ssm_w = n_grp * n_ch
    grp_per_blk = V7X_LANES // n_ch
    n_blk = n_grp // grp_per_blk
    ch_blk, st_blk = grp_per_blk * n_ch, grp_per_blk * n_state
    ssm_off = pool_w // ch_blk

    x2 = x.reshape(t_dim, d_model)
    tgt2 = loss_target.reshape(t_dim, d_model)
    row = lambda p: p.reshape(1, -1)

    sharded = ["ffn1_gate", "ffn1_up", "ffn1_down", "ffn2_gate", "ffn2_up", "ffn2_down", "w_in", "w_out", "w_glu", "w_pool"]
    transposed = {"ffn1_gate", "ffn1_up", "ffn2_gate", "ffn2_up"}
    shards = [(weights[n].T if n in transposed else weights[n]).astype(BF16) for n in sharded]
    pads = [ff_pad if n.startswith("ffn") else 0 for n in sharded]
    full = dict(zip(sharded, _all_gather_weights(shards, pads)))

    h1, saved1 = _ffn_fwd(x2, row(ffn1_norm), full["ffn1_gate"], full["ffn1_up"], full["ffn1_down"], "ffn1")
    n2, r2 = _rms_fwd(h1, row(mix_norm), name="mix_rms")
    z = _matmul([(n2, full["w_in"])], [F32], name="mix_in")

    d_pool = _pool_window(z, seq, pool_w, pool_gw, name="pool_window", transpose=False)
    y_pool_lin = _bd([(d_pool, full["w_pool"])], F32, name="pool_mix")

    disc_params = (lam_re, lam_im, log_dt.reshape(n_grp, 1), jnp.swapaxes(b_re, 1, 2), jnp.swapaxes(b_im, 1, 2))
    a_re, a_im, bb_re_t, bb_im_t = _discretize_fwd(disc_params)
    a_re_row, a_im_row = row(a_re), row(a_im)
    wb_re, wb_im = _block_diag(bb_re_t, grp_per_blk), _block_diag(bb_im_t, grp_per_blk)
    wc_re = _block_diag(jnp.swapaxes(c_re, 1, 2), grp_per_blk)
    wc_im = _block_diag(jnp.swapaxes(c_im, 1, 2), grp_per_blk)
    bu_re = _bd([(z, wb_re)], F32, name="s5_bu_re", offs=[ssm_off], precision=HI)
    bu_im = _bd([(z, wb_im)], F32, name="s5_bu_im", offs=[ssm_off], precision=HI)
    xs_re, xs_im = _scan_fwd(bu_re, bu_im, a_re_row, a_im_row, seq)
    y_lin = _bd([(xs_re, wc_re), (xs_im, -wc_im)], F32, name="s5_cx", precision=HI)

    def s5_post(y_lin, u, skip):
        y = y_lin + skip * u
        return y, _gelu_parts(y)[0].astype(BF16)

    u_cols = (z, ssm_w, pool_w // ssm_w)
    y_ssm, yg = _rowk(s5_post, [y_lin, u_cols], [row(d_skip)], [("row", ssm_w, F32), ("row", ssm_w, BF16)], name="s5_post")

    def glu(accs, ex, bc):
        q = accs[0] + bc[0]
        return q, ex[0].astype(F32) * jax.nn.sigmoid(q)

    q_glu, y_s5 = _matmul([(yg, full["w_glu"])], [F32, F32], name="s5_glu", extras=[yg], bcast=[row(b_glu)], epilogue=glu)

    def merge(yp_lin, ys, scale, gp, gs):
        yp = yp_lin * scale
        rp = lax.rsqrt(jnp.mean(yp * yp, axis=-1, keepdims=True) + NORM_EPS)
        rs = lax.rsqrt(jnp.mean(ys * ys, axis=-1, keepdims=True) + NORM_EPS)
        merged = jnp.concatenate([yp * rp * gp, ys * rs * gs], axis=-1)
        return merged.astype(BF16), rp, rs

    merged, r_pool, r_ssm = _rowk(merge, [y_pool_lin, y_s5], [row(pool_scale), row(pool_out_norm), row(ssm_out_norm)],
                                  [("row", pool_w + ssm_w, BF16), ("row", 1, F32), ("row", 1, F32)], name="mix_merge")
    h2 = _matmul([(merged, full["w_out"])], [F32], name="mix_out", extras=[h1],
                 epilogue=lambda accs, ex, bc: (ex[0] + accs[0],))
    h3, saved2 = _ffn_fwd(h2, row(ffn2_norm), full["ffn2_gate"], full["ffn2_up"], full["ffn2_down"], "ffn2")

    def head(h, tgt, gain):
        r = lax.rsqrt(jnp.mean(h * h, axis=-1, keepdims=True) + NORM_EPS)
        xh = h * r
        err = xh * gain - tgt
        loss = jnp.sum(0.5 * jnp.mean(err * err, axis=-1, keepdims=True), axis=0, keepdims=True)
        dout = err * (1.0 / d_model)
        dy = dout * gain
        dh = r * (dy - xh * jnp.mean(dy * xh, axis=-1, keepdims=True))
        return dh, (0.5 * dh).astype(BF16), _colsum(dout * xh), jnp.broadcast_to(loss, (1, V7X_LANES))

    dh3, dyb2, g_final, loss_part = _rowk(head, [h3, tgt2], [row(final_norm)],
                                          [("row", d_model, F32), ("row", d_model, BF16), ("sum", d_model), ("sum", V7X_LANES)],
                                          name="loss_head")

    grads = {}
    dh2, _, grads["ffn2_norm"], grads["ffn2_gate"], grads["ffn2_up"], grads["ffn2_down"] = _ffn_bwd(
        dh3, dyb2, saved2, row(ffn2_norm), full["ffn2_gate"], full["ffn2_up"], full["ffn2_down"], "ffn2")
    dh2b = dh2.astype(BF16)
    d_merged = _matmul([(dh2b, full["w_out"])], [F32], name="mix_out_dx", tb=True)
    grads["w_out"] = _matmul([(merged, dh2b)], [BF16], name="mix_out_dw", ta=True)

    def merge_bwd(dm, yp_lin, ys, rp, rs, scale, gp, gs):
        yp = yp_lin * scale
        d_yp, d_gp = _rms_bwd_rows(dm[:, :pool_w], yp, rp, gp)
        d_ys, d_gs = _rms_bwd_rows(dm[:, pool_w:], ys, rs, gs)
        return (d_yp * scale).astype(BF16), d_ys, _colsum(d_yp * yp_lin), d_gp, d_gs

    d_pool_lin, d_ys, grads["pool_scale"], grads["pool_out_norm"], grads["ssm_out_norm"] = _rowk(
        merge_bwd, [d_merged, y_pool_lin, y_s5, r_pool, r_ssm], [row(pool_scale), row(pool_out_norm), row(ssm_out_norm)],
        [("row", pool_w, BF16), ("row", ssm_w, F32), ("sum", pool_w), ("sum", pool_w), ("sum", ssm_w)], name="mix_merge_bwd")

    w_pool_t = jnp.swapaxes(full["w_pool"], 1, 2)
    dd_pool = _bd([(d_pool_lin, w_pool_t)], F32, name="pool_mix_dx")
    grads["w_pool"] = _bd_wgrad(d_pool, d_pool_lin, pool_gw, pool_gw, n_pool, name="pool_mix_dw")
    dz_pool = _pool_window(dd_pool, seq, pool_w, pool_gw, name="pool_window_bwd", transpose=True)

    def glu_bwd(d_ys, yg, q):
        sg = jax.nn.sigmoid(q)
        dq = d_ys * yg.astype(F32) * sg * (1.0 - sg)
        return dq.astype(BF16), d_ys * sg, _colsum(dq)

    dq, d_yg_direct, grads["b_glu"] = _rowk(glu_bwd, [d_ys, yg, q_glu], [],
                                            [("row", ssm_w, BF16), ("row", ssm_w, F32), ("sum", ssm_w)], name="s5_glu_bwd")
    d_yg_mm = _matmul([(dq, full["w_glu"])], [F32], name="s5_glu_dx", tb=True)
    grads["w_glu"] = _matmul([(yg, dq)], [BF16], name="s5_glu_dw", ta=True)

    def gelu_bwd(d1, d2, y, u, skip):
        dy = (d1 + d2) * _gelu_parts(y)[1]
        return dy, dy * skip, _colsum(dy * u)

    dy_ssm, du_skip, grads["d_skip"] = _rowk(gelu_bwd, [d_yg_direct, d_yg_mm, y_ssm, u_cols], [row(d_skip)],
                                             [("row", ssm_w, F32), ("row", ssm_w, F32), ("sum", ssm_w)], name="s5_gelu_bwd")
    wc_re_t, wc_im_t = jnp.swapaxes(wc_re, 1, 2), jnp.swapaxes(wc_im, 1, 2)
    gx_re = _bd([(dy_ssm, wc_re_t)], F32, name="s5_gx_re", precision=HI)
    gx_im = _bd([(dy_ssm, -wc_im_t)], F32, name="s5_gx_im", precision=HI)
    d_wc_re = _bd_wgrad(xs_re, dy_ssm, st_blk, ch_blk, n_blk, name="s5_dc_re", precision=HI)
    d_wc_im = _bd_wgrad(xs_im, dy_ssm, st_blk, ch_blk, n_blk, name="s5_dc_im", precision=HI, sign=-1.0)
    lm_re, lm_im, da_re, da_im = _scan_bwd(gx_re, gx_im, xs_re, xs_im, a_re_row, a_im_row, seq)
    d_wb_re = _bd_wgrad(z, lm_re, ch_blk, st_blk, n_blk, name="s5_db_re", off_a=ssm_off, precision=HI)
    d_wb_im = _bd_wgrad(z, lm_im, ch_blk, st_blk, n_blk, name="s5_db_im", off_a=ssm_off, precision=HI)
    wb_re_t, wb_im_t = jnp.swapaxes(wb_re, 1, 2), jnp.swapaxes(wb_im, 1, 2)
    du_lin = _bd([(lm_re, wb_re_t), (lm_im, wb_im_t)], F32, name="s5_du", precision=HI)
    grads["c_re"] = jnp.swapaxes(_block_diag_t(d_wc_re, grp_per_blk, n_state, n_ch), 1, 2)
    grads["c_im"] = jnp.swapaxes(_block_diag_t(d_wc_im, grp_per_blk, n_state, n_ch), 1, 2)
    d_bb_re_t = _block_diag_t(d_wb_re, grp_per_blk, n_ch, n_state)
    d_bb_im_t = _block_diag_t(d_wb_im, grp_per_blk, n_ch, n_state)
    g_lam_re, g_lam_im, g_log_dt, g_b_re_t, g_b_im_t = _discretize_bwd(
        disc_params, (da_re.reshape(n_grp, n_state), da_im.reshape(n_grp, n_state), d_bb_re_t, d_bb_im_t))
    grads["lam_re"], grads["lam_im"], grads["log_dt"] = g_lam_re, g_lam_im, g_log_dt.reshape(n_grp)
    grads["b_re"], grads["b_im"] = jnp.swapaxes(g_b_re_t, 1, 2), jnp.swapaxes(g_b_im_t, 1, 2)

    def join(dzp, du1, du2):
        return (jnp.concatenate([dzp, du1 + du2], axis=-1).astype(BF16),)

    (dz,) = _rowk(join, [dz_pool, du_lin, du_skip], [], [("row", pool_w + ssm_w, BF16)], name="mix_in_join")
    dn2 = _matmul([(dz, full["w_in"])], [F32], name="mix_in_dx", tb=True)
    grads["w_in"] = _matmul([(n2, dz)], [BF16], name="mix_in_dw", ta=True)

    def mix_rms_bwd(dh, dn, h, r, gain):
        dx, dgain = _rms_bwd_rows(dn, h, r, gain)
        dx = dh + dx
        return dx, (0.5 * dx).astype(BF16), dgain

    dh1, dyb1, grads["mix_norm"] = _rowk(mix_rms_bwd, [dh2, dn2, h1, r2], [row(mix_norm)],
                                         [("row", d_model, F32), ("row", d_model, BF16), ("sum", d_model)], name="mix_rms_bwd")
    dx, _, grads["ffn1_norm"], grads["ffn1_gate"], grads["ffn1_up"], grads["ffn1_down"] = _ffn_bwd(
        dh1, dyb1, saved1, row(ffn1_norm), full["ffn1_gate"], full["ffn1_up"], full["ffn1_down"], "ffn1")
    grads["final_norm"] = g_final

    pool_rows = pool_gw // N_DEV
    grads["w_pool"] = grads["w_pool"].reshape(n_pool, N_DEV, pool_rows, pool_gw).transpose(1, 0, 2, 3).reshape(
        N_DEV * n_pool * pool_rows, pool_gw)
    big = [grads[n].astype(BF16) for n in sharded]
    shard_rows = [weights[n].shape[1] if n in transposed else (n_pool * pool_rows if n == "w_pool" else weights[n].shape[0])
                  for n in sharded]
    parts = _exchange(big, shard_rows, name="grads_reduce_scatter")
    summed = {}
    for n, p in zip(sharded, parts):
        s = _sum_parts(p, name=f"sum_{n}")
        if n in transposed:
            s = s.T
        summed[n] = s.reshape(weights[n].shape)

    small = [n for n in names if n not in sharded]
    packed_g, layout = _pack([grads[n].reshape(weights[n].shape) for n in small] + [loss_part])
    (gathered,) = _exchange([packed_g], [None], name="small_all_gather")
    small_sum = _sum_parts(gathered, name="sum_small")
    unpacked = _unpack(small_sum, layout)
    for n, g in zip(small, unpacked[:-1]):
        summed[n] = g
    loss = unpacked[-1][0, 0]

    packed_w, _ = _pack([weights[n] for n in small] + [jnp.zeros_like(loss_part)])
    packed_m, _ = _pack([moments_m[n] for n in small] + [jnp.zeros_like(loss_part)])
    packed_v, _ = _pack([moments_v[n] for n in small] + [jnp.ones_like(loss_part)])
    small_out = [_unpack(o, layout) for o in _adamw(packed_w, small_sum, packed_m, packed_v, name="adamw_small")]
    delta, new_m, new_v = {}, {}, {}
    for i, n in enumerate(small):
        delta[n], new_m[n], new_v[n] = small_out[0][i], small_out[1][i], small_out[2][i]
    for n in sharded:
        shape = weights[n].shape
        as2d = lambda a: a.reshape(-1, shape[-1])
        d, nm, nv = _adamw(as2d(weights[n]), as2d(summed[n]), as2d(moments_m[n]), as2d(moments_v[n]), name=f"adamw_{n}")
        delta[n], new_m[n], new_v[n] = d.reshape(shape), nm.reshape(shape), nv.reshape(shape)

    return (loss, dx.reshape(x.shape), *[summed[n] for n in names], *[delta[n] for n in names],
            *[new_m[n] for n in names], *[new_v[n] for n in names])
```

```python
import functools
import math

import jax
import jax.numpy as jnp
from jax import lax
from jax.experimental import pallas as pl
from jax.experimental.pallas import tpu as pltpu

F32 = jnp.float32
BF16 = jnp.bfloat16
MESH = pl.DeviceIdType.MESH
ANY = pl.BlockSpec(memory_space=pl.ANY)

N_DEV = 8
N_CHIP = 4
V7X_LANES = 128
V7X_SUBLANES = 8
V7X_VMEM_LIMIT_BYTES = 56 * 1024 * 1024
FF_PAD_MULTIPLE = 1024
ADAMW_BLOCK_BYTES = 1 << 20

NORM_EPS = 1e-6
POOL_WINDOWS = (2, 4, 8, 16)
ADAM_LR, ADAM_B1, ADAM_B2, ADAM_EPS, ADAM_WD, ADAM_STEP = 0.001, 0.9, 0.999, 1e-08, 0.01, 10
GELU_C = math.sqrt(2.0 / math.pi)
GELU_A = 0.044715
HI = lax.Precision.HIGHEST


def _tile(dim, pref, mult):
    t = min(pref, dim)
    t -= t % mult
    while t >= mult:
        if dim % t == 0:
            return t
        t -= mult
    return dim


def _pcall(body, *, name, grid, in_specs, out_specs, out_shape, operands, scratch_shapes=(), semantics=None, comm=None):
    if comm is None:
        params = pltpu.CompilerParams(dimension_semantics=semantics, vmem_limit_bytes=V7X_VMEM_LIMIT_BYTES)
        outs = pl.pallas_call(body, name=name, grid=grid, in_specs=list(in_specs), out_specs=list(out_specs),
                              out_shape=list(out_shape), scratch_shapes=list(scratch_shapes),
                              compiler_params=params)(*operands)
        return list(outs), []
    sizes = [len(in_specs), len(comm.inputs), len(out_shape), len(comm.out_shape), len(scratch_shapes),
             len(comm.sem_shapes)]

    def wrapped(*refs):
        groups, pos = [], 0
        for n in sizes:
            groups.append(refs[pos:pos + n])
            pos += n
        ins, c_ins, outs, c_outs, scratch, sems = groups
        if not grid:
            comm.start(c_ins, c_outs, sems)
            if body is not None:
                body(*ins, *outs, *scratch)
            comm.finish(c_ins, c_outs, sems)
            return
        ids = [pl.program_id(a) for a in range(len(grid))]
        first = functools.reduce(jnp.logical_and, [i == 0 for i in ids])
        last = functools.reduce(jnp.logical_and, [i == g - 1 for i, g in zip(ids, grid)])

        @pl.when(first)
        def _():
            comm.start(c_ins, c_outs, sems)

        body(*ins, *outs, *scratch)

        @pl.when(last)
        def _():
            comm.finish(c_ins, c_outs, sems)

    params = pltpu.CompilerParams(dimension_semantics=("arbitrary",) * len(grid), vmem_limit_bytes=V7X_VMEM_LIMIT_BYTES)
    res = pl.pallas_call(wrapped, name=name, grid=grid, in_specs=list(in_specs) + [ANY] * sizes[1],
                         out_specs=list(out_specs) + [ANY] * sizes[3], out_shape=list(out_shape) + list(comm.out_shape),
                         scratch_shapes=list(scratch_shapes) + list(comm.sem_shapes),
                         compiler_params=params)(*operands, *comm.inputs)
    return list(res[:sizes[2]]), list(res[sizes[2]:])


def _my_place():
    return lax.axis_index("x"), lax.axis_index("y"), lax.axis_index("c")


def _block_index(px, py, pc):
    return 4 * px + 2 * py + pc


class _Gather:
    def __init__(self, shards, pads):
        self.n, self.pads = len(shards), list(pads)
        zero_blocks = [jnp.zeros((p, s.shape[1]), s.dtype) for s, p in zip(shards, pads) if p]
        self.inputs = list(shards) + zero_blocks

        def full_shape(s, pad):
            if s.ndim == 2:
                return (N_DEV * s.shape[0] + pad, s.shape[1])
            return (s.shape[0], N_DEV * s.shape[1], s.shape[2])

        self.out_shape = [jax.ShapeDtypeStruct(full_shape(s, p), s.dtype) for s, p in zip(shards, pads)]
        self.sem_shapes = [pltpu.SemaphoreType.DMA((7 * self.n,)), pltpu.SemaphoreType.DMA((7 * self.n,)),
                           pltpu.SemaphoreType.DMA((self.n + len(zero_blocks),))]

    def _copies(self, ins, outs, sems):
        n = self.n
        send_sems, recv_sems, local_sems = sems
        x, y, c = _my_place()
        me, sibling = (x, y, c), (x, y, 1 - c)
        chips = [(1 - x, y), (x, 1 - y), (1 - x, 1 - y)]

        def rows(i, place):
            idx = _block_index(*place)
            r = ins[i].shape[-2]
            if ins[i].ndim == 2:
                return outs[i].at[pl.ds(idx * r, r), :]
            return outs[i].at[:, pl.ds(idx * r, r), :]

        def copy(i, k, block, to, src=None):
            return pltpu.make_async_remote_copy(
                src_ref=rows(i, block) if src is None else src, dst_ref=rows(i, block),
                send_sem=send_sems.at[7 * i + k], recv_sem=recv_sems.at[7 * i + k],
                device_id=to, device_id_type=MESH)

        local = [pltpu.make_async_copy(ins[i], rows(i, me), local_sems.at[i]) for i in range(n)]
        zi = 0
        for i in range(n):
            if self.pads[i]:
                start = N_DEV * ins[i].shape[0]
                local.append(pltpu.make_async_copy(ins[n + zi], outs[i].at[pl.ds(start, self.pads[i]), :],
                                                   local_sems.at[n + zi]))
                zi += 1
        first = []
        for i in range(n):
            first.append(copy(i, 0, me, sibling, src=ins[i]))
            first += [copy(i, 1 + j, me, (*chip, c), src=ins[i]) for j, chip in enumerate(chips)]
        return local, first, copy, chips, me, sibling, c

    def start(self, ins, outs, sems):
        local, first, *_ = self._copies(ins, outs, sems)
        for cp in local + first:
            cp.start()

    def finish(self, ins, outs, sems):
        local, first, copy, chips, me, sibling, c = self._copies(ins, outs, sems)
        passed = []
        for j, chip in enumerate(chips):
            for i in range(self.n):
                copy(i, 1 + j, (*chip, c), me).wait_recv()
                fwd = copy(i, 4 + j, (*chip, c), sibling)
                fwd.start()
                passed.append(fwd)
        for i in range(self.n):
            copy(i, 0, sibling, me).wait_recv()
        for j, chip in enumerate(chips):
            for i in range(self.n):
                copy(i, 4 + j, (*chip, 1 - c), me).wait_recv()
        for cp in first + passed:
            cp.wait_send()
        for cp in local:
            cp.wait()


class _SiblingSwap:
    def __init__(self, grads, rows):
        self.n, self.rows = len(grads), list(rows)
        self.inputs = list(grads)
        self.out_shape = []
        for g, r in zip(grads, rows):
            self.out_shape += [jax.ShapeDtypeStruct((N_CHIP, r, g.shape[1]), g.dtype)] * 2
        self.sem_shapes = [pltpu.SemaphoreType.DMA((N_CHIP * self.n,))] * 3

    def _copies(self, ins, outs, sems):
        send_sems, recv_sems, local_sems = sems
        x, y, c = _my_place()
        local, remote = [], []
        for i in range(self.n):
            r = self.rows[i]
            own, got = outs[2 * i], outs[2 * i + 1]
            for q in range(N_CHIP):
                k = N_CHIP * i + q
                local.append(pltpu.make_async_copy(ins[i].at[pl.ds((2 * q + c) * r, r), :], own.at[q], local_sems.at[k]))
                remote.append(pltpu.make_async_remote_copy(
                    src_ref=ins[i].at[pl.ds((2 * q + 1 - c) * r, r), :], dst_ref=got.at[q],
                    send_sem=send_sems.at[k], recv_sem=recv_sems.at[k], device_id=(x, y, 1 - c), device_id_type=MESH))
        return local, remote

    def start(self, ins, outs, sems):
        local, remote = self._copies(ins, outs, sems)
        for cp in local + remote:
            cp.start()

    def finish(self, ins, outs, sems):
        local, remote = self._copies(ins, outs, sems)
        for cp in remote:
            cp.wait_recv()
        for cp in remote:
            cp.wait_send()
        for cp in local:
            cp.wait()


class _ChipScatter:
    FLIPS = [(0, 1), (1, 0), (1, 1)]

    def __init__(self, parts):
        self.n = len(parts)
        self.inputs = list(parts)
        self.out_shape = [jax.ShapeDtypeStruct(p.shape, p.dtype) for p in parts]
        self.sem_shapes = [pltpu.SemaphoreType.DMA((3 * self.n,)), pltpu.SemaphoreType.DMA((3 * self.n,)),
                           pltpu.SemaphoreType.DMA((self.n,))]

    def _copies(self, ins, outs, sems):
        send_sems, recv_sems, local_sems = sems
        x, y, c = _my_place()
        my_chip = 2 * x + y
        local, sends, recvs = [], [], []
        for i in range(self.n):
            local.append(pltpu.make_async_copy(ins[i].at[my_chip], outs[i].at[my_chip], local_sems.at[i]))
            for k, (fx, fy) in enumerate(self.FLIPS):
                px, py = x ^ fx, y ^ fy
                q = 2 * px + py
                common = dict(send_sem=send_sems.at[3 * i + k], recv_sem=recv_sems.at[3 * i + k],
                              device_id=(px, py, c), device_id_type=MESH)
                sends.append(pltpu.make_async_remote_copy(src_ref=ins[i].at[q], dst_ref=outs[i].at[my_chip], **common))
                recvs.append(pltpu.make_async_remote_copy(src_ref=ins[i].at[q], dst_ref=outs[i].at[q], **common))
        return local, sends, recvs

    def start(self, ins, outs, sems):
        local, sends, _ = self._copies(ins, outs, sems)
        for cp in local + sends:
            cp.start()

    def finish(self, ins, outs, sems):
        local, sends, recvs = self._copies(ins, outs, sems)
        for cp in recvs:
            cp.wait_recv()
        for cp in sends:
            cp.wait_send()
        for cp in local:
            cp.wait()


_FLIPS = [(0, 0, 1), (0, 1, 0), (0, 1, 1), (1, 0, 0), (1, 0, 1), (1, 1, 0), (1, 1, 1)]


class _Broadcast:
    def __init__(self, arrays):
        self.n = len(arrays)
        self.inputs = list(arrays)
        self.out_shape = [jax.ShapeDtypeStruct((N_DEV, *a.shape), a.dtype) for a in arrays]
        self.sem_shapes = [pltpu.SemaphoreType.DMA((7 * self.n,)), pltpu.SemaphoreType.DMA((7 * self.n,)),
                           pltpu.SemaphoreType.DMA((self.n,))]

    def _copies(self, ins, outs, sems):
        send_sems, recv_sems, local_sems = sems
        x, y, c = _my_place()
        my_idx = _block_index(x, y, c)
        local, sends, recvs = [], [], []
        for i in range(self.n):
            local.append(pltpu.make_async_copy(ins[i], outs[i].at[my_idx], local_sems.at[i]))
            for k, (fx, fy, fc) in enumerate(_FLIPS):
                peer = (x ^ fx, y ^ fy, c ^ fc)
                common = dict(send_sem=send_sems.at[7 * i + k], recv_sem=recv_sems.at[7 * i + k],
                              device_id=peer, device_id_type=MESH)
                sends.append(pltpu.make_async_remote_copy(src_ref=ins[i], dst_ref=outs[i].at[my_idx], **common))
                recvs.append(pltpu.make_async_remote_copy(src_ref=ins[i], dst_ref=outs[i].at[_block_index(*peer)], **common))
        return local, sends, recvs

    def start(self, ins, outs, sems):
        local, sends, _ = self._copies(ins, outs, sems)
        for cp in local + sends:
            cp.start()

    def finish(self, ins, outs, sems):
        local, sends, recvs = self._copies(ins, outs, sems)
        for cp in recvs:
            cp.wait_recv()
        for cp in sends:
            cp.wait_send()
        for cp in local:
            cp.wait()


def _comm_only(comm, *, name):
    return _pcall(None, name=name, grid=(), in_specs=[], out_specs=[], out_shape=[], operands=[], comm=comm)[1]


def _matmul(pairs, out_dtypes, *, name, ta=False, tb=False, separate=False, epilogue=None,
            extras=(), bcast=(), tm=512, tn=512, tk=4096, precision=None, comm=None):
    a0, b0 = pairs[0]
    m_dim, k_dim = (a0.shape[1], a0.shape[0]) if ta else a0.shape
    n_dim = b0.shape[0] if tb else b0.shape[1]
    tm = _tile(m_dim, tm, V7X_LANES if ta else 16)
    tn = _tile(n_dim, tn, V7X_LANES)
    tk = _tile(k_dim, tk, V7X_LANES)
    nk = k_dim // tk
    n_acc = len(pairs) if separate else 1
    n_ex, n_bc, n_out = len(extras), len(bcast), len(out_dtypes)
    dims = (((0 if ta else 1,), (1 if tb else 0,)), ((), ()))
    if epilogue is None:
        epilogue = lambda accs, ex, bc: tuple(accs)
    a_spec = pl.BlockSpec((tk, tm), lambda m, n, k: (k, m)) if ta else pl.BlockSpec((tm, tk), lambda m, n, k: (m, k))
    b_spec = pl.BlockSpec((tn, tk), lambda m, n, k: (n, k)) if tb else pl.BlockSpec((tk, tn), lambda m, n, k: (k, n))
    operands, operand_specs, pair_slots = [], [], []
    for pair in pairs:
        slots = []
        for arr, spec in zip(pair, (a_spec, b_spec)):
            found = [i for i, o in enumerate(operands) if o is arr]
            if not found:
                operands.append(arr)
                operand_specs.append(spec)
                found = [len(operands) - 1]
            slots.append(found[0])
        pair_slots.append(slots)
    n_ops = len(operands)

    def body(*refs):
        a_refs = [refs[sa] for sa, _ in pair_slots]
        b_refs = [refs[sb] for _, sb in pair_slots]
        ex_refs = refs[n_ops:n_ops + n_ex]
        bc_refs = refs[n_ops + n_ex:n_ops + n_ex + n_bc]
        out_refs = refs[n_ops + n_ex + n_bc:n_ops + n_ex + n_bc + n_out]
        acc_refs = refs[n_ops + n_ex + n_bc + n_out:]
        parts = [lax.dot_general(a[...], b[...], dims, preferred_element_type=F32, precision=precision)
                 for a, b in zip(a_refs, b_refs)]
        if not separate:
            parts = [functools.reduce(lambda p, q: p + q, parts)]

        def finish(accs):
            outs = epilogue(accs, [e[...] for e in ex_refs], [c[...] for c in bc_refs])
            for o_ref, o in zip(out_refs, outs):
                o_ref[...] = o.astype(o_ref.dtype)

        if nk == 1:
            finish(parts)
        else:
            k = pl.program_id(2)

            @pl.when(k == 0)
            def _():
                for acc, p in zip(acc_refs, parts):
                    acc[...] = p

            @pl.when(k > 0)
            def _():
                for acc, p in zip(acc_refs, parts):
                    acc[...] += p

            @pl.when(k == nk - 1)
            def _():
                finish([acc[...] for acc in acc_refs])

    mn_spec = pl.BlockSpec((tm, tn), lambda m, n, k: (m, n))
    bc_spec = pl.BlockSpec((1, tn), lambda m, n, k: (0, n))
    outs, c_outs = _pcall(
        body, name=name, grid=(m_dim // tm, n_dim // tn, nk),
        in_specs=operand_specs + [mn_spec] * n_ex + [bc_spec] * n_bc,
        out_specs=[mn_spec] * n_out,
        out_shape=[jax.ShapeDtypeStruct((m_dim, n_dim), d) for d in out_dtypes],
        scratch_shapes=[pltpu.VMEM((tm, tn), F32)] * (n_acc if nk > 1 else 0),
        semantics=("parallel", "parallel", "arbitrary"),
        operands=[*operands, *extras, *bcast], comm=comm)
    outs = outs[0] if n_out == 1 else outs
    return outs if comm is None else (outs, c_outs)


def _with_comm(result, comm):
    return (result, []) if comm is None else result


def _bd(pairs, out_dtype, *, name, offs=None, tm=512, precision=None, comm=None):
    n_blocks, ka, kb = pairs[0][1].shape
    t_dim = pairs[0][0].shape[0]
    tm = _tile(t_dim, tm, 16)
    offs = offs or [0] * len(pairs)
    n_pairs = len(pairs)

    def body(*refs):
        acc = None
        for i in range(n_pairs):
            p = jnp.dot(refs[2 * i][...], refs[2 * i + 1][...], preferred_element_type=F32, precision=precision)
            acc = p if acc is None else acc + p
        refs[2 * n_pairs][...] = acc.astype(out_dtype)

    in_specs = []
    for off in offs:
        in_specs.append(pl.BlockSpec((tm, ka), lambda j, t, off=off: (t, j + off)))
        in_specs.append(pl.BlockSpec((None, ka, kb), lambda j, t: (j, 0, 0)))
    outs, c_outs = _pcall(
        body, name=name, grid=(n_blocks, t_dim // tm), in_specs=in_specs,
        out_specs=[pl.BlockSpec((tm, kb), lambda j, t: (t, j))],
        out_shape=[jax.ShapeDtypeStruct((t_dim, n_blocks * kb), out_dtype)],
        semantics=("parallel", "parallel"), operands=[t for p in pairs for t in p], comm=comm)
    return outs[0] if comm is None else (outs[0], c_outs)


def _bd_wgrad(a, b, ka, kb, n_blocks, *, name, off_a=0, off_b=0, tm=512, precision=None, sign=1.0):
    t_dim = a.shape[0]
    tm = _tile(t_dim, tm, 16)

    def body(a_ref, b_ref, o_ref):
        p = lax.dot_general(a_ref[...], b_ref[...], (((0,), (0,)), ((), ())), preferred_element_type=F32,
                            precision=precision)
        if sign != 1.0:
            p = p * sign

        @pl.when(pl.program_id(1) == 0)
        def _():
            o_ref[...] = p

        @pl.when(pl.program_id(1) > 0)
        def _():
            o_ref[...] += p

    return _pcall(
        body, name=name, grid=(n_blocks, t_dim // tm),
        in_specs=[pl.BlockSpec((tm, ka), lambda j, t: (t, j + off_a)),
                  pl.BlockSpec((tm, kb), lambda j, t: (t, j + off_b))],
        out_specs=[pl.BlockSpec((None, ka, kb), lambda j, t: (j, 0, 0))],
        out_shape=[jax.ShapeDtypeStruct((n_blocks, ka, kb), F32)],
        semantics=("parallel", "arbitrary"), operands=[a, b])[0][0]


def _rowk(fn, rows, bcast, outs, *, name, tm=256):
    rows = [r if isinstance(r, tuple) else (r, r.shape[1], 0) for r in rows]
    t_dim = rows[0][0].shape[0]
    tm = _tile(t_dim, tm, 16)
    n_rows, n_bc = len(rows), len(bcast)

    def body(*refs):
        ins = [r[...] for r in refs[:n_rows + n_bc]]
        vals = fn(*ins)
        first = pl.program_id(0) == 0
        for o_ref, v, spec in zip(refs[n_rows + n_bc:], vals, outs):
            if spec[0] == "row":
                o_ref[...] = v.astype(o_ref.dtype)
            else:
                @pl.when(first)
                def _(o_ref=o_ref, v=v):
                    o_ref[...] = v

                @pl.when(jnp.logical_not(first))
                def _(o_ref=o_ref, v=v):
                    o_ref[...] += v

    in_specs = [pl.BlockSpec((tm, w), lambda i, cb=cb: (i, cb)) for _, w, cb in rows]
    in_specs += [pl.BlockSpec((1, b.shape[1]), lambda i: (0, 0)) for b in bcast]
    out_specs, out_shape = [], []
    for spec in outs:
        if spec[0] == "row":
            out_specs.append(pl.BlockSpec((tm, spec[1]), lambda i: (i, 0)))
            out_shape.append(jax.ShapeDtypeStruct((t_dim, spec[1]), spec[2]))
        else:
            out_specs.append(pl.BlockSpec((1, spec[1]), lambda i: (0, 0)))
            out_shape.append(jax.ShapeDtypeStruct((1, spec[1]), F32))
    return _pcall(body, name=name, grid=(t_dim // tm,), in_specs=in_specs, out_specs=out_specs, out_shape=out_shape,
                  semantics=("arbitrary",), operands=[*[r[0] for r in rows], *bcast])[0]


def _colsum(v):
    return jnp.sum(v, axis=0, keepdims=True)


def _rms_fwd(x, g, *, name):
    def fn(x, g):
        r = lax.rsqrt(jnp.mean(x * x, axis=-1, keepdims=True) + NORM_EPS)
        return (x * r * g).astype(BF16), r
    d = x.shape[1]
    return _rowk(fn, [x], [g], [("row", d, BF16), ("row", 1, F32)], name=name)


def _rms_bwd_rows(dn, x, r, g):
    xh = x * r
    dy = dn * g
    dx = r * (dy - xh * jnp.mean(dy * xh, axis=-1, keepdims=True))
    return dx, _colsum(dn * xh)


def _silu_parts(g):
    sg = jax.nn.sigmoid(g)
    return g * sg, sg * (1.0 + g * (1.0 - sg))


def _ffn_fwd(h, norm_g, w_gate_t, w_up_t, get_w_down, tag, comm_gateup, comm_down):
    n, r = _rms_fwd(h, norm_g, name=f"{tag}_rms")

    def gate_up(accs, ex, bc):
        g, u = accs
        return g, u, _silu_parts(g)[0] * u

    (g, u, a), got1 = _matmul([(n, w_gate_t), (n, w_up_t)], [BF16, BF16, BF16], name=f"{tag}_gateup", tb=True,
                              separate=True, epilogue=gate_up, comm=comm_gateup)
    w_down = get_w_down(got1)
    out, got2 = _with_comm(_matmul([(a, w_down)], [F32], name=f"{tag}_down", extras=[h], tm=1024, tn=1024, tk=1024,
                                   epilogue=lambda accs, ex, bc: (ex[0] + 0.5 * accs[0],), comm=comm_down), comm_down)
    return out, (h, n, r, g, u, a), got1, got2


def _ffn_bwd_act(dyb, saved, w_down, tag, comm):
    g, u = saved[3], saved[4]

    def act_bwd(accs, ex, bc):
        da, g, u = accs[0], ex[0].astype(F32), ex[1].astype(F32)
        silu, dsilu = _silu_parts(g)
        return da * u * dsilu, da * silu

    return _matmul([(dyb, w_down)], [BF16, BF16], name=f"{tag}_dact", tb=True, extras=[g, u], epilogue=act_bwd, comm=comm)


def _ffn_bwd_input(dh, dg, du, saved, norm_g, w_gate_t, w_up_t, tag, comm):
    h, n, r = saved[0], saved[1], saved[2]
    dn, got = _with_comm(_matmul([(dg, w_gate_t), (du, w_up_t)], [F32], name=f"{tag}_dn", tm=1024, tn=1024, tk=1024,
                                 comm=comm), comm)

    def fn(dh, dn, h, r, gain):
        dx, dgain = _rms_bwd_rows(dn, h, r, gain)
        dx = dh + dx
        return dx, (0.5 * dx).astype(BF16), dx.astype(BF16), dgain

    d = h.shape[1]
    dx, dx_half_b, dx_b, d_norm = _rowk(fn, [dh, dn, h, r], [norm_g],
                                        [("row", d, F32), ("row", d, BF16), ("row", d, BF16), ("sum", d)],
                                        name=f"{tag}_rms_bwd")
    return dx, dx_half_b, dx_b, d_norm, got


def _pool_window(z, seq, width, group_width, *, name, transpose):
    assert POOL_WINDOWS == (2, 4, 8, 16)
    t_dim = z.shape[0]
    tc = _tile(group_width, 256, V7X_LANES)
    per_group = group_width // tc

    def body(z_ref, o_ref):
        gid = pl.program_id(1) // per_group
        v = z_ref[...]
        t = lax.broadcasted_iota(jnp.int32, v.shape, 0)
        win = jnp.where(gid == 0, 2, jnp.where(gid == 1, 4, jnp.where(gid == 2, 8, 16)))
        cnt = jnp.minimum(t + 1, win).astype(F32)
        if transpose:
            e = v / cnt
            shift = lambda q, k: jnp.where(t < seq - k, pltpu.roll(q, seq - k, 0), 0.0)
        else:
            e = v
            shift = lambda q, k: jnp.where(t >= k, pltpu.roll(q, k, 0), 0.0)
        s1 = e + shift(e, 1)
        s2 = s1 + shift(s1, 2)
        s3 = s2 + shift(s2, 4)
        s4 = s3 + shift(s3, 8)
        s = jnp.where(gid == 0, s1, jnp.where(gid == 1, s2, jnp.where(gid == 2, s3, s4)))
        if transpose:
            o_ref[...] = (s - v).astype(o_ref.dtype)
        else:
            o_ref[...] = (s / cnt - v).astype(o_ref.dtype)

    spec = pl.BlockSpec((seq, tc), lambda b, c: (b, c))
    return _pcall(body, name=name, grid=(t_dim // seq, width // tc), in_specs=[spec], out_specs=[spec],
                  out_shape=[jax.ShapeDtypeStruct((t_dim, width), F32 if transpose else BF16)],
                  semantics=("parallel", "parallel"), operands=[z])[0][0]


def _discretize(lam_re, lam_im, log_dt, b_re_t, b_im_t):
    dt = jnp.exp(log_dt)
    mag = jnp.exp(lam_re * dt)
    a_re = mag * jnp.cos(lam_im * dt)
    a_im = mag * jnp.sin(lam_im * dt)
    den = lam_re * lam_re + lam_im * lam_im
    f_re = ((a_re - 1.0) * lam_re + a_im * lam_im) / den
    f_im = (a_im * lam_re - (a_re - 1.0) * lam_im) / den
    f_re, f_im = f_re[:, None, :], f_im[:, None, :]
    return a_re, a_im, f_re * b_re_t - f_im * b_im_t, f_re * b_im_t + f_im * b_re_t


def _discretize_fwd(params):
    shapes = [jax.ShapeDtypeStruct(params[0].shape, F32)] * 2 + [jax.ShapeDtypeStruct(params[3].shape, F32)] * 2

    def body(*refs):
        outs = _discretize(*[r[...] for r in refs[:5]])
        for o_ref, o in zip(refs[5:], outs):
            o_ref[...] = o

    return pl.pallas_call(body, name="s5_discretize", out_shape=shapes,
                          compiler_params=pltpu.CompilerParams(vmem_limit_bytes=V7X_VMEM_LIMIT_BYTES))(*params)


def _discretize_bwd(params, cots):
    shapes = [jax.ShapeDtypeStruct(p.shape, F32) for p in params]

    def body(*refs):
        _, vjp = jax.vjp(_discretize, *[r[...] for r in refs[:5]])
        grads = vjp(tuple(r[...] for r in refs[5:9]))
        for o_ref, o in zip(refs[9:], grads):
            o_ref[...] = o

    return pl.pallas_call(body, name="s5_discretize_bwd", out_shape=shapes,
                          compiler_params=pltpu.CompilerParams(vmem_limit_bytes=V7X_VMEM_LIMIT_BYTES))(*params, *cots)


def _scan_tiles(t_dim, n_dim, seq):
    return _tile(seq, 256, V7X_SUBLANES), _tile(n_dim, 1024, V7X_LANES)


def _scan_fwd(bu_re, bu_im, a_re, a_im, seq, comm=None):
    t_dim, n_dim = bu_re.shape
    tt, tl = _scan_tiles(t_dim, n_dim, seq)
    per_seq = seq // tt

    def body(br_ref, bi_ref, ar_ref, ai_ref, xr_ref, xi_ref, cr_ref, ci_ref):
        @pl.when(pl.program_id(1) % per_seq == 0)
        def _():
            cr_ref[...] = jnp.zeros_like(cr_ref)
            ci_ref[...] = jnp.zeros_like(ci_ref)

        ar, ai = ar_ref[...], ai_ref[...]

        def step(i, carry):
            xr, xi = carry
            row = pl.ds(i, 1)
            nr = ar * xr - ai * xi + br_ref[row, :]
            ni = ai * xr + ar * xi + bi_ref[row, :]
            xr_ref[row, :] = nr
            xi_ref[row, :] = ni
            return nr, ni

        xr, xi = lax.fori_loop(0, tt, step, (cr_ref[...], ci_ref[...]), unroll=8)
        cr_ref[...] = xr
        ci_ref[...] = xi

    blk = pl.BlockSpec((tt, tl), lambda l, t: (t, l))
    vec = pl.BlockSpec((1, tl), lambda l, t: (0, l))
    outs, c_outs = _pcall(
        body, name="s5_scan", grid=(n_dim // tl, t_dim // tt), in_specs=[blk, blk, vec, vec], out_specs=[blk, blk],
        out_shape=[jax.ShapeDtypeStruct((t_dim, n_dim), F32)] * 2, scratch_shapes=[pltpu.VMEM((1, tl), F32)] * 2,
        semantics=("parallel", "arbitrary"), operands=[bu_re, bu_im, a_re, a_im], comm=comm)
    return outs, c_outs


def _scan_bwd(gx_re, gx_im, x_re, x_im, a_re, a_im, seq, comm=None):
    t_dim, n_dim = gx_re.shape
    tt, tl = _scan_tiles(t_dim, n_dim, seq)
    per_seq = seq // tt
    n_t = t_dim // tt
    prev_rows = V7X_SUBLANES

    def body(gr_ref, gi_ref, xr_ref, xi_ref, pr_ref, pi_ref, ar_ref, ai_ref,
             lr_ref, li_ref, dar_ref, dai_ref, cr_ref, ci_ref):
        step_id = pl.program_id(1)
        blk_id = n_t - 1 - step_id

        @pl.when((blk_id + 1) % per_seq == 0)
        def _():
            cr_ref[...] = jnp.zeros_like(cr_ref)
            ci_ref[...] = jnp.zeros_like(ci_ref)

        ar, ai = ar_ref[...], ai_ref[...]

        def step(j, carry):
            lr, li = carry
            row = pl.ds(tt - 1 - j, 1)
            nr = gr_ref[row, :] + ar * lr + ai * li
            ni = gi_ref[row, :] - ai * lr + ar * li
            lr_ref[row, :] = nr
            li_ref[row, :] = ni
            return nr, ni

        lr, li = lax.fori_loop(0, tt, step, (cr_ref[...], ci_ref[...]), unroll=8)
        cr_ref[...] = lr
        ci_ref[...] = li

        first_of_seq = blk_id % per_seq == 0
        keep = jnp.where(first_of_seq, 0.0, 1.0)
        t = lax.broadcasted_iota(jnp.int32, (tt, tl), 0)
        xr_prev = jnp.where(t == 0, pr_ref[prev_rows - 1:prev_rows, :] * keep, pltpu.roll(xr_ref[...], 1, 0))
        xi_prev = jnp.where(t == 0, pi_ref[prev_rows - 1:prev_rows, :] * keep, pltpu.roll(xi_ref[...], 1, 0))
        lam_r, lam_i = lr_ref[...], li_ref[...]
        d_re = _colsum(lam_r * xr_prev + lam_i * xi_prev)
        d_im = _colsum(lam_i * xr_prev - lam_r * xi_prev)

        @pl.when(step_id == 0)
        def _():
            dar_ref[...] = d_re
            dai_ref[...] = d_im

        @pl.when(step_id > 0)
        def _():
            dar_ref[...] += d_re
            dai_ref[...] += d_im

    blk = pl.BlockSpec((tt, tl), lambda l, t: (n_t - 1 - t, l))
    prev = pl.BlockSpec((prev_rows, tl), lambda l, t: (jnp.maximum((n_t - 1 - t) * (tt // prev_rows) - 1, 0), l))
    vec = pl.BlockSpec((1, tl), lambda l, t: (0, l))
    return _pcall(
        body, name="s5_scan_bwd", grid=(n_dim // tl, n_t),
        in_specs=[blk, blk, blk, blk, prev, prev, vec, vec], out_specs=[blk, blk, vec, vec],
        out_shape=[jax.ShapeDtypeStruct((t_dim, n_dim), F32)] * 2 + [jax.ShapeDtypeStruct((1, n_dim), F32)] * 2,
        scratch_shapes=[pltpu.VMEM((1, tl), F32)] * 2, semantics=("parallel", "arbitrary"),
        operands=[gx_re, gx_im, x_re, x_im, x_re, x_im, a_re, a_im], comm=comm)


def _block_diag(w, per_block):
    g, ka, kb = w.shape
    eye = jnp.eye(per_block, dtype=w.dtype)
    out = jnp.einsum("jakb,ac->jakcb", w.reshape(g // per_block, per_block, ka, kb), eye)
    return out.reshape(g // per_block, per_block * ka, per_block * kb)


def _block_diag_t(d, per_block, ka, kb):
    j = d.shape[0]
    eye = jnp.eye(per_block, dtype=d.dtype)
    picked = jnp.einsum("jakcb,ac->jakb", d.reshape(j, per_block, ka, per_block, kb), eye)
    return picked.reshape(j * per_block, ka, kb)


def _gelu_parts(y):
    inner = GELU_C * (y + GELU_A * y * y * y)
    th = jnp.tanh(inner)
    val = 0.5 * y * (1.0 + th)
    grad = 0.5 * (1.0 + th) + 0.5 * y * (1.0 - th * th) * GELU_C * (1.0 + 3.0 * GELU_A * y * y)
    return val, grad


def _add_pairs(a, b, *, name):
    n, r, c = a.shape
    tr = _tile(r, 704, 16)
    tc = _tile(c, 2048, V7X_LANES)

    def body(a_ref, b_ref, o_ref):
        o_ref[...] = (a_ref[...].astype(F32) + b_ref[...].astype(F32)).astype(o_ref.dtype)

    spec = pl.BlockSpec((None, tr, tc), lambda q, i, j: (q, i, j))
    return _pcall(body, name=name, grid=(n, r // tr, c // tc), in_specs=[spec, spec], out_specs=[spec],
                  out_shape=[jax.ShapeDtypeStruct(a.shape, a.dtype)], semantics=("parallel",) * 3, operands=[a, b])[0][0]


def _sum_parts(parts, *, name):
    n_parts, r, c = parts.shape
    tr = _tile(r, 704, 16)
    tc = _tile(c, 1024, V7X_LANES)

    def body(p_ref, o_ref):
        acc = p_ref[0].astype(F32)
        for s in range(1, n_parts):
            acc = acc + p_ref[s].astype(F32)
        o_ref[...] = acc

    return _pcall(body, name=name, grid=(r // tr, c // tc),
                  in_specs=[pl.BlockSpec((n_parts, tr, tc), lambda i, j: (0, i, j))],
                  out_specs=[pl.BlockSpec((tr, tc), lambda i, j: (i, j))],
                  out_shape=[jax.ShapeDtypeStruct((r, c), F32)], semantics=("parallel", "parallel"),
                  operands=[parts])[0][0]


def _adamw(w, g, m, v, *, name):
    r, c = w.shape
    tr = _tile(r, max(V7X_SUBLANES, ADAMW_BLOCK_BYTES // (4 * c)), V7X_SUBLANES)
    c1 = 1.0 / (1.0 - ADAM_B1 ** ADAM_STEP)
    c2 = 1.0 / (1.0 - ADAM_B2 ** ADAM_STEP)

    def body(w_ref, g_ref, m_ref, v_ref, d_ref, nm_ref, nv_ref):
        g = g_ref[...]
        nm = ADAM_B1 * m_ref[...] + (1.0 - ADAM_B1) * g
        nv = ADAM_B2 * v_ref[...] + (1.0 - ADAM_B2) * (g * g)
        m_hat = nm * c1
        v_hat = nv * c2
        d_ref[...] = -ADAM_LR * (m_hat / (jnp.sqrt(v_hat) + ADAM_EPS) + ADAM_WD * w_ref[...])
        nm_ref[...] = nm
        nv_ref[...] = nv

    spec = pl.BlockSpec((tr, c), lambda i: (i, 0))
    return _pcall(body, name=name, grid=(r // tr,), in_specs=[spec] * 4, out_specs=[spec] * 3,
                  out_shape=[jax.ShapeDtypeStruct((r, c), F32)] * 3, semantics=("parallel",), operands=[w, g, m, v])[0]


def _pack(arrays, width=V7X_LANES):
    tile = V7X_SUBLANES * width
    parts, layout, row = [], [], 0
    for a in arrays:
        n = a.size
        rows = -(-n // tile) * V7X_SUBLANES
        flat = jnp.pad(a.reshape(-1).astype(F32), (0, rows * width - n))
        parts.append(flat.reshape(rows, width))
        layout.append((row, rows, n, a.shape))
        row += rows
    return jnp.concatenate(parts, axis=0), layout


def _unpack(packed, layout):
    return [packed[row:row + rows].reshape(-1)[:n].reshape(shape) for row, rows, n, shape in layout]


def kernel(x, ffn1_norm, ffn1_gate, ffn1_up, ffn1_down, mix_norm, w_in, w_pool, pool_scale, lam_re, lam_im, log_dt, b_re, b_im, c_re, c_im, d_skip, w_glu, b_glu, pool_out_norm, ssm_out_norm, w_out, ffn2_norm, ffn2_gate, ffn2_up, ffn2_down, final_norm, loss_target, m_ffn1_norm, m_ffn1_gate, m_ffn1_up, m_ffn1_down, m_mix_norm, m_w_in, m_w_pool, m_pool_scale, m_lam_re, m_lam_im, m_log_dt, m_b_re, m_b_im, m_c_re, m_c_im, m_d_skip, m_w_glu, m_b_glu, m_pool_out_norm, m_ssm_out_norm, m_w_out, m_ffn2_norm, m_ffn2_gate, m_ffn2_up, m_ffn2_down, m_final_norm, v_ffn1_norm, v_ffn1_gate, v_ffn1_up, v_ffn1_down, v_mix_norm, v_w_in, v_w_pool, v_pool_scale, v_lam_re, v_lam_im, v_log_dt, v_b_re, v_b_im, v_c_re, v_c_im, v_d_skip, v_w_glu, v_b_glu, v_pool_out_norm, v_ssm_out_norm, v_w_out, v_ffn2_norm, v_ffn2_gate, v_ffn2_up, v_ffn2_down, v_final_norm):
    weights = dict(ffn1_norm=ffn1_norm, ffn1_gate=ffn1_gate, ffn1_up=ffn1_up, ffn1_down=ffn1_down, mix_norm=mix_norm, w_in=w_in, w_pool=w_pool, pool_scale=pool_scale, lam_re=lam_re, lam_im=lam_im, log_dt=log_dt, b_re=b_re, b_im=b_im, c_re=c_re, c_im=c_im, d_skip=d_skip, w_glu=w_glu, b_glu=b_glu, pool_out_norm=pool_out_norm, ssm_out_norm=ssm_out_norm, w_out=w_out, ffn2_norm=ffn2_norm, ffn2_gate=ffn2_gate, ffn2_up=ffn2_up, ffn2_down=ffn2_down, final_norm=final_norm)
    moments_m = dict(ffn1_norm=m_ffn1_norm, ffn1_gate=m_ffn1_gate, ffn1_up=m_ffn1_up, ffn1_down=m_ffn1_down, mix_norm=m_mix_norm, w_in=m_w_in, w_pool=m_w_pool, pool_scale=m_pool_scale, lam_re=m_lam_re, lam_im=m_lam_im, log_dt=m_log_dt, b_re=m_b_re, b_im=m_b_im, c_re=m_c_re, c_im=m_c_im, d_skip=m_d_skip, w_glu=m_w_glu, b_glu=m_b_glu, pool_out_norm=m_pool_out_norm, ssm_out_norm=m_ssm_out_norm, w_out=m_w_out, ffn2_norm=m_ffn2_norm, ffn2_gate=m_ffn2_gate, ffn2_up=m_ffn2_up, ffn2_down=m_ffn2_down, final_norm=m_final_norm)
    moments_v = dict(ffn1_norm=v_ffn1_norm, ffn1_gate=v_ffn1_gate, ffn1_up=v_ffn1_up, ffn1_down=v_ffn1_down, mix_norm=v_mix_norm, w_in=v_w_in, w_pool=v_w_pool, pool_scale=v_pool_scale, lam_re=v_lam_re, lam_im=v_lam_im, log_dt=v_log_dt, b_re=v_b_re, b_im=v_b_im, c_re=v_c_re, c_im=v_c_im, d_skip=v_d_skip, w_glu=v_w_glu, b_glu=v_b_glu, pool_out_norm=v_pool_out_norm, ssm_out_norm=v_ssm_out_norm, w_out=v_w_out, ffn2_norm=v_ffn2_norm, ffn2_gate=v_ffn2_gate, ffn2_up=v_ffn2_up, ffn2_down=v_ffn2_down, final_norm=v_final_norm)
    names = list(weights)

    n_seq, seq, d_model = x.shape
    t_dim = n_seq * seq
    ff_shard = ffn1_gate.shape[1]
    ff = N_DEV * ff_shard
    ff_pad = -(-ff // FF_PAD_MULTIPLE) * FF_PAD_MULTIPLE - ff
    n_pool, pool_gw = w_pool.shape[0], w_pool.shape[2]
    pool_w = n_pool * pool_gw
    pool_rows = pool_gw // N_DEV
    n_grp, n_state, n_ch = b_re.shape
    ssm_w = n_grp * n_ch
    grp_per_blk = V7X_LANES // n_ch
    n_blk = n_grp // grp_per_blk
    ch_blk, st_blk = grp_per_blk * n_ch, grp_per_blk * n_state
    ssm_off = pool_w // ch_blk

    x2 = x.reshape(t_dim, d_model)
    tgt2 = loss_target.reshape(t_dim, d_model)
    row = lambda p: p.reshape(1, -1)

    sharded = ["ffn1_gate", "ffn1_up", "ffn1_down", "ffn2_gate", "ffn2_up", "ffn2_down", "w_in", "w_out", "w_glu", "w_pool"]
    transposed = {"ffn1_gate", "ffn1_up", "ffn2_gate", "ffn2_up"}
    shard = {n: (weights[n].T if n in transposed else weights[n]).astype(BF16) for n in sharded}

    def gather(*group):
        return _Gather([shard[n] for n in group], [ff_pad if n.startswith("ffn") else 0 for n in group])

    full = {}
    full["ffn1_gate"], full["ffn1_up"] = _comm_only(gather("ffn1_gate", "ffn1_up"), name="gather_first")

    def ffn1_down_weights(got):
        full["ffn1_down"], full["w_in"] = got
        return full["ffn1_down"]

    h1, saved1, _, got = _ffn_fwd(x2, row(ffn1_norm), full["ffn1_gate"], full["ffn1_up"], ffn1_down_weights, "ffn1",
                                  gather("ffn1_down", "w_in"), gather("ffn2_gate"))
    (full["ffn2_gate"],) = got
    n2, r2 = _rms_fwd(h1, row(mix_norm), name="mix_rms")
    z, (full["w_glu"], full["w_pool"]) = _matmul([(n2, full["w_in"])], [F32], name="mix_in", comm=gather("w_glu", "w_pool"))

    d_pool = _pool_window(z, seq, pool_w, pool_gw, name="pool_window", transpose=False)
    y_pool_lin = _bd([(d_pool, full["w_pool"])], F32, name="pool_mix")

    disc_params = (lam_re, lam_im, log_dt.reshape(n_grp, 1), jnp.swapaxes(b_re, 1, 2), jnp.swapaxes(b_im, 1, 2))
    a_re, a_im, bb_re_t, bb_im_t = _discretize_fwd(disc_params)
    a_re_row, a_im_row = row(a_re), row(a_im)
    wb_re, wb_im = _block_diag(bb_re_t, grp_per_blk), _block_diag(bb_im_t, grp_per_blk)
    wc_re = _block_diag(jnp.swapaxes(c_re, 1, 2), grp_per_blk)
    wc_im = _block_diag(jnp.swapaxes(c_im, 1, 2), grp_per_blk)
    bu_re = _bd([(z, wb_re)], F32, name="s5_bu_re", offs=[ssm_off], precision=HI)
    bu_im = _bd([(z, wb_im)], F32, name="s5_bu_im", offs=[ssm_off], precision=HI)
    (xs_re, xs_im), (full["w_out"],) = _scan_fwd(bu_re, bu_im, a_re_row, a_im_row, seq, comm=gather("w_out"))
    y_lin, (full["ffn2_up"],) = _bd([(xs_re, wc_re), (xs_im, -wc_im)], F32, name="s5_cx", precision=HI,
                                    comm=gather("ffn2_up"))

    def s5_post(y_lin, u, skip):
        y = y_lin + skip * u
        return y, _gelu_parts(y)[0].astype(BF16)

    u_cols = (z, ssm_w, pool_w // ssm_w)
    y_ssm, yg = _rowk(s5_post, [y_lin, u_cols], [row(d_skip)], [("row", ssm_w, F32), ("row", ssm_w, BF16)], name="s5_post")

    def glu(accs, ex, bc):
        q = accs[0] + bc[0]
        return q, ex[0].astype(F32) * jax.nn.sigmoid(q)

    q_glu, y_s5 = _matmul([(yg, full["w_glu"])], [F32, F32], name="s5_glu", extras=[yg], bcast=[row(b_glu)], epilogue=glu)

    def merge(yp_lin, ys, scale, gp, gs):
        yp = yp_lin * scale
        rp = lax.rsqrt(jnp.mean(yp * yp, axis=-1, keepdims=True) + NORM_EPS)
        rs = lax.rsqrt(jnp.mean(ys * ys, axis=-1, keepdims=True) + NORM_EPS)
        merged = jnp.concatenate([yp * rp * gp, ys * rs * gs], axis=-1)
        return merged.astype(BF16), rp, rs

    merged, r_pool, r_ssm = _rowk(merge, [y_pool_lin, y_s5], [row(pool_scale), row(pool_out_norm), row(ssm_out_norm)],
                                  [("row", pool_w + ssm_w, BF16), ("row", 1, F32), ("row", 1, F32)], name="mix_merge")
    h2 = _matmul([(merged, full["w_out"])], [F32], name="mix_out", extras=[h1],
                 epilogue=lambda accs, ex, bc: (ex[0] + accs[0],))

    def ffn2_down_weights(got):
        (full["ffn2_down"],) = got
        return full["ffn2_down"]

    h3, saved2, _, _ = _ffn_fwd(h2, row(ffn2_norm), full["ffn2_gate"], full["ffn2_up"], ffn2_down_weights, "ffn2",
                                gather("ffn2_down"), None)

    def head(h, tgt, gain):
        r = lax.rsqrt(jnp.mean(h * h, axis=-1, keepdims=True) + NORM_EPS)
        xh = h * r
        err = xh * gain - tgt
        loss = jnp.sum(0.5 * jnp.mean(err * err, axis=-1, keepdims=True), axis=0, keepdims=True)
        dout = err * (1.0 / d_model)
        dy = dout * gain
        dh = r * (dy - xh * jnp.mean(dy * xh, axis=-1, keepdims=True))
        return dh, (0.5 * dh).astype(BF16), _colsum(dout * xh), jnp.broadcast_to(loss, (1, V7X_LANES))

    dh3, dyb2, g_final, loss_part = _rowk(head, [h3, tgt2], [row(final_norm)],
                                          [("row", d_model, F32), ("row", d_model, BF16), ("sum", d_model), ("sum", V7X_LANES)],
                                          name="loss_head")

    grads = {}
    shard_rows = {n: (n_pool * pool_rows if n == "w_pool" else shard[n].shape[0]) for n in sharded}

    def to_sibling(group, name):
        got = _comm_only(_SiblingSwap([grads[n] for n in group], [shard_rows[n] for n in group]), name=name)
        return [_add_pairs(got[2 * i], got[2 * i + 1], name=f"chip_sum_{n}") for i, n in enumerate(group)]

    from_chips = {}

    def to_chips(group, parts):
        return _ChipScatter(parts), group

    def take(group, got):
        for n, g in zip(group, got):
            from_chips[n] = g

    dg2, du2 = _ffn_bwd_act(dyb2, saved2, full["ffn2_down"], "ffn2", None)
    dh2, _, dh2b, grads["ffn2_norm"], _ = _ffn_bwd_input(dh3, dg2, du2, saved2, row(ffn2_norm), full["ffn2_gate"],
                                                        full["ffn2_up"], "ffn2", None)
    grads["ffn2_gate"] = _matmul([(dg2, saved2[1])], [BF16], name="ffn2_dwgate", ta=True)
    grads["ffn2_up"] = _matmul([(du2, saved2[1])], [BF16], name="ffn2_dwup", ta=True)
    grads["ffn2_down"] = _matmul([(saved2[5], dyb2)], [BF16], name="ffn2_dwdown", ta=True)
    part_g2, part_u2, part_d2 = to_sibling(["ffn2_gate", "ffn2_up", "ffn2_down"], "grads_to_sibling_ffn2")

    d_merged = _matmul([(dh2b, full["w_out"])], [F32], name="mix_out_dx", tb=True)
    grads["w_out"] = _matmul([(merged, dh2b)], [BF16], name="mix_out_dw", ta=True)

    def merge_bwd(dm, yp_lin, ys, rp, rs, scale, gp, gs):
        yp = yp_lin * scale
        d_yp, d_gp = _rms_bwd_rows(dm[:, :pool_w], yp, rp, gp)
        d_ys, d_gs = _rms_bwd_rows(dm[:, pool_w:], ys, rs, gs)
        return (d_yp * scale).astype(BF16), d_ys, _colsum(d_yp * yp_lin), d_gp, d_gs

    d_pool_lin, d_ys, grads["pool_scale"], grads["pool_out_norm"], grads["ssm_out_norm"] = _rowk(
        merge_bwd, [d_merged, y_pool_lin, y_s5, r_pool, r_ssm], [row(pool_scale), row(pool_out_norm), row(ssm_out_norm)],
        [("row", pool_w, BF16), ("row", ssm_w, F32), ("sum", pool_w), ("sum", pool_w), ("sum", ssm_w)], name="mix_merge_bwd")

    w_pool_t = jnp.swapaxes(full["w_pool"], 1, 2)
    dd_pool = _bd([(d_pool_lin, w_pool_t)], F32, name="pool_mix_dx")
    g_pool = _bd_wgrad(d_pool, d_pool_lin, pool_gw, pool_gw, n_pool, name="pool_mix_dw")
    grads["w_pool"] = g_pool.reshape(n_pool, N_DEV, pool_rows, pool_gw).transpose(1, 0, 2, 3).reshape(
        N_DEV * n_pool * pool_rows, pool_gw).astype(BF16)
    dz_pool = _pool_window(dd_pool, seq, pool_w, pool_gw, name="pool_window_bwd", transpose=True)

    def glu_bwd(d_ys, yg, q):
        sg = jax.nn.sigmoid(q)
        dq = d_ys * yg.astype(F32) * sg * (1.0 - sg)
        return dq.astype(BF16), d_ys * sg, _colsum(dq)

    dq, d_yg_direct, grads["b_glu"] = _rowk(glu_bwd, [d_ys, yg, q_glu], [],
                                            [("row", ssm_w, BF16), ("row", ssm_w, F32), ("sum", ssm_w)], name="s5_glu_bwd")
    d_yg_mm = _matmul([(dq, full["w_glu"])], [F32], name="s5_glu_dx", tb=True)
    grads["w_glu"] = _matmul([(yg, dq)], [BF16], name="s5_glu_dw", ta=True)

    def gelu_bwd(d1, d2, y, u, skip):
        dy = (d1 + d2) * _gelu_parts(y)[1]
        return dy, dy * skip, _colsum(dy * u)

    dy_ssm, du_skip, grads["d_skip"] = _rowk(gelu_bwd, [d_yg_direct, d_yg_mm, y_ssm, u_cols], [row(d_skip)],
                                             [("row", ssm_w, F32), ("row", ssm_w, F32), ("sum", ssm_w)], name="s5_gelu_bwd")
    wc_re_t, wc_im_t = jnp.swapaxes(wc_re, 1, 2), jnp.swapaxes(wc_im, 1, 2)
    gx_re = _bd([(dy_ssm, wc_re_t)], F32, name="s5_gx_re", precision=HI)
    gx_im = _bd([(dy_ssm, -wc_im_t)], F32, name="s5_gx_im", precision=HI)
    d_wc_re = _bd_wgrad(xs_re, dy_ssm, st_blk, ch_blk, n_blk, name="s5_dc_re", precision=HI)
    d_wc_im = _bd_wgrad(xs_im, dy_ssm, st_blk, ch_blk, n_blk, name="s5_dc_im", precision=HI, sign=-1.0)
    plan, group = to_chips(["ffn2_gate"], [part_g2])
    (lm_re, lm_im, da_re, da_im), got = _scan_bwd(gx_re, gx_im, xs_re, xs_im, a_re_row, a_im_row, seq, comm=plan)
    take(group, got)
    d_wb_re = _bd_wgrad(z, lm_re, ch_blk, st_blk, n_blk, name="s5_db_re", off_a=ssm_off, precision=HI)
    d_wb_im = _bd_wgrad(z, lm_im, ch_blk, st_blk, n_blk, name="s5_db_im", off_a=ssm_off, precision=HI)
    wb_re_t, wb_im_t = jnp.swapaxes(wb_re, 1, 2), jnp.swapaxes(wb_im, 1, 2)
    du_lin = _bd([(lm_re, wb_re_t), (lm_im, wb_im_t)], F32, name="s5_du", precision=HI)
    grads["c_re"] = jnp.swapaxes(_block_diag_t(d_wc_re, grp_per_blk, n_state, n_ch), 1, 2)
    grads["c_im"] = jnp.swapaxes(_block_diag_t(d_wc_im, grp_per_blk, n_state, n_ch), 1, 2)
    d_bb_re_t = _block_diag_t(d_wb_re, grp_per_blk, n_ch, n_state)
    d_bb_im_t = _block_diag_t(d_wb_im, grp_per_blk, n_ch, n_state)
    g_lam_re, g_lam_im, g_log_dt, g_b_re_t, g_b_im_t = _discretize_bwd(
        disc_params, (da_re.reshape(n_grp, n_state), da_im.reshape(n_grp, n_state), d_bb_re_t, d_bb_im_t))
    grads["lam_re"], grads["lam_im"], grads["log_dt"] = g_lam_re, g_lam_im, g_log_dt.reshape(n_grp)
    grads["b_re"], grads["b_im"] = jnp.swapaxes(g_b_re_t, 1, 2), jnp.swapaxes(g_b_im_t, 1, 2)

    def join(dzp, du1, du2):
        return (jnp.concatenate([dzp, du1 + du2], axis=-1).astype(BF16),)

    (dz,) = _rowk(join, [dz_pool, du_lin, du_skip], [], [("row", pool_w + ssm_w, BF16)], name="mix_in_join")
    dn2 = _matmul([(dz, full["w_in"])], [F32], name="mix_in_dx", tb=True)
    grads["w_in"] = _matmul([(n2, dz)], [BF16], name="mix_in_dw", ta=True)
    mixer = ["w_out", "w_glu", "w_pool", "w_in"]
    parts_mixer = to_sibling(mixer, "grads_to_sibling_mixer")

    def mix_rms_bwd(dh, dn, h, r, gain):
        dx, dgain = _rms_bwd_rows(dn, h, r, gain)
        dx = dh + dx
        return dx, (0.5 * dx).astype(BF16), dgain

    dh1, dyb1, grads["mix_norm"] = _rowk(mix_rms_bwd, [dh2, dn2, h1, r2], [row(mix_norm)],
                                         [("row", d_model, F32), ("row", d_model, BF16), ("sum", d_model)], name="mix_rms_bwd")

    plan, group = to_chips(["ffn2_up"], [part_u2])
    (dg1, du1), got = _ffn_bwd_act(dyb1, saved1, full["ffn1_down"], "ffn1", plan)
    take(group, got)
    plan, group = to_chips(["ffn2_down"] + mixer, [part_d2] + parts_mixer)
    dx, _, _, grads["ffn1_norm"], got = _ffn_bwd_input(dh1, dg1, du1, saved1, row(ffn1_norm), full["ffn1_gate"],
                                                       full["ffn1_up"], "ffn1", plan)
    take(group, got)
    grads["ffn1_gate"] = _matmul([(dg1, saved1[1])], [BF16], name="ffn1_dwgate", ta=True)
    (part_g1,) = to_sibling(["ffn1_gate"], "grads_to_sibling_ffn1_gate")
    plan, group = to_chips(["ffn1_gate"], [part_g1])
    grads["ffn1_up"], got = _matmul([(du1, saved1[1])], [BF16], name="ffn1_dwup", ta=True, comm=plan)
    take(group, got)
    (part_u1,) = to_sibling(["ffn1_up"], "grads_to_sibling_ffn1_up")
    plan, group = to_chips(["ffn1_up"], [part_u1])
    grads["ffn1_down"], got = _matmul([(saved1[5], dyb1)], [BF16], name="ffn1_dwdown", ta=True, comm=plan)
    take(group, got)
    (part_d1,) = to_sibling(["ffn1_down"], "grads_to_sibling_ffn1_down")
    plan, group = to_chips(["ffn1_down"], [part_d1])
    take(group, _comm_only(plan, name="grads_to_chips_last"))
    grads["final_norm"] = g_final

    summed = {n: _sum_parts(from_chips[n], name=f"sum_{n}") for n in sharded}

    small = [n for n in names if n not in sharded]
    packed_g, layout = _pack([grads[n].reshape(weights[n].shape) for n in small] + [loss_part])
    (gathered,) = _comm_only(_Broadcast([packed_g]), name="small_all_gather")
    small_sum = _sum_parts(gathered, name="sum_small")
    unpacked = _unpack(small_sum, layout)
    for n, g in zip(small, unpacked[:-1]):
        summed[n] = g
    loss = unpacked[-1][0, 0]

    packed_w, _ = _pack([weights[n] for n in small] + [jnp.zeros_like(loss_part)])
    packed_m, _ = _pack([moments_m[n] for n in small] + [jnp.zeros_like(loss_part)])
    packed_v, _ = _pack([moments_v[n] for n in small] + [jnp.ones_like(loss_part)])
    small_out = [_unpack(o, layout) for o in _adamw(packed_w, small_sum, packed_m, packed_v, name="adamw_small")]
    delta, new_m, new_v = {}, {}, {}
    for i, n in enumerate(small):
        delta[n], new_m[n], new_v[n] = small_out[0][i], small_out[1][i], small_out[2][i]
    for n in sharded:
        shape = weights[n].shape
        if n in transposed:
            to2d, back = (lambda a: a.T), (lambda a: a.T)
        else:
            to2d, back = (lambda a: a.reshape(-1, shape[-1])), (lambda a: a.reshape(shape))
        d, nm, nv = _adamw(to2d(weights[n]), summed[n], to2d(moments_m[n]), to2d(moments_v[n]), name=f"adamw_{n}")
        summed[n], delta[n], new_m[n], new_v[n] = back(summed[n]), back(d), back(nm), back(nv)

    return (loss, dx.reshape(x.shape), *[summed[n] for n in names], *[delta[n] for n in names],
            *[new_m[n] for n in names], *[new_v[n] for n in names])
```

```python
import functools
import math

import jax
import jax.numpy as jnp
from jax import lax
from jax.experimental import pallas as pl
from jax.experimental.pallas import tpu as pltpu

F32 = jnp.float32
BF16 = jnp.bfloat16
MESH = pl.DeviceIdType.MESH
ANY = pl.BlockSpec(memory_space=pl.ANY)

N_DEV = 8
N_CHIP = 4
V7X_LANES = 128
V7X_SUBLANES = 8
V7X_VMEM_LIMIT_BYTES = 56 * 1024 * 1024
FF_PAD_MULTIPLE = 1024
ADAMW_BLOCK_BYTES = 1 << 20

NORM_EPS = 1e-6
POOL_WINDOWS = (2, 4, 8, 16)
ADAM_LR, ADAM_B1, ADAM_B2, ADAM_EPS, ADAM_WD, ADAM_STEP = 0.001, 0.9, 0.999, 1e-08, 0.01, 10
GELU_C = math.sqrt(2.0 / math.pi)
GELU_A = 0.044715
HI = lax.Precision.HIGHEST


def _tile(dim, pref, mult):
    t = min(pref, dim)
    t -= t % mult
    while t >= mult:
        if dim % t == 0:
            return t
        t -= mult
    return dim


def _pcall(body, *, name, grid, in_specs, out_specs, out_shape, operands, scratch_shapes=(), semantics=None, comm=None):
    if comm is None:
        params = pltpu.CompilerParams(dimension_semantics=semantics, vmem_limit_bytes=V7X_VMEM_LIMIT_BYTES)
        outs = pl.pallas_call(body, name=name, grid=grid, in_specs=list(in_specs), out_specs=list(out_specs),
                              out_shape=list(out_shape), scratch_shapes=list(scratch_shapes),
                              compiler_params=params)(*operands)
        return list(outs), []
    sizes = [len(in_specs), len(comm.inputs), len(out_shape), len(comm.out_shape), len(scratch_shapes),
             len(comm.sem_shapes)]

    def wrapped(*refs):
        groups, pos = [], 0
        for n in sizes:
            groups.append(refs[pos:pos + n])
            pos += n
        ins, c_ins, outs, c_outs, scratch, sems = groups
        if not grid:
            comm.start(c_ins, c_outs, sems)
            if body is not None:
                body(*ins, *outs, *scratch)
            comm.finish(c_ins, c_outs, sems)
            return
        ids = [pl.program_id(a) for a in range(len(grid))]
        first = functools.reduce(jnp.logical_and, [i == 0 for i in ids])
        last = functools.reduce(jnp.logical_and, [i == g - 1 for i, g in zip(ids, grid)])

        @pl.when(first)
        def _():
            comm.start(c_ins, c_outs, sems)

        body(*ins, *outs, *scratch)

        @pl.when(last)
        def _():
            comm.finish(c_ins, c_outs, sems)

    params = pltpu.CompilerParams(dimension_semantics=("arbitrary",) * len(grid), vmem_limit_bytes=V7X_VMEM_LIMIT_BYTES)
    res = pl.pallas_call(wrapped, name=name, grid=grid, in_specs=list(in_specs) + [ANY] * sizes[1],
                         out_specs=list(out_specs) + [ANY] * sizes[3], out_shape=list(out_shape) + list(comm.out_shape),
                         scratch_shapes=list(scratch_shapes) + list(comm.sem_shapes),
                         compiler_params=params)(*operands, *comm.inputs)
    return list(res[:sizes[2]]), list(res[sizes[2]:])


def _my_place():
    return lax.axis_index("x"), lax.axis_index("y"), lax.axis_index("c")


def _block_index(px, py, pc):
    return 4 * px + 2 * py + pc


class _Gather:
    def __init__(self, shards, pads):
        self.n, self.pads = len(shards), list(pads)
        zero_blocks = [jnp.zeros((p, s.shape[1]), s.dtype) for s, p in zip(shards, pads) if p]
        self.inputs = list(shards) + zero_blocks

        def full_shape(s, pad):
            if s.ndim == 2:
                return (N_DEV * s.shape[0] + pad, s.shape[1])
            return (s.shape[0], N_DEV * s.shape[1], s.shape[2])

        self.out_shape = [jax.ShapeDtypeStruct(full_shape(s, p), s.dtype) for s, p in zip(shards, pads)]
        self.sem_shapes = [pltpu.SemaphoreType.DMA((7 * self.n,)), pltpu.SemaphoreType.DMA((7 * self.n,)),
                           pltpu.SemaphoreType.DMA((self.n + len(zero_blocks),))]

    def _copies(self, ins, outs, sems):
        n = self.n
        send_sems, recv_sems, local_sems = sems
        x, y, c = _my_place()
        me, sibling = (x, y, c), (x, y, 1 - c)
        chips = [(1 - x, y), (x, 1 - y), (1 - x, 1 - y)]

        def rows(i, place):
            idx = _block_index(*place)
            r = ins[i].shape[-2]
            if ins[i].ndim == 2:
                return outs[i].at[pl.ds(idx * r, r), :]
            return outs[i].at[:, pl.ds(idx * r, r), :]

        def copy(i, k, block, to, src=None):
            return pltpu.make_async_remote_copy(
                src_ref=rows(i, block) if src is None else src, dst_ref=rows(i, block),
                send_sem=send_sems.at[7 * i + k], recv_sem=recv_sems.at[7 * i + k],
                device_id=to, device_id_type=MESH)

        local = [pltpu.make_async_copy(ins[i], rows(i, me), local_sems.at[i]) for i in range(n)]
        zi = 0
        for i in range(n):
            if self.pads[i]:
                start = N_DEV * ins[i].shape[0]
                local.append(pltpu.make_async_copy(ins[n + zi], outs[i].at[pl.ds(start, self.pads[i]), :],
                                                   local_sems.at[n + zi]))
                zi += 1
        first = []
        for i in range(n):
            first.append(copy(i, 0, me, sibling, src=ins[i]))
            first += [copy(i, 1 + j, me, (*chip, c), src=ins[i]) for j, chip in enumerate(chips)]
        return local, first, copy, chips, me, sibling, c

    def start(self, ins, outs, sems):
        local, first, *_ = self._copies(ins, outs, sems)
        for cp in local + first:
            cp.start()

    def finish(self, ins, outs, sems):
        local, first, copy, chips, me, sibling, c = self._copies(ins, outs, sems)
        passed = []
        for j, chip in enumerate(chips):
            for i in range(self.n):
                copy(i, 1 + j, (*chip, c), me).wait_recv()
                fwd = copy(i, 4 + j, (*chip, c), sibling)
                fwd.start()
                passed.append(fwd)
        for i in range(self.n):
            copy(i, 0, sibling, me).wait_recv()
        for j, chip in enumerate(chips):
            for i in range(self.n):
                copy(i, 4 + j, (*chip, 1 - c), me).wait_recv()
        for cp in first + passed:
            cp.wait_send()
        for cp in local:
            cp.wait()


class _SiblingSwap:
    def __init__(self, grads, rows):
        self.n, self.rows = len(grads), list(rows)
        self.inputs = list(grads)
        self.out_shape = [jax.ShapeDtypeStruct((N_CHIP * r, g.shape[1]), g.dtype) for g, r in zip(grads, rows)]
        self.sem_shapes = [pltpu.SemaphoreType.DMA((N_CHIP * self.n,))] * 2

    def _copies(self, ins, outs, sems):
        send_sems, recv_sems = sems
        x, y, c = _my_place()
        copies = []
        for i in range(self.n):
            r = self.rows[i]
            for q in range(N_CHIP):
                k = N_CHIP * i + q
                copies.append(pltpu.make_async_remote_copy(
                    src_ref=ins[i].at[pl.ds((2 * q + 1 - c) * r, r), :], dst_ref=outs[i].at[pl.ds(q * r, r), :],
                    send_sem=send_sems.at[k], recv_sem=recv_sems.at[k], device_id=(x, y, 1 - c), device_id_type=MESH))
        return copies

    def start(self, ins, outs, sems):
        for cp in self._copies(ins, outs, sems):
            cp.start()

    def finish(self, ins, outs, sems):
        copies = self._copies(ins, outs, sems)
        for cp in copies:
            cp.wait_recv()
        for cp in copies:
            cp.wait_send()


class _ChipScatter:
    FLIPS = [(0, 1), (1, 0), (1, 1)]

    def __init__(self, parts):
        self.n = len(parts)
        self.inputs = list(parts)
        self.out_shape = [jax.ShapeDtypeStruct((len(self.FLIPS) * (p.shape[0] // N_CHIP), p.shape[1]), p.dtype)
                          for p in parts]
        self.sem_shapes = [pltpu.SemaphoreType.DMA((3 * self.n,))] * 2

    def _copies(self, ins, outs, sems):
        send_sems, recv_sems = sems
        x, y, c = _my_place()
        copies = []
        for i in range(self.n):
            r = ins[i].shape[0] // N_CHIP
            for k, (fx, fy) in enumerate(self.FLIPS):
                px, py = x ^ fx, y ^ fy
                copies.append(pltpu.make_async_remote_copy(
                    src_ref=ins[i].at[pl.ds((2 * px + py) * r, r), :], dst_ref=outs[i].at[pl.ds(k * r, r), :],
                    send_sem=send_sems.at[3 * i + k], recv_sem=recv_sems.at[3 * i + k],
                    device_id=(px, py, c), device_id_type=MESH))
        return copies

    def start(self, ins, outs, sems):
        for cp in self._copies(ins, outs, sems):
            cp.start()

    def finish(self, ins, outs, sems):
        copies = self._copies(ins, outs, sems)
        for cp in copies:
            cp.wait_recv()
        for cp in copies:
            cp.wait_send()


_FLIPS = [(0, 0, 1), (0, 1, 0), (0, 1, 1), (1, 0, 0), (1, 0, 1), (1, 1, 0), (1, 1, 1)]


class _Broadcast:
    def __init__(self, arrays):
        self.n = len(arrays)
        self.inputs = list(arrays)
        self.out_shape = [jax.ShapeDtypeStruct((N_DEV, *a.shape), a.dtype) for a in arrays]
        self.sem_shapes = [pltpu.SemaphoreType.DMA((7 * self.n,)), pltpu.SemaphoreType.DMA((7 * self.n,)),
                           pltpu.SemaphoreType.DMA((self.n,))]

    def _copies(self, ins, outs, sems):
        send_sems, recv_sems, local_sems = sems
        x, y, c = _my_place()
        my_idx = _block_index(x, y, c)
        local, sends, recvs = [], [], []
        for i in range(self.n):
            local.append(pltpu.make_async_copy(ins[i], outs[i].at[my_idx], local_sems.at[i]))
            for k, (fx, fy, fc) in enumerate(_FLIPS):
                peer = (x ^ fx, y ^ fy, c ^ fc)
                common = dict(send_sem=send_sems.at[7 * i + k], recv_sem=recv_sems.at[7 * i + k],
                              device_id=peer, device_id_type=MESH)
                sends.append(pltpu.make_async_remote_copy(src_ref=ins[i], dst_ref=outs[i].at[my_idx], **common))
                recvs.append(pltpu.make_async_remote_copy(src_ref=ins[i], dst_ref=outs[i].at[_block_index(*peer)], **common))
        return local, sends, recvs

    def start(self, ins, outs, sems):
        local, sends, _ = self._copies(ins, outs, sems)
        for cp in local + sends:
            cp.start()

    def finish(self, ins, outs, sems):
        local, sends, recvs = self._copies(ins, outs, sems)
        for cp in recvs:
            cp.wait_recv()
        for cp in sends:
            cp.wait_send()
        for cp in local:
            cp.wait()


def _comm_only(comm, *, name):
    return _pcall(None, name=name, grid=(), in_specs=[], out_specs=[], out_shape=[], operands=[], comm=comm)[1]


def _matmul(pairs, out_dtypes, *, name, ta=False, tb=False, separate=False, epilogue=None,
            extras=(), bcast=(), tm=512, tn=512, tk=4096, precision=None, comm=None):
    a0, b0 = pairs[0]
    m_dim, k_dim = (a0.shape[1], a0.shape[0]) if ta else a0.shape
    n_dim = b0.shape[0] if tb else b0.shape[1]
    tm = _tile(m_dim, tm, V7X_LANES if ta else 16)
    tn = _tile(n_dim, tn, V7X_LANES)
    tk = _tile(k_dim, tk, V7X_LANES)
    nk = k_dim // tk
    n_acc = len(pairs) if separate else 1
    n_ex, n_bc, n_out = len(extras), len(bcast), len(out_dtypes)
    dims = (((0 if ta else 1,), (1 if tb else 0,)), ((), ()))
    if epilogue is None:
        epilogue = lambda accs, ex, bc: tuple(accs)
    a_spec = pl.BlockSpec((tk, tm), lambda m, n, k: (k, m)) if ta else pl.BlockSpec((tm, tk), lambda m, n, k: (m, k))
    b_spec = pl.BlockSpec((tn, tk), lambda m, n, k: (n, k)) if tb else pl.BlockSpec((tk, tn), lambda m, n, k: (k, n))
    operands, operand_specs, pair_slots = [], [], []
    for pair in pairs:
        slots = []
        for arr, spec in zip(pair, (a_spec, b_spec)):
            found = [i for i, o in enumerate(operands) if o is arr]
            if not found:
                operands.append(arr)
                operand_specs.append(spec)
                found = [len(operands) - 1]
            slots.append(found[0])
        pair_slots.append(slots)
    n_ops = len(operands)

    def body(*refs):
        a_refs = [refs[sa] for sa, _ in pair_slots]
        b_refs = [refs[sb] for _, sb in pair_slots]
        ex_refs = refs[n_ops:n_ops + n_ex]
        bc_refs = refs[n_ops + n_ex:n_ops + n_ex + n_bc]
        out_refs = refs[n_ops + n_ex + n_bc:n_ops + n_ex + n_bc + n_out]
        acc_refs = refs[n_ops + n_ex + n_bc + n_out:]
        parts = [lax.dot_general(a[...], b[...], dims, preferred_element_type=F32, precision=precision)
                 for a, b in zip(a_refs, b_refs)]
        if not separate:
            parts = [functools.reduce(lambda p, q: p + q, parts)]

        def finish(accs):
            outs = epilogue(accs, [e[...] for e in ex_refs], [c[...] for c in bc_refs])
            for o_ref, o in zip(out_refs, outs):
                o_ref[...] = o.astype(o_ref.dtype)

        if nk == 1:
            finish(parts)
        else:
            k = pl.program_id(2)

            @pl.when(k == 0)
            def _():
                for acc, p in zip(acc_refs, parts):
                    acc[...] = p

            @pl.when(k > 0)
            def _():
                for acc, p in zip(acc_refs, parts):
                    acc[...] += p

            @pl.when(k == nk - 1)
            def _():
                finish([acc[...] for acc in acc_refs])

    mn_spec = pl.BlockSpec((tm, tn), lambda m, n, k: (m, n))
    bc_spec = pl.BlockSpec((1, tn), lambda m, n, k: (0, n))
    outs, c_outs = _pcall(
        body, name=name, grid=(m_dim // tm, n_dim // tn, nk),
        in_specs=operand_specs + [mn_spec] * n_ex + [bc_spec] * n_bc,
        out_specs=[mn_spec] * n_out,
        out_shape=[jax.ShapeDtypeStruct((m_dim, n_dim), d) for d in out_dtypes],
        scratch_shapes=[pltpu.VMEM((tm, tn), F32)] * (n_acc if nk > 1 else 0),
        semantics=("parallel", "parallel", "arbitrary"),
        operands=[*operands, *extras, *bcast], comm=comm)
    outs = outs[0] if n_out == 1 else outs
    return outs if comm is None else (outs, c_outs)


def _with_comm(result, comm):
    return (result, []) if comm is None else result


def _bd(pairs, out_dtype, *, name, offs=None, tm=512, precision=None, comm=None):
    n_blocks, ka, kb = pairs[0][1].shape
    t_dim = pairs[0][0].shape[0]
    tm = _tile(t_dim, tm, 16)
    offs = offs or [0] * len(pairs)
    n_pairs = len(pairs)

    def body(*refs):
        acc = None
        for i in range(n_pairs):
            p = jnp.dot(refs[2 * i][...], refs[2 * i + 1][...], preferred_element_type=F32, precision=precision)
            acc = p if acc is None else acc + p
        refs[2 * n_pairs][...] = acc.astype(out_dtype)

    in_specs = []
    for off in offs:
        in_specs.append(pl.BlockSpec((tm, ka), lambda j, t, off=off: (t, j + off)))
        in_specs.append(pl.BlockSpec((None, ka, kb), lambda j, t: (j, 0, 0)))
    outs, c_outs = _pcall(
        body, name=name, grid=(n_blocks, t_dim // tm), in_specs=in_specs,
        out_specs=[pl.BlockSpec((tm, kb), lambda j, t: (t, j))],
        out_shape=[jax.ShapeDtypeStruct((t_dim, n_blocks * kb), out_dtype)],
        semantics=("parallel", "parallel"), operands=[t for p in pairs for t in p], comm=comm)
    return outs[0] if comm is None else (outs[0], c_outs)


def _bd_wgrad(a, b, ka, kb, n_blocks, *, name, off_a=0, off_b=0, tm=512, precision=None, sign=1.0):
    t_dim = a.shape[0]
    tm = _tile(t_dim, tm, 16)

    def body(a_ref, b_ref, o_ref):
        p = lax.dot_general(a_ref[...], b_ref[...], (((0,), (0,)), ((), ())), preferred_element_type=F32,
                            precision=precision)
        if sign != 1.0:
            p = p * sign

        @pl.when(pl.program_id(1) == 0)
        def _():
            o_ref[...] = p

        @pl.when(pl.program_id(1) > 0)
        def _():
            o_ref[...] += p

    return _pcall(
        body, name=name, grid=(n_blocks, t_dim // tm),
        in_specs=[pl.BlockSpec((tm, ka), lambda j, t: (t, j + off_a)),
                  pl.BlockSpec((tm, kb), lambda j, t: (t, j + off_b))],
        out_specs=[pl.BlockSpec((None, ka, kb), lambda j, t: (j, 0, 0))],
        out_shape=[jax.ShapeDtypeStruct((n_blocks, ka, kb), F32)],
        semantics=("parallel", "arbitrary"), operands=[a, b])[0][0]


def _rowk(fn, rows, bcast, outs, *, name, tm=256):
    rows = [r if isinstance(r, tuple) else (r, r.shape[1], 0) for r in rows]
    t_dim = rows[0][0].shape[0]
    tm = _tile(t_dim, tm, 16)
    n_rows, n_bc = len(rows), len(bcast)

    def body(*refs):
        ins = [r[...] for r in refs[:n_rows + n_bc]]
        vals = fn(*ins)
        first = pl.program_id(0) == 0
        for o_ref, v, spec in zip(refs[n_rows + n_bc:], vals, outs):
            if spec[0] == "row":
                o_ref[...] = v.astype(o_ref.dtype)
            else:
                @pl.when(first)
                def _(o_ref=o_ref, v=v):
                    o_ref[...] = v

                @pl.when(jnp.logical_not(first))
                def _(o_ref=o_ref, v=v):
                    o_ref[...] += v

    in_specs = [pl.BlockSpec((tm, w), lambda i, cb=cb: (i, cb)) for _, w, cb in rows]
    in_specs += [pl.BlockSpec((1, b.shape[1]), lambda i: (0, 0)) for b in bcast]
    out_specs, out_shape = [], []
    for spec in outs:
        if spec[0] == "row":
            out_specs.append(pl.BlockSpec((tm, spec[1]), lambda i: (i, 0)))
            out_shape.append(jax.ShapeDtypeStruct((t_dim, spec[1]), spec[2]))
        else:
            out_specs.append(pl.BlockSpec((1, spec[1]), lambda i: (0, 0)))
            out_shape.append(jax.ShapeDtypeStruct((1, spec[1]), F32))
    return _pcall(body, name=name, grid=(t_dim // tm,), in_specs=in_specs, out_specs=out_specs, out_shape=out_shape,
                  semantics=("arbitrary",), operands=[*[r[0] for r in rows], *bcast])[0]


def _colsum(v):
    return jnp.sum(v, axis=0, keepdims=True)


def _rms_fwd(x, g, *, name):
    def fn(x, g):
        r = lax.rsqrt(jnp.mean(x * x, axis=-1, keepdims=True) + NORM_EPS)
        return (x * r * g).astype(BF16), r
    d = x.shape[1]
    return _rowk(fn, [x], [g], [("row", d, BF16), ("row", 1, F32)], name=name)


def _rms_bwd_rows(dn, x, r, g):
    xh = x * r
    dy = dn * g
    dx = r * (dy - xh * jnp.mean(dy * xh, axis=-1, keepdims=True))
    return dx, _colsum(dn * xh)


def _silu_parts(g):
    sg = jax.nn.sigmoid(g)
    return g * sg, sg * (1.0 + g * (1.0 - sg))


def _ffn_fwd(h, norm_g, w_gate_t, w_up_t, get_w_down, tag, comm_gateup, comm_down):
    n, r = _rms_fwd(h, norm_g, name=f"{tag}_rms")

    def gate_up(accs, ex, bc):
        g, u = accs
        return g, u, _silu_parts(g)[0] * u

    (g, u, a), got1 = _matmul([(n, w_gate_t), (n, w_up_t)], [BF16, BF16, BF16], name=f"{tag}_gateup", tb=True,
                              separate=True, epilogue=gate_up, comm=comm_gateup)
    w_down = get_w_down(got1)
    out, got2 = _with_comm(_matmul([(a, w_down)], [F32], name=f"{tag}_down", extras=[h], tm=1024, tn=1024, tk=1024,
                                   epilogue=lambda accs, ex, bc: (ex[0] + 0.5 * accs[0],), comm=comm_down), comm_down)
    return out, (h, n, r, g, u, a), got1, got2


def _ffn_bwd_act(dyb, saved, w_down, tag, comm):
    g, u = saved[3], saved[4]

    def act_bwd(accs, ex, bc):
        da, g, u = accs[0], ex[0].astype(F32), ex[1].astype(F32)
        silu, dsilu = _silu_parts(g)
        return da * u * dsilu, da * silu

    return _matmul([(dyb, w_down)], [BF16, BF16], name=f"{tag}_dact", tb=True, extras=[g, u], epilogue=act_bwd, comm=comm)


def _ffn_bwd_input(dh, dg, du, saved, norm_g, w_gate_t, w_up_t, tag, comm):
    h, n, r = saved[0], saved[1], saved[2]
    dn, got = _with_comm(_matmul([(dg, w_gate_t), (du, w_up_t)], [F32], name=f"{tag}_dn", tm=1024, tn=1024, tk=1024,
                                 comm=comm), comm)

    def fn(dh, dn, h, r, gain):
        dx, dgain = _rms_bwd_rows(dn, h, r, gain)
        dx = dh + dx
        return dx, (0.5 * dx).astype(BF16), dx.astype(BF16), dgain

    d = h.shape[1]
    dx, dx_half_b, dx_b, d_norm = _rowk(fn, [dh, dn, h, r], [norm_g],
                                        [("row", d, F32), ("row", d, BF16), ("row", d, BF16), ("sum", d)],
                                        name=f"{tag}_rms_bwd")
    return dx, dx_half_b, dx_b, d_norm, got


def _pool_window(z, seq, width, group_width, *, name, transpose):
    assert POOL_WINDOWS == (2, 4, 8, 16)
    t_dim = z.shape[0]
    tc = _tile(group_width, 256, V7X_LANES)
    per_group = group_width // tc

    def body(z_ref, o_ref):
        gid = pl.program_id(1) // per_group
        v = z_ref[...]
        t = lax.broadcasted_iota(jnp.int32, v.shape, 0)
        win = jnp.where(gid == 0, 2, jnp.where(gid == 1, 4, jnp.where(gid == 2, 8, 16)))
        cnt = jnp.minimum(t + 1, win).astype(F32)
        if transpose:
            e = v / cnt
            shift = lambda q, k: jnp.where(t < seq - k, pltpu.roll(q, seq - k, 0), 0.0)
        else:
            e = v
            shift = lambda q, k: jnp.where(t >= k, pltpu.roll(q, k, 0), 0.0)
        s1 = e + shift(e, 1)
        s2 = s1 + shift(s1, 2)
        s3 = s2 + shift(s2, 4)
        s4 = s3 + shift(s3, 8)
        s = jnp.where(gid == 0, s1, jnp.where(gid == 1, s2, jnp.where(gid == 2, s3, s4)))
        if transpose:
            o_ref[...] = (s - v).astype(o_ref.dtype)
        else:
            o_ref[...] = (s / cnt - v).astype(o_ref.dtype)

    spec = pl.BlockSpec((seq, tc), lambda b, c: (b, c))
    return _pcall(body, name=name, grid=(t_dim // seq, width // tc), in_specs=[spec], out_specs=[spec],
                  out_shape=[jax.ShapeDtypeStruct((t_dim, width), F32 if transpose else BF16)],
                  semantics=("parallel", "parallel"), operands=[z])[0][0]


def _discretize(lam_re, lam_im, log_dt, b_re_t, b_im_t):
    dt = jnp.exp(log_dt)
    mag = jnp.exp(lam_re * dt)
    a_re = mag * jnp.cos(lam_im * dt)
    a_im = mag * jnp.sin(lam_im * dt)
    den = lam_re * lam_re + lam_im * lam_im
    f_re = ((a_re - 1.0) * lam_re + a_im * lam_im) / den
    f_im = (a_im * lam_re - (a_re - 1.0) * lam_im) / den
    f_re, f_im = f_re[:, None, :], f_im[:, None, :]
    return a_re, a_im, f_re * b_re_t - f_im * b_im_t, f_re * b_im_t + f_im * b_re_t


def _discretize_fwd(params):
    shapes = [jax.ShapeDtypeStruct(params[0].shape, F32)] * 2 + [jax.ShapeDtypeStruct(params[3].shape, F32)] * 2

    def body(*refs):
        outs = _discretize(*[r[...] for r in refs[:5]])
        for o_ref, o in zip(refs[5:], outs):
            o_ref[...] = o

    return pl.pallas_call(body, name="s5_discretize", out_shape=shapes,
                          compiler_params=pltpu.CompilerParams(vmem_limit_bytes=V7X_VMEM_LIMIT_BYTES))(*params)


def _discretize_bwd(params, cots):
    shapes = [jax.ShapeDtypeStruct(p.shape, F32) for p in params]

    def body(*refs):
        _, vjp = jax.vjp(_discretize, *[r[...] for r in refs[:5]])
        grads = vjp(tuple(r[...] for r in refs[5:9]))
        for o_ref, o in zip(refs[9:], grads):
            o_ref[...] = o

    return pl.pallas_call(body, name="s5_discretize_bwd", out_shape=shapes,
                          compiler_params=pltpu.CompilerParams(vmem_limit_bytes=V7X_VMEM_LIMIT_BYTES))(*params, *cots)


def _scan_tiles(t_dim, n_dim, seq):
    return _tile(seq, 256, V7X_SUBLANES), _tile(n_dim, 1024, V7X_LANES)


def _scan_fwd(bu_re, bu_im, a_re, a_im, seq, comm=None):
    t_dim, n_dim = bu_re.shape
    tt, tl = _scan_tiles(t_dim, n_dim, seq)
    per_seq = seq // tt

    def body(br_ref, bi_ref, ar_ref, ai_ref, xr_ref, xi_ref, cr_ref, ci_ref):
        @pl.when(pl.program_id(1) % per_seq == 0)
        def _():
            cr_ref[...] = jnp.zeros_like(cr_ref)
            ci_ref[...] = jnp.zeros_like(ci_ref)

        ar, ai = ar_ref[...], ai_ref[...]

        def step(i, carry):
            xr, xi = carry
            row = pl.ds(i, 1)
            nr = ar * xr - ai * xi + br_ref[row, :]
            ni = ai * xr + ar * xi + bi_ref[row, :]
            xr_ref[row, :] = nr
            xi_ref[row, :] = ni
            return nr, ni

        xr, xi = lax.fori_loop(0, tt, step, (cr_ref[...], ci_ref[...]), unroll=8)
        cr_ref[...] = xr
        ci_ref[...] = xi

    blk = pl.BlockSpec((tt, tl), lambda l, t: (t, l))
    vec = pl.BlockSpec((1, tl), lambda l, t: (0, l))
    outs, c_outs = _pcall(
        body, name="s5_scan", grid=(n_dim // tl, t_dim // tt), in_specs=[blk, blk, vec, vec], out_specs=[blk, blk],
        out_shape=[jax.ShapeDtypeStruct((t_dim, n_dim), F32)] * 2, scratch_shapes=[pltpu.VMEM((1, tl), F32)] * 2,
        semantics=("parallel", "arbitrary"), operands=[bu_re, bu_im, a_re, a_im], comm=comm)
    return outs, c_outs


def _scan_bwd(gx_re, gx_im, x_re, x_im, a_re, a_im, seq, comm=None):
    t_dim, n_dim = gx_re.shape
    tt, tl = _scan_tiles(t_dim, n_dim, seq)
    per_seq = seq // tt
    n_t = t_dim // tt
    prev_rows = V7X_SUBLANES

    def body(gr_ref, gi_ref, xr_ref, xi_ref, pr_ref, pi_ref, ar_ref, ai_ref,
             lr_ref, li_ref, dar_ref, dai_ref, cr_ref, ci_ref):
        step_id = pl.program_id(1)
        blk_id = n_t - 1 - step_id

        @pl.when((blk_id + 1) % per_seq == 0)
        def _():
            cr_ref[...] = jnp.zeros_like(cr_ref)
            ci_ref[...] = jnp.zeros_like(ci_ref)

        ar, ai = ar_ref[...], ai_ref[...]

        def step(j, carry):
            lr, li = carry
            row = pl.ds(tt - 1 - j, 1)
            nr = gr_ref[row, :] + ar * lr + ai * li
            ni = gi_ref[row, :] - ai * lr + ar * li
            lr_ref[row, :] = nr
            li_ref[row, :] = ni
            return nr, ni

        lr, li = lax.fori_loop(0, tt, step, (cr_ref[...], ci_ref[...]), unroll=8)
        cr_ref[...] = lr
        ci_ref[...] = li

        first_of_seq = blk_id % per_seq == 0
        keep = jnp.where(first_of_seq, 0.0, 1.0)
        t = lax.broadcasted_iota(jnp.int32, (tt, tl), 0)
        xr_prev = jnp.where(t == 0, pr_ref[prev_rows - 1:prev_rows, :] * keep, pltpu.roll(xr_ref[...], 1, 0))
        xi_prev = jnp.where(t == 0, pi_ref[prev_rows - 1:prev_rows, :] * keep, pltpu.roll(xi_ref[...], 1, 0))
        lam_r, lam_i = lr_ref[...], li_ref[...]
        d_re = _colsum(lam_r * xr_prev + lam_i * xi_prev)
        d_im = _colsum(lam_i * xr_prev - lam_r * xi_prev)

        @pl.when(step_id == 0)
        def _():
            dar_ref[...] = d_re
            dai_ref[...] = d_im

        @pl.when(step_id > 0)
        def _():
            dar_ref[...] += d_re
            dai_ref[...] += d_im

    blk = pl.BlockSpec((tt, tl), lambda l, t: (n_t - 1 - t, l))
    prev = pl.BlockSpec((prev_rows, tl), lambda l, t: (jnp.maximum((n_t - 1 - t) * (tt // prev_rows) - 1, 0), l))
    vec = pl.BlockSpec((1, tl), lambda l, t: (0, l))
    return _pcall(
        body, name="s5_scan_bwd", grid=(n_dim // tl, n_t),
        in_specs=[blk, blk, blk, blk, prev, prev, vec, vec], out_specs=[blk, blk, vec, vec],
        out_shape=[jax.ShapeDtypeStruct((t_dim, n_dim), F32)] * 2 + [jax.ShapeDtypeStruct((1, n_dim), F32)] * 2,
        scratch_shapes=[pltpu.VMEM((1, tl), F32)] * 2, semantics=("parallel", "arbitrary"),
        operands=[gx_re, gx_im, x_re, x_im, x_re, x_im, a_re, a_im], comm=comm)


def _block_diag(w, per_block):
    g, ka, kb = w.shape
    eye = jnp.eye(per_block, dtype=w.dtype)
    out = jnp.einsum("jakb,ac->jakcb", w.reshape(g // per_block, per_block, ka, kb), eye)
    return out.reshape(g // per_block, per_block * ka, per_block * kb)


def _block_diag_t(d, per_block, ka, kb):
    j = d.shape[0]
    eye = jnp.eye(per_block, dtype=d.dtype)
    picked = jnp.einsum("jakcb,ac->jakb", d.reshape(j, per_block, ka, per_block, kb), eye)
    return picked.reshape(j * per_block, ka, kb)


def _gelu_parts(y):
    inner = GELU_C * (y + GELU_A * y * y * y)
    th = jnp.tanh(inner)
    val = 0.5 * y * (1.0 + th)
    grad = 0.5 * (1.0 + th) + 0.5 * y * (1.0 - th * th) * GELU_C * (1.0 + 3.0 * GELU_A * y * y)
    return val, grad


def _chip_sum(grad, got, core, *, name):
    r, c = got.shape[0] // N_CHIP, got.shape[1]
    tr = _tile(r, 704, 16)
    tc = _tile(c, 2048, V7X_LANES)
    per = r // tr

    def body(core_ref, g_ref, s_ref, o_ref):
        o_ref[...] = (g_ref[...].astype(F32) + s_ref[...].astype(F32)).astype(o_ref.dtype)

    slot = pl.BlockSpec((tr, tc), lambda q, i, j, core: (q * per + i, j))
    grid_spec = pltpu.PrefetchScalarGridSpec(
        num_scalar_prefetch=1, grid=(N_CHIP, per, c // tc),
        in_specs=[pl.BlockSpec((tr, tc), lambda q, i, j, core: ((2 * q + core[0]) * per + i, j)), slot],
        out_specs=slot)
    return pl.pallas_call(
        body, name=name, grid_spec=grid_spec, out_shape=jax.ShapeDtypeStruct(got.shape, got.dtype),
        compiler_params=pltpu.CompilerParams(dimension_semantics=("parallel",) * 3,
                                             vmem_limit_bytes=V7X_VMEM_LIMIT_BYTES))(core, grad, got)


def _final_sum(parts, others, chip, *, name):
    r, c = parts.shape[0] // N_CHIP, parts.shape[1]
    n_others = others.shape[0] // r
    tr = _tile(r, 704, 16)
    tc = _tile(c, 1024, V7X_LANES)
    per = r // tr

    def body(chip_ref, p_ref, *refs):
        acc = p_ref[...].astype(F32)
        for o_ref in refs[:n_others]:
            acc = acc + o_ref[...].astype(F32)
        refs[n_others][...] = acc

    grid_spec = pltpu.PrefetchScalarGridSpec(
        num_scalar_prefetch=1, grid=(per, c // tc),
        in_specs=[pl.BlockSpec((tr, tc), lambda i, j, chip: (chip[0] * per + i, j))]
        + [pl.BlockSpec((tr, tc), lambda i, j, chip, s=s: (s * per + i, j)) for s in range(n_others)],
        out_specs=pl.BlockSpec((tr, tc), lambda i, j, chip: (i, j)))
    return pl.pallas_call(
        body, name=name, grid_spec=grid_spec, out_shape=jax.ShapeDtypeStruct((r, c), F32),
        compiler_params=pltpu.CompilerParams(dimension_semantics=("parallel",) * 2,
                                             vmem_limit_bytes=V7X_VMEM_LIMIT_BYTES))(chip, parts, *[others] * n_others)


def _sum_parts(parts, *, name):
    n_parts, r, c = parts.shape
    tr = _tile(r, 704, 16)
    tc = _tile(c, 1024, V7X_LANES)

    def body(p_ref, o_ref):
        acc = p_ref[0].astype(F32)
        for s in range(1, n_parts):
            acc = acc + p_ref[s].astype(F32)
        o_ref[...] = acc

    return _pcall(body, name=name, grid=(r // tr, c // tc),
                  in_specs=[pl.BlockSpec((n_parts, tr, tc), lambda i, j: (0, i, j))],
                  out_specs=[pl.BlockSpec((tr, tc), lambda i, j: (i, j))],
                  out_shape=[jax.ShapeDtypeStruct((r, c), F32)], semantics=("parallel", "parallel"),
                  operands=[parts])[0][0]


def _adamw(w, g, m, v, *, name):
    r, c = w.shape
    tr = _tile(r, max(V7X_SUBLANES, ADAMW_BLOCK_BYTES // (4 * c)), V7X_SUBLANES)
    c1 = 1.0 / (1.0 - ADAM_B1 ** ADAM_STEP)
    c2 = 1.0 / (1.0 - ADAM_B2 ** ADAM_STEP)

    def body(w_ref, g_ref, m_ref, v_ref, d_ref, nm_ref, nv_ref):
        g = g_ref[...]
        nm = ADAM_B1 * m_ref[...] + (1.0 - ADAM_B1) * g
        nv = ADAM_B2 * v_ref[...] + (1.0 - ADAM_B2) * (g * g)
        m_hat = nm * c1
        v_hat = nv * c2
        d_ref[...] = -ADAM_LR * (m_hat / (jnp.sqrt(v_hat) + ADAM_EPS) + ADAM_WD * w_ref[...])
        nm_ref[...] = nm
        nv_ref[...] = nv

    spec = pl.BlockSpec((tr, c), lambda i: (i, 0))
    return _pcall(body, name=name, grid=(r // tr,), in_specs=[spec] * 4, out_specs=[spec] * 3,
                  out_shape=[jax.ShapeDtypeStruct((r, c), F32)] * 3, semantics=("parallel",), operands=[w, g, m, v])[0]


def _pack(arrays, width=V7X_LANES):
    tile = V7X_SUBLANES * width
    parts, layout, row = [], [], 0
    for a in arrays:
        n = a.size
        rows = -(-n // tile) * V7X_SUBLANES
        flat = jnp.pad(a.reshape(-1).astype(F32), (0, rows * width - n))
        parts.append(flat.reshape(rows, width))
        layout.append((row, rows, n, a.shape))
        row += rows
    return jnp.concatenate(parts, axis=0), layout


def _unpack(packed, layout):
    return [packed[row:row + rows].reshape(-1)[:n].reshape(shape) for row, rows, n, shape in layout]


def kernel(x, ffn1_norm, ffn1_gate, ffn1_up, ffn1_down, mix_norm, w_in, w_pool, pool_scale, lam_re, lam_im, log_dt, b_re, b_im, c_re, c_im, d_skip, w_glu, b_glu, pool_out_norm, ssm_out_norm, w_out, ffn2_norm, ffn2_gate, ffn2_up, ffn2_down, final_norm, loss_target, m_ffn1_norm, m_ffn1_gate, m_ffn1_up, m_ffn1_down, m_mix_norm, m_w_in, m_w_pool, m_pool_scale, m_lam_re, m_lam_im, m_log_dt, m_b_re, m_b_im, m_c_re, m_c_im, m_d_skip, m_w_glu, m_b_glu, m_pool_out_norm, m_ssm_out_norm, m_w_out, m_ffn2_norm, m_ffn2_gate, m_ffn2_up, m_ffn2_down, m_final_norm, v_ffn1_norm, v_ffn1_gate, v_ffn1_up, v_ffn1_down, v_mix_norm, v_w_in, v_w_pool, v_pool_scale, v_lam_re, v_lam_im, v_log_dt, v_b_re, v_b_im, v_c_re, v_c_im, v_d_skip, v_w_glu, v_b_glu, v_pool_out_norm, v_ssm_out_norm, v_w_out, v_ffn2_norm, v_ffn2_gate, v_ffn2_up, v_ffn2_down, v_final_norm):
    weights = dict(ffn1_norm=ffn1_norm, ffn1_gate=ffn1_gate, ffn1_up=ffn1_up, ffn1_down=ffn1_down, mix_norm=mix_norm, w_in=w_in, w_pool=w_pool, pool_scale=pool_scale, lam_re=lam_re, lam_im=lam_im, log_dt=log_dt, b_re=b_re, b_im=b_im, c_re=c_re, c_im=c_im, d_skip=d_skip, w_glu=w_glu, b_glu=b_glu, pool_out_norm=pool_out_norm, ssm_out_norm=ssm_out_norm, w_out=w_out, ffn2_norm=ffn2_norm, ffn2_gate=ffn2_gate, ffn2_up=ffn2_up, ffn2_down=ffn2_down, final_norm=final_norm)
    moments_m = dict(ffn1_norm=m_ffn1_norm, ffn1_gate=m_ffn1_gate, ffn1_up=m_ffn1_up, ffn1_down=m_ffn1_down, mix_norm=m_mix_norm, w_in=m_w_in, w_pool=m_w_pool, pool_scale=m_pool_scale, lam_re=m_lam_re, lam_im=m_lam_im, log_dt=m_log_dt, b_re=m_b_re, b_im=m_b_im, c_re=m_c_re, c_im=m_c_im, d_skip=m_d_skip, w_glu=m_w_glu, b_glu=m_b_glu, pool_out_norm=m_pool_out_norm, ssm_out_norm=m_ssm_out_norm, w_out=m_w_out, ffn2_norm=m_ffn2_norm, ffn2_gate=m_ffn2_gate, ffn2_up=m_ffn2_up, ffn2_down=m_ffn2_down, final_norm=m_final_norm)
    moments_v = dict(ffn1_norm=v_ffn1_norm, ffn1_gate=v_ffn1_gate, ffn1_up=v_ffn1_up, ffn1_down=v_ffn1_down, mix_norm=v_mix_norm, w_in=v_w_in, w_pool=v_w_pool, pool_scale=v_pool_scale, lam_re=v_lam_re, lam_im=v_lam_im, log_dt=v_log_dt, b_re=v_b_re, b_im=v_b_im, c_re=v_c_re, c_im=v_c_im, d_skip=v_d_skip, w_glu=v_w_glu, b_glu=v_b_glu, pool_out_norm=v_pool_out_norm, ssm_out_norm=v_ssm_out_norm, w_out=v_w_out, ffn2_norm=v_ffn2_norm, ffn2_gate=v_ffn2_gate, ffn2_up=v_ffn2_up, ffn2_down=v_ffn2_down, final_norm=v_final_norm)
    names = list(weights)

    n_seq, seq, d_model = x.shape
    t_dim = n_seq * seq
    ff_shard = ffn1_gate.shape[1]
    ff = N_DEV * ff_shard
    ff_pad = -(-ff // FF_PAD_MULTIPLE) * FF_PAD_MULTIPLE - ff
    n_pool, pool_gw = w_pool.shape[0], w_pool.shape[2]
    pool_w = n_pool * pool_gw
    pool_rows = pool_gw // N_DEV
    n_grp, n_state, n_ch = b_re.shape
    ssm_w = n_grp * n_ch
    grp_per_blk = V7X_LANES // n_ch
    n_blk = n_grp // grp_per_blk
    ch_blk, st_blk = grp_per_blk * n_ch, grp_per_blk * n_state
    ssm_off = pool_w // ch_blk

    x2 = x.reshape(t_dim, d_model)
    tgt2 = loss_target.reshape(t_dim, d_model)
    row = lambda p: p.reshape(1, -1)

    sharded = ["ffn1_gate", "ffn1_up", "ffn1_down", "ffn2_gate", "ffn2_up", "ffn2_down", "w_in", "w_out", "w_glu", "w_pool"]
    transposed = {"ffn1_gate", "ffn1_up", "ffn2_gate", "ffn2_up"}
    shard = {n: (weights[n].T if n in transposed else weights[n]).astype(BF16) for n in sharded}

    def gather(*group):
        return _Gather([shard[n] for n in group], [ff_pad if n.startswith("ffn") else 0 for n in group])

    full = {}
    full["ffn1_gate"], full["ffn1_up"] = _comm_only(gather("ffn1_gate", "ffn1_up"), name="gather_first")

    def ffn1_down_weights(got):
        full["ffn1_down"], full["w_in"] = got
        return full["ffn1_down"]

    h1, saved1, _, got = _ffn_fwd(x2, row(ffn1_norm), full["ffn1_gate"], full["ffn1_up"], ffn1_down_weights, "ffn1",
                                  gather("ffn1_down", "w_in"), gather("ffn2_gate"))
    (full["ffn2_gate"],) = got
    n2, r2 = _rms_fwd(h1, row(mix_norm), name="mix_rms")
    z, (full["w_glu"], full["w_pool"]) = _matmul([(n2, full["w_in"])], [F32], name="mix_in", comm=gather("w_glu", "w_pool"))

    d_pool = _pool_window(z, seq, pool_w, pool_gw, name="pool_window", transpose=False)
    y_pool_lin = _bd([(d_pool, full["w_pool"])], F32, name="pool_mix")

    disc_params = (lam_re, lam_im, log_dt.reshape(n_grp, 1), jnp.swapaxes(b_re, 1, 2), jnp.swapaxes(b_im, 1, 2))
    a_re, a_im, bb_re_t, bb_im_t = _discretize_fwd(disc_params)
    a_re_row, a_im_row = row(a_re), row(a_im)
    wb_re, wb_im = _block_diag(bb_re_t, grp_per_blk), _block_diag(bb_im_t, grp_per_blk)
    wc_re = _block_diag(jnp.swapaxes(c_re, 1, 2), grp_per_blk)
    wc_im = _block_diag(jnp.swapaxes(c_im, 1, 2), grp_per_blk)
    bu_re = _bd([(z, wb_re)], F32, name="s5_bu_re", offs=[ssm_off], precision=HI)
    bu_im = _bd([(z, wb_im)], F32, name="s5_bu_im", offs=[ssm_off], precision=HI)
    (xs_re, xs_im), (full["w_out"],) = _scan_fwd(bu_re, bu_im, a_re_row, a_im_row, seq, comm=gather("w_out"))
    y_lin, (full["ffn2_up"],) = _bd([(xs_re, wc_re), (xs_im, -wc_im)], F32, name="s5_cx", precision=HI,
                                    comm=gather("ffn2_up"))

    def s5_post(y_lin, u, skip):
        y = y_lin + skip * u
        return y, _gelu_parts(y)[0].astype(BF16)

    u_cols = (z, ssm_w, pool_w // ssm_w)
    y_ssm, yg = _rowk(s5_post, [y_lin, u_cols], [row(d_skip)], [("row", ssm_w, F32), ("row", ssm_w, BF16)], name="s5_post")

    def glu(accs, ex, bc):
        q = accs[0] + bc[0]
        return q, ex[0].astype(F32) * jax.nn.sigmoid(q)

    q_glu, y_s5 = _matmul([(yg, full["w_glu"])], [F32, F32], name="s5_glu", extras=[yg], bcast=[row(b_glu)], epilogue=glu)

    def merge(yp_lin, ys, scale, gp, gs):
        yp = yp_lin * scale
        rp = lax.rsqrt(jnp.mean(yp * yp, axis=-1, keepdims=True) + NORM_EPS)
        rs = lax.rsqrt(jnp.mean(ys * ys, axis=-1, keepdims=True) + NORM_EPS)
        merged = jnp.concatenate([yp * rp * gp, ys * rs * gs], axis=-1)
        return merged.astype(BF16), rp, rs

    merged, r_pool, r_ssm = _rowk(merge, [y_pool_lin, y_s5], [row(pool_scale), row(pool_out_norm), row(ssm_out_norm)],
                                  [("row", pool_w + ssm_w, BF16), ("row", 1, F32), ("row", 1, F32)], name="mix_merge")
    h2 = _matmul([(merged, full["w_out"])], [F32], name="mix_out", extras=[h1],
                 epilogue=lambda accs, ex, bc: (ex[0] + accs[0],))

    def ffn2_down_weights(got):
        (full["ffn2_down"],) = got
        return full["ffn2_down"]

    h3, saved2, _, _ = _ffn_fwd(h2, row(ffn2_norm), full["ffn2_gate"], full["ffn2_up"], ffn2_down_weights, "ffn2",
                                gather("ffn2_down"), None)

    def head(h, tgt, gain):
        r = lax.rsqrt(jnp.mean(h * h, axis=-1, keepdims=True) + NORM_EPS)
        xh = h * r
        err = xh * gain - tgt
        loss = jnp.sum(0.5 * jnp.mean(err * err, axis=-1, keepdims=True), axis=0, keepdims=True)
        dout = err * (1.0 / d_model)
        dy = dout * gain
        dh = r * (dy - xh * jnp.mean(dy * xh, axis=-1, keepdims=True))
        return dh, (0.5 * dh).astype(BF16), _colsum(dout * xh), jnp.broadcast_to(loss, (1, V7X_LANES))

    dh3, dyb2, g_final, loss_part = _rowk(head, [h3, tgt2], [row(final_norm)],
                                          [("row", d_model, F32), ("row", d_model, BF16), ("sum", d_model), ("sum", V7X_LANES)],
                                          name="loss_head")

    grads = {}
    shard_rows = {n: (n_pool * pool_rows if n == "w_pool" else shard[n].shape[0]) for n in sharded}

    def to_sibling(group, name):
        got = _comm_only(_SiblingSwap([grads[n] for n in group], [shard_rows[n] for n in group]), name=name)
        for n, g in zip(group, got):
            chip_parts[n] = _chip_sum(grads[n], g, my_core, name=f"chip_sum_{n}")
        return [chip_parts[n] for n in group]

    my_core = lax.axis_index("c").astype(jnp.int32).reshape(1)
    my_chip = (2 * lax.axis_index("x") + lax.axis_index("y")).astype(jnp.int32).reshape(1)
    chip_parts, from_chips = {}, {}

    def to_chips(group, parts):
        return _ChipScatter(parts), group

    def take(group, got):
        for n, g in zip(group, got):
            from_chips[n] = g

    dg2, du2 = _ffn_bwd_act(dyb2, saved2, full["ffn2_down"], "ffn2", None)
    dh2, _, dh2b, grads["ffn2_norm"], _ = _ffn_bwd_input(dh3, dg2, du2, saved2, row(ffn2_norm), full["ffn2_gate"],
                                                        full["ffn2_up"], "ffn2", None)
    grads["ffn2_gate"] = _matmul([(dg2, saved2[1])], [BF16], name="ffn2_dwgate", ta=True)
    grads["ffn2_up"] = _matmul([(du2, saved2[1])], [BF16], name="ffn2_dwup", ta=True)
    grads["ffn2_down"] = _matmul([(saved2[5], dyb2)], [BF16], name="ffn2_dwdown", ta=True)
    part_g2, part_u2, part_d2 = to_sibling(["ffn2_gate", "ffn2_up", "ffn2_down"], "grads_to_sibling_ffn2")

    d_merged = _matmul([(dh2b, full["w_out"])], [F32], name="mix_out_dx", tb=True)
    grads["w_out"] = _matmul([(merged, dh2b)], [BF16], name="mix_out_dw", ta=True)

    def merge_bwd(dm, yp_lin, ys, rp, rs, scale, gp, gs):
        yp = yp_lin * scale
        d_yp, d_gp = _rms_bwd_rows(dm[:, :pool_w], yp, rp, gp)
        d_ys, d_gs = _rms_bwd_rows(dm[:, pool_w:], ys, rs, gs)
        return (d_yp * scale).astype(BF16), d_ys, _colsum(d_yp * yp_lin), d_gp, d_gs

    d_pool_lin, d_ys, grads["pool_scale"], grads["pool_out_norm"], grads["ssm_out_norm"] = _rowk(
        merge_bwd, [d_merged, y_pool_lin, y_s5, r_pool, r_ssm], [row(pool_scale), row(pool_out_norm), row(ssm_out_norm)],
        [("row", pool_w, BF16), ("row", ssm_w, F32), ("sum", pool_w), ("sum", pool_w), ("sum", ssm_w)], name="mix_merge_bwd")

    w_pool_t = jnp.swapaxes(full["w_pool"], 1, 2)
    dd_pool = _bd([(d_pool_lin, w_pool_t)], F32, name="pool_mix_dx")
    g_pool = _bd_wgrad(d_pool, d_pool_lin, pool_gw, pool_gw, n_pool, name="pool_mix_dw")
    grads["w_pool"] = g_pool.reshape(n_pool, N_DEV, pool_rows, pool_gw).transpose(1, 0, 2, 3).reshape(
        N_DEV * n_pool * pool_rows, pool_gw).astype(BF16)
    dz_pool = _pool_window(dd_pool, seq, pool_w, pool_gw, name="pool_window_bwd", transpose=True)

    def glu_bwd(d_ys, yg, q):
        sg = jax.nn.sigmoid(q)
        dq = d_ys * yg.astype(F32) * sg * (1.0 - sg)
        return dq.astype(BF16), d_ys * sg, _colsum(dq)

    dq, d_yg_direct, grads["b_glu"] = _rowk(glu_bwd, [d_ys, yg, q_glu], [],
                                            [("row", ssm_w, BF16), ("row", ssm_w, F32), ("sum", ssm_w)], name="s5_glu_bwd")
    d_yg_mm = _matmul([(dq, full["w_glu"])], [F32], name="s5_glu_dx", tb=True)
    grads["w_glu"] = _matmul([(yg, dq)], [BF16], name="s5_glu_dw", ta=True)

    def gelu_bwd(d1, d2, y, u, skip):
        dy = (d1 + d2) * _gelu_parts(y)[1]
        return dy, dy * skip, _colsum(dy * u)

    dy_ssm, du_skip, grads["d_skip"] = _rowk(gelu_bwd, [d_yg_direct, d_yg_mm, y_ssm, u_cols], [row(d_skip)],
                                             [("row", ssm_w, F32), ("row", ssm_w, F32), ("sum", ssm_w)], name="s5_gelu_bwd")
    wc_re_t, wc_im_t = jnp.swapaxes(wc_re, 1, 2), jnp.swapaxes(wc_im, 1, 2)
    gx_re = _bd([(dy_ssm, wc_re_t)], F32, name="s5_gx_re", precision=HI)
    gx_im = _bd([(dy_ssm, -wc_im_t)], F32, name="s5_gx_im", precision=HI)
    d_wc_re = _bd_wgrad(xs_re, dy_ssm, st_blk, ch_blk, n_blk, name="s5_dc_re", precision=HI)
    d_wc_im = _bd_wgrad(xs_im, dy_ssm, st_blk, ch_blk, n_blk, name="s5_dc_im", precision=HI, sign=-1.0)
    plan, group = to_chips(["ffn2_gate"], [part_g2])
    (lm_re, lm_im, da_re, da_im), got = _scan_bwd(gx_re, gx_im, xs_re, xs_im, a_re_row, a_im_row, seq, comm=plan)
    take(group, got)
    d_wb_re = _bd_wgrad(z, lm_re, ch_blk, st_blk, n_blk, name="s5_db_re", off_a=ssm_off, precision=HI)
    d_wb_im = _bd_wgrad(z, lm_im, ch_blk, st_blk, n_blk, name="s5_db_im", off_a=ssm_off, precision=HI)
    wb_re_t, wb_im_t = jnp.swapaxes(wb_re, 1, 2), jnp.swapaxes(wb_im, 1, 2)
    du_lin = _bd([(lm_re, wb_re_t), (lm_im, wb_im_t)], F32, name="s5_du", precision=HI)
    grads["c_re"] = jnp.swapaxes(_block_diag_t(d_wc_re, grp_per_blk, n_state, n_ch), 1, 2)
    grads["c_im"] = jnp.swapaxes(_block_diag_t(d_wc_im, grp_per_blk, n_state, n_ch), 1, 2)
    d_bb_re_t = _block_diag_t(d_wb_re, grp_per_blk, n_ch, n_state)
    d_bb_im_t = _block_diag_t(d_wb_im, grp_per_blk, n_ch, n_state)
    g_lam_re, g_lam_im, g_log_dt, g_b_re_t, g_b_im_t = _discretize_bwd(
        disc_params, (da_re.reshape(n_grp, n_state), da_im.reshape(n_grp, n_state), d_bb_re_t, d_bb_im_t))
    grads["lam_re"], grads["lam_im"], grads["log_dt"] = g_lam_re, g_lam_im, g_log_dt.reshape(n_grp)
    grads["b_re"], grads["b_im"] = jnp.swapaxes(g_b_re_t, 1, 2), jnp.swapaxes(g_b_im_t, 1, 2)

    def join(dzp, du1, du2):
        return (jnp.concatenate([dzp, du1 + du2], axis=-1).astype(BF16),)

    (dz,) = _rowk(join, [dz_pool, du_lin, du_skip], [], [("row", pool_w + ssm_w, BF16)], name="mix_in_join")
    dn2 = _matmul([(dz, full["w_in"])], [F32], name="mix_in_dx", tb=True)
    grads["w_in"] = _matmul([(n2, dz)], [BF16], name="mix_in_dw", ta=True)
    mixer = ["w_out", "w_glu", "w_pool", "w_in"]
    parts_mixer = to_sibling(mixer, "grads_to_sibling_mixer")

    def mix_rms_bwd(dh, dn, h, r, gain):
        dx, dgain = _rms_bwd_rows(dn, h, r, gain)
        dx = dh + dx
        return dx, (0.5 * dx).astype(BF16), dgain

    dh1, dyb1, grads["mix_norm"] = _rowk(mix_rms_bwd, [dh2, dn2, h1, r2], [row(mix_norm)],
                                         [("row", d_model, F32), ("row", d_model, BF16), ("sum", d_model)], name="mix_rms_bwd")

    plan, group = to_chips(["ffn2_up"], [part_u2])
    (dg1, du1), got = _ffn_bwd_act(dyb1, saved1, full["ffn1_down"], "ffn1", plan)
    take(group, got)
    plan, group = to_chips(["ffn2_down"] + mixer, [part_d2] + parts_mixer)
    dx, _, _, grads["ffn1_norm"], got = _ffn_bwd_input(dh1, dg1, du1, saved1, row(ffn1_norm), full["ffn1_gate"],
                                                       full["ffn1_up"], "ffn1", plan)
    take(group, got)
    grads["ffn1_gate"] = _matmul([(dg1, saved1[1])], [BF16], name="ffn1_dwgate", ta=True)
    (part_g1,) = to_sibling(["ffn1_gate"], "grads_to_sibling_ffn1_gate")
    plan, group = to_chips(["ffn1_gate"], [part_g1])
    grads["ffn1_up"], got = _matmul([(du1, saved1[1])], [BF16], name="ffn1_dwup", ta=True, comm=plan)
    take(group, got)
    (part_u1,) = to_sibling(["ffn1_up"], "grads_to_sibling_ffn1_up")
    plan, group = to_chips(["ffn1_up"], [part_u1])
    grads["ffn1_down"], got = _matmul([(saved1[5], dyb1)], [BF16], name="ffn1_dwdown", ta=True, comm=plan)
    take(group, got)
    (part_d1,) = to_sibling(["ffn1_down"], "grads_to_sibling_ffn1_down")
    plan, group = to_chips(["ffn1_down"], [part_d1])
    take(group, _comm_only(plan, name="grads_to_chips_last"))
    grads["final_norm"] = g_final

    summed = {n: _final_sum(chip_parts[n], from_chips[n], my_chip, name=f"sum_{n}") for n in sharded}

    small = [n for n in names if n not in sharded]
    packed_g, layout = _pack([grads[n].reshape(weights[n].shape) for n in small] + [loss_part])
    (gathered,) = _comm_only(_Broadcast([packed_g]), name="small_all_gather")
    small_sum = _sum_parts(gathered, name="sum_small")
    unpacked = _unpack(small_sum, layout)
    for n, g in zip(small, unpacked[:-1]):
        summed[n] = g
    loss = unpacked[-1][0, 0]

    packed_w, _ = _pack([weights[n] for n in small] + [jnp.zeros_like(loss_part)])
    packed_m, _ = _pack([moments_m[n] for n in small] + [jnp.zeros_like(loss_part)])
    packed_v, _ = _pack([moments_v[n] for n in small] + [jnp.ones_like(loss_part)])
    small_out = [_unpack(o, layout) for o in _adamw(packed_w, small_sum, packed_m, packed_v, name="adamw_small")]
    delta, new_m, new_v = {}, {}, {}
    for i, n in enumerate(small):
        delta[n], new_m[n], new_v[n] = small_out[0][i], small_out[1][i], small_out[2][i]
    for n in sharded:
        shape = weights[n].shape
        if n in transposed:
            to2d, back = (lambda a: a.T), (lambda a: a.T)
        else:
            to2d, back = (lambda a: a.reshape(-1, shape[-1])), (lambda a: a.reshape(shape))
        d, nm, nv = _adamw(to2d(weights[n]), summed[n], to2d(moments_m[n]), to2d(moments_v[n]), name=f"adamw_{n}")
        summed[n], delta[n], new_m[n], new_v[n] = back(summed[n]), back(d), back(nm), back(nv)

    return (loss, dx.reshape(x.shape), *[summed[n] for n in names], *[delta[n] for n in names],
            *[new_m[n] for n in names], *[new_v[n] for n in names])
```

```python
import functools
import math

import jax
import jax.numpy as jnp
from jax import lax
from jax.experimental import pallas as pl
from jax.experimental.pallas import tpu as pltpu

F32 = jnp.float32
BF16 = jnp.bfloat16
MESH = pl.DeviceIdType.MESH
ANY = pl.BlockSpec(memory_space=pl.ANY)

N_DEV = 8
N_CHIP = 4
V7X_LANES = 128
V7X_SUBLANES = 8
V7X_VMEM_LIMIT_BYTES = 56 * 1024 * 1024
FF_PAD_MULTIPLE = 1024
ADAMW_BLOCK_BYTES = 1 << 20

NORM_EPS = 1e-6
POOL_WINDOWS = (2, 4, 8, 16)
ADAM_LR, ADAM_B1, ADAM_B2, ADAM_EPS, ADAM_WD, ADAM_STEP = 0.001, 0.9, 0.999, 1e-08, 0.01, 10
GELU_C = math.sqrt(2.0 / math.pi)
GELU_A = 0.044715
HI = lax.Precision.HIGH


def _tile(dim, pref, mult):
    t = min(pref, dim)
    t -= t % mult
    while t >= mult:
        if dim % t == 0:
            return t
        t -= mult
    return dim


def _pcall(body, *, name, grid, in_specs, out_specs, out_shape, operands, scratch_shapes=(), semantics=None, comm=None):
    if comm is None:
        params = pltpu.CompilerParams(dimension_semantics=semantics, vmem_limit_bytes=V7X_VMEM_LIMIT_BYTES)
        outs = pl.pallas_call(body, name=name, grid=grid, in_specs=list(in_specs), out_specs=list(out_specs),
                              out_shape=list(out_shape), scratch_shapes=list(scratch_shapes),
                              compiler_params=params)(*operands)
        return list(outs), []
    sizes = [len(in_specs), len(comm.inputs), len(out_shape), len(comm.out_shape), len(scratch_shapes),
             len(comm.sem_shapes)]

    def wrapped(*refs):
        groups, pos = [], 0
        for n in sizes:
            groups.append(refs[pos:pos + n])
            pos += n
        ins, c_ins, outs, c_outs, scratch, sems = groups
        if not grid:
            comm.start(c_ins, c_outs, sems)
            if body is not None:
                body(*ins, *outs, *scratch)
            comm.finish(c_ins, c_outs, sems)
            return
        ids = [pl.program_id(a) for a in range(len(grid))]
        first = functools.reduce(jnp.logical_and, [i == 0 for i in ids])
        last = functools.reduce(jnp.logical_and, [i == g - 1 for i, g in zip(ids, grid)])

        @pl.when(first)
        def _():
            comm.start(c_ins, c_outs, sems)

        body(*ins, *outs, *scratch)

        @pl.when(last)
        def _():
            comm.finish(c_ins, c_outs, sems)

    params = pltpu.CompilerParams(dimension_semantics=("arbitrary",) * len(grid), vmem_limit_bytes=V7X_VMEM_LIMIT_BYTES)
    res = pl.pallas_call(wrapped, name=name, grid=grid, in_specs=list(in_specs) + [ANY] * sizes[1],
                         out_specs=list(out_specs) + [ANY] * sizes[3], out_shape=list(out_shape) + list(comm.out_shape),
                         scratch_shapes=list(scratch_shapes) + list(comm.sem_shapes),
                         compiler_params=params)(*operands, *comm.inputs)
    return list(res[:sizes[2]]), list(res[sizes[2]:])


def _my_place():
    return lax.axis_index("x"), lax.axis_index("y"), lax.axis_index("c")


def _block_index(px, py, pc):
    return 4 * px + 2 * py + pc


class _Gather:
    def __init__(self, shards, pads):
        self.n, self.pads = len(shards), list(pads)
        zero_blocks = [jnp.zeros((p, s.shape[1]), s.dtype) for s, p in zip(shards, pads) if p]
        self.inputs = list(shards) + zero_blocks

        def full_shape(s, pad):
            if s.ndim == 2:
                return (N_DEV * s.shape[0] + pad, s.shape[1])
            return (s.shape[0], N_DEV * s.shape[1], s.shape[2])

        self.out_shape = [jax.ShapeDtypeStruct(full_shape(s, p), s.dtype) for s, p in zip(shards, pads)]
        self.sem_shapes = [pltpu.SemaphoreType.DMA((7 * self.n,)), pltpu.SemaphoreType.DMA((7 * self.n,)),
                           pltpu.SemaphoreType.DMA((self.n + len(zero_blocks),))]

    def _copies(self, ins, outs, sems):
        n = self.n
        send_sems, recv_sems, local_sems = sems
        x, y, c = _my_place()
        me, sibling = (x, y, c), (x, y, 1 - c)
        chips = [(1 - x, y), (x, 1 - y), (1 - x, 1 - y)]

        def rows(i, place):
            idx = _block_index(*place)
            r = ins[i].shape[-2]
            if ins[i].ndim == 2:
                return outs[i].at[pl.ds(idx * r, r), :]
            return outs[i].at[:, pl.ds(idx * r, r), :]

        def copy(i, k, block, to, src=None):
            return pltpu.make_async_remote_copy(
                src_ref=rows(i, block) if src is None else src, dst_ref=rows(i, block),
                send_sem=send_sems.at[7 * i + k], recv_sem=recv_sems.at[7 * i + k],
                device_id=to, device_id_type=MESH)

        local = [pltpu.make_async_copy(ins[i], rows(i, me), local_sems.at[i]) for i in range(n)]
        zi = 0
        for i in range(n):
            if self.pads[i]:
                start = N_DEV * ins[i].shape[0]
                local.append(pltpu.make_async_copy(ins[n + zi], outs[i].at[pl.ds(start, self.pads[i]), :],
                                                   local_sems.at[n + zi]))
                zi += 1
        first = []
        for i in range(n):
            first.append(copy(i, 0, me, sibling, src=ins[i]))
            first += [copy(i, 1 + j, me, (*chip, c), src=ins[i]) for j, chip in enumerate(chips)]
        return local, first, copy, chips, me, sibling, c

    def start(self, ins, outs, sems):
        local, first, *_ = self._copies(ins, outs, sems)
        for cp in local + first:
            cp.start()

    def finish(self, ins, outs, sems):
        local, first, copy, chips, me, sibling, c = self._copies(ins, outs, sems)
        passed = []
        for j, chip in enumerate(chips):
            for i in range(self.n):
                copy(i, 1 + j, (*chip, c), me).wait_recv()
                fwd = copy(i, 4 + j, (*chip, c), sibling)
                fwd.start()
                passed.append(fwd)
        for i in range(self.n):
            copy(i, 0, sibling, me).wait_recv()
        for j, chip in enumerate(chips):
            for i in range(self.n):
                copy(i, 4 + j, (*chip, 1 - c), me).wait_recv()
        for cp in first + passed:
            cp.wait_send()
        for cp in local:
            cp.wait()


class _SiblingSwap:
    def __init__(self, grads, rows):
        self.n, self.rows = len(grads), list(rows)
        self.inputs = list(grads)
        self.out_shape = [jax.ShapeDtypeStruct((N_CHIP * r, g.shape[1]), g.dtype) for g, r in zip(grads, rows)]
        self.sem_shapes = [pltpu.SemaphoreType.DMA((N_CHIP * self.n,))] * 2

    def _copies(self, ins, outs, sems):
        send_sems, recv_sems = sems
        x, y, c = _my_place()
        copies = []
        for i in range(self.n):
            r = self.rows[i]
            for q in range(N_CHIP):
                k = N_CHIP * i + q
                copies.append(pltpu.make_async_remote_copy(
                    src_ref=ins[i].at[pl.ds((2 * q + 1 - c) * r, r), :], dst_ref=outs[i].at[pl.ds(q * r, r), :],
                    send_sem=send_sems.at[k], recv_sem=recv_sems.at[k], device_id=(x, y, 1 - c), device_id_type=MESH))
        return copies

    def start(self, ins, outs, sems):
        for cp in self._copies(ins, outs, sems):
            cp.start()

    def finish(self, ins, outs, sems):
        copies = self._copies(ins, outs, sems)
        for cp in copies:
            cp.wait_recv()
        for cp in copies:
            cp.wait_send()


class _ChipScatter:
    FLIPS = [(0, 1), (1, 0), (1, 1)]

    def __init__(self, parts):
        self.n = len(parts)
        self.inputs = list(parts)
        self.out_shape = [jax.ShapeDtypeStruct((len(self.FLIPS) * (p.shape[0] // N_CHIP), p.shape[1]), p.dtype)
                          for p in parts]
        self.sem_shapes = [pltpu.SemaphoreType.DMA((3 * self.n,))] * 2

    def _copies(self, ins, outs, sems):
        send_sems, recv_sems = sems
        x, y, c = _my_place()
        copies = []
        for i in range(self.n):
            r = ins[i].shape[0] // N_CHIP
            for k, (fx, fy) in enumerate(self.FLIPS):
                px, py = x ^ fx, y ^ fy
                copies.append(pltpu.make_async_remote_copy(
                    src_ref=ins[i].at[pl.ds((2 * px + py) * r, r), :], dst_ref=outs[i].at[pl.ds(k * r, r), :],
                    send_sem=send_sems.at[3 * i + k], recv_sem=recv_sems.at[3 * i + k],
                    device_id=(px, py, c), device_id_type=MESH))
        return copies

    def start(self, ins, outs, sems):
        for cp in self._copies(ins, outs, sems):
            cp.start()

    def finish(self, ins, outs, sems):
        copies = self._copies(ins, outs, sems)
        for cp in copies:
            cp.wait_recv()
        for cp in copies:
            cp.wait_send()


_FLIPS = [(0, 0, 1), (0, 1, 0), (0, 1, 1), (1, 0, 0), (1, 0, 1), (1, 1, 0), (1, 1, 1)]


class _Broadcast:
    def __init__(self, arrays):
        self.n = len(arrays)
        self.inputs = list(arrays)
        self.out_shape = [jax.ShapeDtypeStruct((N_DEV, *a.shape), a.dtype) for a in arrays]
        self.sem_shapes = [pltpu.SemaphoreType.DMA((7 * self.n,)), pltpu.SemaphoreType.DMA((7 * self.n,)),
                           pltpu.SemaphoreType.DMA((self.n,))]

    def _copies(self, ins, outs, sems):
        send_sems, recv_sems, local_sems = sems
        x, y, c = _my_place()
        my_idx = _block_index(x, y, c)
        local, sends, recvs = [], [], []
        for i in range(self.n):
            local.append(pltpu.make_async_copy(ins[i], outs[i].at[my_idx], local_sems.at[i]))
            for k, (fx, fy, fc) in enumerate(_FLIPS):
                peer = (x ^ fx, y ^ fy, c ^ fc)
                common = dict(send_sem=send_sems.at[7 * i + k], recv_sem=recv_sems.at[7 * i + k],
                              device_id=peer, device_id_type=MESH)
                sends.append(pltpu.make_async_remote_copy(src_ref=ins[i], dst_ref=outs[i].at[my_idx], **common))
                recvs.append(pltpu.make_async_remote_copy(src_ref=ins[i], dst_ref=outs[i].at[_block_index(*peer)], **common))
        return local, sends, recvs

    def start(self, ins, outs, sems):
        local, sends, _ = self._copies(ins, outs, sems)
        for cp in local + sends:
            cp.start()

    def finish(self, ins, outs, sems):
        local, sends, recvs = self._copies(ins, outs, sems)
        for cp in recvs:
            cp.wait_recv()
        for cp in sends:
            cp.wait_send()
        for cp in local:
            cp.wait()


def _comm_only(comm, *, name):
    return _pcall(None, name=name, grid=(), in_specs=[], out_specs=[], out_shape=[], operands=[], comm=comm)[1]


def _matmul(pairs, out_dtypes, *, name, ta=False, tb=False, separate=False, epilogue=None,
            extras=(), bcast=(), tm=512, tn=512, tk=4096, precision=None, comm=None):
    a0, b0 = pairs[0]
    m_dim, k_dim = (a0.shape[1], a0.shape[0]) if ta else a0.shape
    n_dim = b0.shape[0] if tb else b0.shape[1]
    tm = _tile(m_dim, tm, V7X_LANES if ta else 16)
    tn = _tile(n_dim, tn, V7X_LANES)
    tk = _tile(k_dim, tk, V7X_LANES)
    nk = k_dim // tk
    n_acc = len(pairs) if separate else 1
    n_ex, n_bc, n_out = len(extras), len(bcast), len(out_dtypes)
    dims = (((0 if ta else 1,), (1 if tb else 0,)), ((), ()))
    if epilogue is None:
        epilogue = lambda accs, ex, bc: tuple(accs)
    a_spec = pl.BlockSpec((tk, tm), lambda m, n, k: (k, m)) if ta else pl.BlockSpec((tm, tk), lambda m, n, k: (m, k))
    b_spec = pl.BlockSpec((tn, tk), lambda m, n, k: (n, k)) if tb else pl.BlockSpec((tk, tn), lambda m, n, k: (k, n))
    operands, operand_specs, pair_slots = [], [], []
    for pair in pairs:
        slots = []
        for arr, spec in zip(pair, (a_spec, b_spec)):
            found = [i for i, o in enumerate(operands) if o is arr]
            if not found:
                operands.append(arr)
                operand_specs.append(spec)
                found = [len(operands) - 1]
            slots.append(found[0])
        pair_slots.append(slots)
    n_ops = len(operands)

    def body(*refs):
        a_refs = [refs[sa] for sa, _ in pair_slots]
        b_refs = [refs[sb] for _, sb in pair_slots]
        ex_refs = refs[n_ops:n_ops + n_ex]
        bc_refs = refs[n_ops + n_ex:n_ops + n_ex + n_bc]
        out_refs = refs[n_ops + n_ex + n_bc:n_ops + n_ex + n_bc + n_out]
        acc_refs = refs[n_ops + n_ex + n_bc + n_out:]
        parts = [lax.dot_general(a[...], b[...], dims, preferred_element_type=F32, precision=precision)
                 for a, b in zip(a_refs, b_refs)]
        if not separate:
            parts = [functools.reduce(lambda p, q: p + q, parts)]

        def finish(accs):
            outs = epilogue(accs, [e[...] for e in ex_refs], [c[...] for c in bc_refs])
            for o_ref, o in zip(out_refs, outs):
                o_ref[...] = o.astype(o_ref.dtype)

        if nk == 1:
            finish(parts)
        else:
            k = pl.program_id(2)

            @pl.when(k == 0)
            def _():
                for acc, p in zip(acc_refs, parts):
                    acc[...] = p

            @pl.when(k > 0)
            def _():
                for acc, p in zip(acc_refs, parts):
                    acc[...] += p

            @pl.when(k == nk - 1)
            def _():
                finish([acc[...] for acc in acc_refs])

    mn_spec = pl.BlockSpec((tm, tn), lambda m, n, k: (m, n))
    bc_spec = pl.BlockSpec((1, tn), lambda m, n, k: (0, n))
    outs, c_outs = _pcall(
        body, name=name, grid=(m_dim // tm, n_dim // tn, nk),
        in_specs=operand_specs + [mn_spec] * n_ex + [bc_spec] * n_bc,
        out_specs=[mn_spec] * n_out,
        out_shape=[jax.ShapeDtypeStruct((m_dim, n_dim), d) for d in out_dtypes],
        scratch_shapes=[pltpu.VMEM((tm, tn), F32)] * (n_acc if nk > 1 else 0),
        semantics=("parallel", "parallel", "arbitrary"),
        operands=[*operands, *extras, *bcast], comm=comm)
    outs = outs[0] if n_out == 1 else outs
    return outs if comm is None else (outs, c_outs)


def _with_comm(result, comm):
    return (result, []) if comm is None else result


def _bd(pairs, out_dtype, *, name, offs=None, tm=512, precision=None, comm=None):
    n_blocks, ka, kb = pairs[0][1].shape
    t_dim = pairs[0][0].shape[0]
    tm = _tile(t_dim, tm, 16)
    offs = offs or [0] * len(pairs)
    n_pairs = len(pairs)

    def body(*refs):
        acc = None
        for i in range(n_pairs):
            p = jnp.dot(refs[2 * i][...], refs[2 * i + 1][...], preferred_element_type=F32, precision=precision)
            acc = p if acc is None else acc + p
        refs[2 * n_pairs][...] = acc.astype(out_dtype)

    in_specs = []
    for off in offs:
        in_specs.append(pl.BlockSpec((tm, ka), lambda j, t, off=off: (t, j + off)))
        in_specs.append(pl.BlockSpec((None, ka, kb), lambda j, t: (j, 0, 0)))
    outs, c_outs = _pcall(
        body, name=name, grid=(n_blocks, t_dim // tm), in_specs=in_specs,
        out_specs=[pl.BlockSpec((tm, kb), lambda j, t: (t, j))],
        out_shape=[jax.ShapeDtypeStruct((t_dim, n_blocks * kb), out_dtype)],
        semantics=("parallel", "parallel"), operands=[t for p in pairs for t in p], comm=comm)
    return outs[0] if comm is None else (outs[0], c_outs)


def _bd_wgrad(a, b, ka, kb, n_blocks, *, name, off_a=0, off_b=0, tm=512, precision=None, sign=1.0):
    t_dim = a.shape[0]
    tm = _tile(t_dim, tm, 16)

    def body(a_ref, b_ref, o_ref):
        p = lax.dot_general(a_ref[...], b_ref[...], (((0,), (0,)), ((), ())), preferred_element_type=F32,
                            precision=precision)
        if sign != 1.0:
            p = p * sign

        @pl.when(pl.program_id(1) == 0)
        def _():
            o_ref[...] = p

        @pl.when(pl.program_id(1) > 0)
        def _():
            o_ref[...] += p

    return _pcall(
        body, name=name, grid=(n_blocks, t_dim // tm),
        in_specs=[pl.BlockSpec((tm, ka), lambda j, t: (t, j + off_a)),
                  pl.BlockSpec((tm, kb), lambda j, t: (t, j + off_b))],
        out_specs=[pl.BlockSpec((None, ka, kb), lambda j, t: (j, 0, 0))],
        out_shape=[jax.ShapeDtypeStruct((n_blocks, ka, kb), F32)],
        semantics=("parallel", "arbitrary"), operands=[a, b])[0][0]


def _rowk(fn, rows, bcast, outs, *, name, tm=256):
    rows = [r if isinstance(r, tuple) else (r, r.shape[1], 0) for r in rows]
    t_dim = rows[0][0].shape[0]
    tm = _tile(t_dim, tm, 16)
    n_rows, n_bc = len(rows), len(bcast)

    def body(*refs):
        ins = [r[...] for r in refs[:n_rows + n_bc]]
        vals = fn(*ins)
        first = pl.program_id(0) == 0
        for o_ref, v, spec in zip(refs[n_rows + n_bc:], vals, outs):
            if spec[0] == "row":
                o_ref[...] = v.astype(o_ref.dtype)
            else:
                @pl.when(first)
                def _(o_ref=o_ref, v=v):
                    o_ref[...] = v

                @pl.when(jnp.logical_not(first))
                def _(o_ref=o_ref, v=v):
                    o_ref[...] += v

    in_specs = [pl.BlockSpec((tm, w), lambda i, cb=cb: (i, cb)) for _, w, cb in rows]
    in_specs += [pl.BlockSpec((1, b.shape[1]), lambda i: (0, 0)) for b in bcast]
    out_specs, out_shape = [], []
    for spec in outs:
        if spec[0] == "row":
            out_specs.append(pl.BlockSpec((tm, spec[1]), lambda i: (i, 0)))
            out_shape.append(jax.ShapeDtypeStruct((t_dim, spec[1]), spec[2]))
        else:
            out_specs.append(pl.BlockSpec((1, spec[1]), lambda i: (0, 0)))
            out_shape.append(jax.ShapeDtypeStruct((1, spec[1]), F32))
    return _pcall(body, name=name, grid=(t_dim // tm,), in_specs=in_specs, out_specs=out_specs, out_shape=out_shape,
                  semantics=("arbitrary",), operands=[*[r[0] for r in rows], *bcast])[0]


def _colsum(v):
    return jnp.sum(v, axis=0, keepdims=True)


def _rms_fwd(x, g, *, name):
    def fn(x, g):
        r = lax.rsqrt(jnp.mean(x * x, axis=-1, keepdims=True) + NORM_EPS)
        return (x * r * g).astype(BF16), r
    d = x.shape[1]
    return _rowk(fn, [x], [g], [("row", d, BF16), ("row", 1, F32)], name=name)


def _rms_bwd_rows(dn, x, r, g):
    xh = x * r
    dy = dn * g
    dx = r * (dy - xh * jnp.mean(dy * xh, axis=-1, keepdims=True))
    return dx, _colsum(dn * xh)


def _silu_parts(g):
    sg = jax.nn.sigmoid(g)
    return g * sg, sg * (1.0 + g * (1.0 - sg))


def _ffn_fwd(h, norm_g, w_gate_t, w_up_t, get_w_down, tag, comm_gateup, comm_down):
    n, r = _rms_fwd(h, norm_g, name=f"{tag}_rms")

    def gate_up(accs, ex, bc):
        g, u = accs
        return g, u, _silu_parts(g)[0] * u

    (g, u, a), got1 = _matmul([(n, w_gate_t), (n, w_up_t)], [BF16, BF16, BF16], name=f"{tag}_gateup", tb=True,
                              separate=True, epilogue=gate_up, comm=comm_gateup)
    w_down = get_w_down(got1)
    out, got2 = _with_comm(_matmul([(a, w_down)], [F32], name=f"{tag}_down", extras=[h], tm=1024, tn=1024, tk=1024,
                                   epilogue=lambda accs, ex, bc: (ex[0] + 0.5 * accs[0],), comm=comm_down), comm_down)
    return out, (h, n, r, g, u, a), got1, got2


def _ffn_bwd_act(dyb, saved, w_down, tag, comm):
    g, u = saved[3], saved[4]

    def act_bwd(accs, ex, bc):
        da, g, u = accs[0], ex[0].astype(F32), ex[1].astype(F32)
        silu, dsilu = _silu_parts(g)
        return da * u * dsilu, da * silu

    return _matmul([(dyb, w_down)], [BF16, BF16], name=f"{tag}_dact", tb=True, extras=[g, u], epilogue=act_bwd, comm=comm)


def _ffn_bwd_input(dh, dg, du, saved, norm_g, w_gate_t, w_up_t, tag, comm):
    h, n, r = saved[0], saved[1], saved[2]
    dn, got = _with_comm(_matmul([(dg, w_gate_t), (du, w_up_t)], [F32], name=f"{tag}_dn", tm=1024, tn=1024, tk=1024,
                                 comm=comm), comm)

    def fn(dh, dn, h, r, gain):
        dx, dgain = _rms_bwd_rows(dn, h, r, gain)
        dx = dh + dx
        return dx, (0.5 * dx).astype(BF16), dx.astype(BF16), dgain

    d = h.shape[1]
    dx, dx_half_b, dx_b, d_norm = _rowk(fn, [dh, dn, h, r], [norm_g],
                                        [("row", d, F32), ("row", d, BF16), ("row", d, BF16), ("sum", d)],
                                        name=f"{tag}_rms_bwd")
    return dx, dx_half_b, dx_b, d_norm, got


def _pool_window(z, seq, width, group_width, *, name, transpose):
    assert POOL_WINDOWS == (2, 4, 8, 16)
    t_dim = z.shape[0]
    tc = _tile(group_width, 256, V7X_LANES)
    per_group = group_width // tc

    def body(z_ref, o_ref):
        gid = pl.program_id(1) // per_group
        v = z_ref[...]
        t = lax.broadcasted_iota(jnp.int32, v.shape, 0)
        win = jnp.where(gid == 0, 2, jnp.where(gid == 1, 4, jnp.where(gid == 2, 8, 16)))
        cnt = jnp.minimum(t + 1, win).astype(F32)
        if transpose:
            e = v / cnt
            shift = lambda q, k: jnp.where(t < seq - k, pltpu.roll(q, seq - k, 0), 0.0)
        else:
            e = v
            shift = lambda q, k: jnp.where(t >= k, pltpu.roll(q, k, 0), 0.0)
        s1 = e + shift(e, 1)
        s2 = s1 + shift(s1, 2)
        s3 = s2 + shift(s2, 4)
        s4 = s3 + shift(s3, 8)
        s = jnp.where(gid == 0, s1, jnp.where(gid == 1, s2, jnp.where(gid == 2, s3, s4)))
        if transpose:
            o_ref[...] = (s - v).astype(o_ref.dtype)
        else:
            o_ref[...] = (s / cnt - v).astype(o_ref.dtype)

    spec = pl.BlockSpec((seq, tc), lambda b, c: (b, c))
    return _pcall(body, name=name, grid=(t_dim // seq, width // tc), in_specs=[spec], out_specs=[spec],
                  out_shape=[jax.ShapeDtypeStruct((t_dim, width), F32 if transpose else BF16)],
                  semantics=("parallel", "parallel"), operands=[z])[0][0]


def _discretize(lam_re, lam_im, log_dt, b_re_t, b_im_t):
    dt = jnp.exp(log_dt)
    mag = jnp.exp(lam_re * dt)
    a_re = mag * jnp.cos(lam_im * dt)
    a_im = mag * jnp.sin(lam_im * dt)
    den = lam_re * lam_re + lam_im * lam_im
    f_re = ((a_re - 1.0) * lam_re + a_im * lam_im) / den
    f_im = (a_im * lam_re - (a_re - 1.0) * lam_im) / den
    f_re, f_im = f_re[:, None, :], f_im[:, None, :]
    return a_re, a_im, f_re * b_re_t - f_im * b_im_t, f_re * b_im_t + f_im * b_re_t


def _discretize_fwd(params):
    shapes = [jax.ShapeDtypeStruct(params[0].shape, F32)] * 2 + [jax.ShapeDtypeStruct(params[3].shape, F32)] * 2

    def body(*refs):
        outs = _discretize(*[r[...] for r in refs[:5]])
        for o_ref, o in zip(refs[5:], outs):
            o_ref[...] = o

    return pl.pallas_call(body, name="s5_discretize", out_shape=shapes,
                          compiler_params=pltpu.CompilerParams(vmem_limit_bytes=V7X_VMEM_LIMIT_BYTES))(*params)


def _discretize_bwd(params, cots):
    shapes = [jax.ShapeDtypeStruct(p.shape, F32) for p in params]

    def body(*refs):
        _, vjp = jax.vjp(_discretize, *[r[...] for r in refs[:5]])
        grads = vjp(tuple(r[...] for r in refs[5:9]))
        for o_ref, o in zip(refs[9:], grads):
            o_ref[...] = o

    return pl.pallas_call(body, name="s5_discretize_bwd", out_shape=shapes,
                          compiler_params=pltpu.CompilerParams(vmem_limit_bytes=V7X_VMEM_LIMIT_BYTES))(*params, *cots)


def _scan_tiles(t_dim, n_dim, seq):
    return _tile(seq, 256, V7X_SUBLANES), _tile(n_dim, 1024, V7X_LANES)


def _scan_fwd(bu_re, bu_im, a_re, a_im, seq, comm=None):
    t_dim, n_dim = bu_re.shape
    tt, tl = _scan_tiles(t_dim, n_dim, seq)
    per_seq = seq // tt

    def body(br_ref, bi_ref, ar_ref, ai_ref, xr_ref, xi_ref, cr_ref, ci_ref):
        @pl.when(pl.program_id(1) % per_seq == 0)
        def _():
            cr_ref[...] = jnp.zeros_like(cr_ref)
            ci_ref[...] = jnp.zeros_like(ci_ref)

        ar, ai = ar_ref[...], ai_ref[...]

        def step(i, carry):
            xr, xi = carry
            row = pl.ds(i, 1)
            nr = ar * xr - ai * xi + br_ref[row, :]
            ni = ai * xr + ar * xi + bi_ref[row, :]
            xr_ref[row, :] = nr
            xi_ref[row, :] = ni
            return nr, ni

        xr, xi = lax.fori_loop(0, tt, step, (cr_ref[...], ci_ref[...]), unroll=8)
        cr_ref[...] = xr
        ci_ref[...] = xi

    blk = pl.BlockSpec((tt, tl), lambda l, t: (t, l))
    vec = pl.BlockSpec((1, tl), lambda l, t: (0, l))
    outs, c_outs = _pcall(
        body, name="s5_scan", grid=(n_dim // tl, t_dim // tt), in_specs=[blk, blk, vec, vec], out_specs=[blk, blk],
        out_shape=[jax.ShapeDtypeStruct((t_dim, n_dim), F32)] * 2, scratch_shapes=[pltpu.VMEM((1, tl), F32)] * 2,
        semantics=("parallel", "arbitrary"), operands=[bu_re, bu_im, a_re, a_im], comm=comm)
    return outs, c_outs


def _scan_bwd(gx_re, gx_im, x_re, x_im, a_re, a_im, seq, comm=None):
    t_dim, n_dim = gx_re.shape
    tt, tl = _scan_tiles(t_dim, n_dim, seq)
    per_seq = seq // tt
    n_t = t_dim // tt
    prev_rows = V7X_SUBLANES

    def body(gr_ref, gi_ref, xr_ref, xi_ref, pr_ref, pi_ref, ar_ref, ai_ref,
             lr_ref, li_ref, dar_ref, dai_ref, cr_ref, ci_ref):
        step_id = pl.program_id(1)
        blk_id = n_t - 1 - step_id

        @pl.when((blk_id + 1) % per_seq == 0)
        def _():
            cr_ref[...] = jnp.zeros_like(cr_ref)
            ci_ref[...] = jnp.zeros_like(ci_ref)

        ar, ai = ar_ref[...], ai_ref[...]

        def step(j, carry):
            lr, li = carry
            row = pl.ds(tt - 1 - j, 1)
            nr = gr_ref[row, :] + ar * lr + ai * li
            ni = gi_ref[row, :] - ai * lr + ar * li
            lr_ref[row, :] = nr
            li_ref[row, :] = ni
            return nr, ni

        lr, li = lax.fori_loop(0, tt, step, (cr_ref[...], ci_ref[...]), unroll=8)
        cr_ref[...] = lr
        ci_ref[...] = li

        first_of_seq = blk_id % per_seq == 0
        keep = jnp.where(first_of_seq, 0.0, 1.0)
        t = lax.broadcasted_iota(jnp.int32, (tt, tl), 0)
        xr_prev = jnp.where(t == 0, pr_ref[prev_rows - 1:prev_rows, :] * keep, pltpu.roll(xr_ref[...], 1, 0))
        xi_prev = jnp.where(t == 0, pi_ref[prev_rows - 1:prev_rows, :] * keep, pltpu.roll(xi_ref[...], 1, 0))
        lam_r, lam_i = lr_ref[...], li_ref[...]
        d_re = _colsum(lam_r * xr_prev + lam_i * xi_prev)
        d_im = _colsum(lam_i * xr_prev - lam_r * xi_prev)

        @pl.when(step_id == 0)
        def _():
            dar_ref[...] = d_re
            dai_ref[...] = d_im

        @pl.when(step_id > 0)
        def _():
            dar_ref[...] += d_re
            dai_ref[...] += d_im

    blk = pl.BlockSpec((tt, tl), lambda l, t: (n_t - 1 - t, l))
    prev = pl.BlockSpec((prev_rows, tl), lambda l, t: (jnp.maximum((n_t - 1 - t) * (tt // prev_rows) - 1, 0), l))
    vec = pl.BlockSpec((1, tl), lambda l, t: (0, l))
    return _pcall(
        body, name="s5_scan_bwd", grid=(n_dim // tl, n_t),
        in_specs=[blk, blk, blk, blk, prev, prev, vec, vec], out_specs=[blk, blk, vec, vec],
        out_shape=[jax.ShapeDtypeStruct((t_dim, n_dim), F32)] * 2 + [jax.ShapeDtypeStruct((1, n_dim), F32)] * 2,
        scratch_shapes=[pltpu.VMEM((1, tl), F32)] * 2, semantics=("parallel", "arbitrary"),
        operands=[gx_re, gx_im, x_re, x_im, x_re, x_im, a_re, a_im], comm=comm)


def _block_diag(w, per_block):
    g, ka, kb = w.shape
    eye = jnp.eye(per_block, dtype=w.dtype)
    out = jnp.einsum("jakb,ac->jakcb", w.reshape(g // per_block, per_block, ka, kb), eye)
    return out.reshape(g // per_block, per_block * ka, per_block * kb)


def _block_diag_t(d, per_block, ka, kb):
    j = d.shape[0]
    eye = jnp.eye(per_block, dtype=d.dtype)
    picked = jnp.einsum("jakcb,ac->jakb", d.reshape(j, per_block, ka, per_block, kb), eye)
    return picked.reshape(j * per_block, ka, kb)


def _gelu_parts(y):
    inner = GELU_C * (y + GELU_A * y * y * y)
    th = jnp.tanh(inner)
    val = 0.5 * y * (1.0 + th)
    grad = 0.5 * (1.0 + th) + 0.5 * y * (1.0 - th * th) * GELU_C * (1.0 + 3.0 * GELU_A * y * y)
    return val, grad


def _chip_sum(grad, got, core, *, name):
    r, c = got.shape[0] // N_CHIP, got.shape[1]
    tr = _tile(r, 704, 16)
    tc = _tile(c, 2048, V7X_LANES)
    per = r // tr

    def body(core_ref, g_ref, s_ref, o_ref):
        o_ref[...] = (g_ref[...].astype(F32) + s_ref[...].astype(F32)).astype(o_ref.dtype)

    slot = pl.BlockSpec((tr, tc), lambda q, i, j, core: (q * per + i, j))
    grid_spec = pltpu.PrefetchScalarGridSpec(
        num_scalar_prefetch=1, grid=(N_CHIP, per, c // tc),
        in_specs=[pl.BlockSpec((tr, tc), lambda q, i, j, core: ((2 * q + core[0]) * per + i, j)), slot],
        out_specs=slot)
    return pl.pallas_call(
        body, name=name, grid_spec=grid_spec, out_shape=jax.ShapeDtypeStruct(got.shape, got.dtype),
        compiler_params=pltpu.CompilerParams(dimension_semantics=("parallel",) * 3,
                                             vmem_limit_bytes=V7X_VMEM_LIMIT_BYTES))(core, grad, got)


def _adamw_math(w, g, m, v):
    nm = ADAM_B1 * m + (1.0 - ADAM_B1) * g
    nv = ADAM_B2 * v + (1.0 - ADAM_B2) * (g * g)
    m_hat = nm * (1.0 / (1.0 - ADAM_B1 ** ADAM_STEP))
    v_hat = nv * (1.0 / (1.0 - ADAM_B2 ** ADAM_STEP))
    return -ADAM_LR * (m_hat / (jnp.sqrt(v_hat) + ADAM_EPS) + ADAM_WD * w), nm, nv


def _sum_adamw(parts, others, chip, w, m, v, *, name):
    r, c = w.shape
    n_others = others.shape[0] // r
    tr = _tile(r, 704, 16)
    tc = _tile(c, 512, V7X_LANES)
    per = r // tr

    def body(chip_ref, p_ref, *refs):
        o_refs, (w_ref, m_ref, v_ref), (g_ref, d_ref, nm_ref, nv_ref) = refs[:n_others], refs[n_others:n_others + 3], refs[n_others + 3:]
        g = p_ref[...].astype(F32)
        for o_ref in o_refs:
            g = g + o_ref[...].astype(F32)
        g_ref[...] = g
        d_ref[...], nm_ref[...], nv_ref[...] = _adamw_math(w_ref[...], g, m_ref[...], v_ref[...])

    own = pl.BlockSpec((tr, tc), lambda i, j, chip: (i, j))
    grid_spec = pltpu.PrefetchScalarGridSpec(
        num_scalar_prefetch=1, grid=(per, c // tc),
        in_specs=[pl.BlockSpec((tr, tc), lambda i, j, chip: (chip[0] * per + i, j))]
        + [pl.BlockSpec((tr, tc), lambda i, j, chip, s=s: (s * per + i, j)) for s in range(n_others)] + [own] * 3,
        out_specs=[own] * 4)
    return pl.pallas_call(
        body, name=name, grid_spec=grid_spec, out_shape=[jax.ShapeDtypeStruct((r, c), F32)] * 4,
        compiler_params=pltpu.CompilerParams(dimension_semantics=("parallel",) * 2,
                                             vmem_limit_bytes=V7X_VMEM_LIMIT_BYTES))(chip, parts, *[others] * n_others, w, m, v)


def _sum_parts(parts, *, name):
    n_parts, r, c = parts.shape
    tr = _tile(r, 704, 16)
    tc = _tile(c, 1024, V7X_LANES)

    def body(p_ref, o_ref):
        acc = p_ref[0].astype(F32)
        for s in range(1, n_parts):
            acc = acc + p_ref[s].astype(F32)
        o_ref[...] = acc

    return _pcall(body, name=name, grid=(r // tr, c // tc),
                  in_specs=[pl.BlockSpec((n_parts, tr, tc), lambda i, j: (0, i, j))],
                  out_specs=[pl.BlockSpec((tr, tc), lambda i, j: (i, j))],
                  out_shape=[jax.ShapeDtypeStruct((r, c), F32)], semantics=("parallel", "parallel"),
                  operands=[parts])[0][0]


def _adamw(w, g, m, v, *, name):
    r, c = w.shape
    tr = _tile(r, max(V7X_SUBLANES, ADAMW_BLOCK_BYTES // (4 * c)), V7X_SUBLANES)

    def body(w_ref, g_ref, m_ref, v_ref, d_ref, nm_ref, nv_ref):
        d_ref[...], nm_ref[...], nv_ref[...] = _adamw_math(w_ref[...], g_ref[...], m_ref[...], v_ref[...])

    spec = pl.BlockSpec((tr, c), lambda i: (i, 0))
    return _pcall(body, name=name, grid=(r // tr,), in_specs=[spec] * 4, out_specs=[spec] * 3,
                  out_shape=[jax.ShapeDtypeStruct((r, c), F32)] * 3, semantics=("parallel",), operands=[w, g, m, v])[0]


def _pack(arrays, width=V7X_LANES):
    tile = V7X_SUBLANES * width
    parts, layout, row = [], [], 0
    for a in arrays:
        n = a.size
        rows = -(-n // tile) * V7X_SUBLANES
        flat = jnp.pad(a.reshape(-1).astype(F32), (0, rows * width - n))
        parts.append(flat.reshape(rows, width))
        layout.append((row, rows, n, a.shape))
        row += rows
    return jnp.concatenate(parts, axis=0), layout


def _unpack(packed, layout):
    return [packed[row:row + rows].reshape(-1)[:n].reshape(shape) for row, rows, n, shape in layout]


def kernel(x, ffn1_norm, ffn1_gate, ffn1_up, ffn1_down, mix_norm, w_in, w_pool, pool_scale, lam_re, lam_im, log_dt, b_re, b_im, c_re, c_im, d_skip, w_glu, b_glu, pool_out_norm, ssm_out_norm, w_out, ffn2_norm, ffn2_gate, ffn2_up, ffn2_down, final_norm, loss_target, m_ffn1_norm, m_ffn1_gate, m_ffn1_up, m_ffn1_down, m_mix_norm, m_w_in, m_w_pool, m_pool_scale, m_lam_re, m_lam_im, m_log_dt, m_b_re, m_b_im, m_c_re, m_c_im, m_d_skip, m_w_glu, m_b_glu, m_pool_out_norm, m_ssm_out_norm, m_w_out, m_ffn2_norm, m_ffn2_gate, m_ffn2_up, m_ffn2_down, m_final_norm, v_ffn1_norm, v_ffn1_gate, v_ffn1_up, v_ffn1_down, v_mix_norm, v_w_in, v_w_pool, v_pool_scale, v_lam_re, v_lam_im, v_log_dt, v_b_re, v_b_im, v_c_re, v_c_im, v_d_skip, v_w_glu, v_b_glu, v_pool_out_norm, v_ssm_out_norm, v_w_out, v_ffn2_norm, v_ffn2_gate, v_ffn2_up, v_ffn2_down, v_final_norm):
    weights = dict(ffn1_norm=ffn1_norm, ffn1_gate=ffn1_gate, ffn1_up=ffn1_up, ffn1_down=ffn1_down, mix_norm=mix_norm, w_in=w_in, w_pool=w_pool, pool_scale=pool_scale, lam_re=lam_re, lam_im=lam_im, log_dt=log_dt, b_re=b_re, b_im=b_im, c_re=c_re, c_im=c_im, d_skip=d_skip, w_glu=w_glu, b_glu=b_glu, pool_out_norm=pool_out_norm, ssm_out_norm=ssm_out_norm, w_out=w_out, ffn2_norm=ffn2_norm, ffn2_gate=ffn2_gate, ffn2_up=ffn2_up, ffn2_down=ffn2_down, final_norm=final_norm)
    moments_m = dict(ffn1_norm=m_ffn1_norm, ffn1_gate=m_ffn1_gate, ffn1_up=m_ffn1_up, ffn1_down=m_ffn1_down, mix_norm=m_mix_norm, w_in=m_w_in, w_pool=m_w_pool, pool_scale=m_pool_scale, lam_re=m_lam_re, lam_im=m_lam_im, log_dt=m_log_dt, b_re=m_b_re, b_im=m_b_im, c_re=m_c_re, c_im=m_c_im, d_skip=m_d_skip, w_glu=m_w_glu, b_glu=m_b_glu, pool_out_norm=m_pool_out_norm, ssm_out_norm=m_ssm_out_norm, w_out=m_w_out, ffn2_norm=m_ffn2_norm, ffn2_gate=m_ffn2_gate, ffn2_up=m_ffn2_up, ffn2_down=m_ffn2_down, final_norm=m_final_norm)
    moments_v = dict(ffn1_norm=v_ffn1_norm, ffn1_gate=v_ffn1_gate, ffn1_up=v_ffn1_up, ffn1_down=v_ffn1_down, mix_norm=v_mix_norm, w_in=v_w_in, w_pool=v_w_pool, pool_scale=v_pool_scale, lam_re=v_lam_re, lam_im=v_lam_im, log_dt=v_log_dt, b_re=v_b_re, b_im=v_b_im, c_re=v_c_re, c_im=v_c_im, d_skip=v_d_skip, w_glu=v_w_glu, b_glu=v_b_glu, pool_out_norm=v_pool_out_norm, ssm_out_norm=v_ssm_out_norm, w_out=v_w_out, ffn2_norm=v_ffn2_norm, ffn2_gate=v_ffn2_gate, ffn2_up=v_ffn2_up, ffn2_down=v_ffn2_down, final_norm=v_final_norm)
    names = list(weights)

    n_seq, seq, d_model = x.shape
    t_dim = n_seq * seq
    ff_shard = ffn1_gate.shape[1]
    ff = N_DEV * ff_shard
    ff_pad = -(-ff // FF_PAD_MULTIPLE) * FF_PAD_MULTIPLE - ff
    n_pool, pool_gw = w_pool.shape[0], w_pool.shape[2]
    pool_w = n_pool * pool_gw
    pool_rows = pool_gw // N_DEV
    n_grp, n_state, n_ch = b_re.shape
    ssm_w = n_grp * n_ch
    grp_per_blk = V7X_LANES // n_ch
    n_blk = n_grp // grp_per_blk
    ch_blk, st_blk = grp_per_blk * n_ch, grp_per_blk * n_state
    ssm_off = pool_w // ch_blk

    x2 = x.reshape(t_dim, d_model)
    tgt2 = loss_target.reshape(t_dim, d_model)
    row = lambda p: p.reshape(1, -1)

    sharded = ["ffn1_gate", "ffn1_up", "ffn1_down", "ffn2_gate", "ffn2_up", "ffn2_down", "w_in", "w_out", "w_glu", "w_pool"]
    transposed = {"ffn1_gate", "ffn1_up", "ffn2_gate", "ffn2_up"}
    shard = {n: (weights[n].T if n in transposed else weights[n]).astype(BF16) for n in sharded}

    def gather(*group):
        return _Gather([shard[n] for n in group], [ff_pad if n.startswith("ffn") else 0 for n in group])

    full = {}
    full["ffn1_gate"], full["ffn1_up"] = _comm_only(gather("ffn1_gate", "ffn1_up"), name="gather_first")

    def ffn1_down_weights(got):
        full["ffn1_down"], full["w_in"] = got
        return full["ffn1_down"]

    h1, saved1, _, got = _ffn_fwd(x2, row(ffn1_norm), full["ffn1_gate"], full["ffn1_up"], ffn1_down_weights, "ffn1",
                                  gather("ffn1_down", "w_in"), gather("ffn2_gate"))
    (full["ffn2_gate"],) = got
    n2, r2 = _rms_fwd(h1, row(mix_norm), name="mix_rms")
    z, (full["w_glu"], full["w_pool"]) = _matmul([(n2, full["w_in"])], [F32], name="mix_in", comm=gather("w_glu", "w_pool"))

    d_pool = _pool_window(z, seq, pool_w, pool_gw, name="pool_window", transpose=False)
    y_pool_lin = _bd([(d_pool, full["w_pool"])], F32, name="pool_mix")

    disc_params = (lam_re, lam_im, log_dt.reshape(n_grp, 1), jnp.swapaxes(b_re, 1, 2), jnp.swapaxes(b_im, 1, 2))
    a_re, a_im, bb_re_t, bb_im_t = _discretize_fwd(disc_params)
    a_re_row, a_im_row = row(a_re), row(a_im)
    wb_re, wb_im = _block_diag(bb_re_t, grp_per_blk), _block_diag(bb_im_t, grp_per_blk)
    wc_re = _block_diag(jnp.swapaxes(c_re, 1, 2), grp_per_blk)
    wc_im = _block_diag(jnp.swapaxes(c_im, 1, 2), grp_per_blk)
    bu_re = _bd([(z, wb_re)], F32, name="s5_bu_re", offs=[ssm_off], precision=HI)
    bu_im = _bd([(z, wb_im)], F32, name="s5_bu_im", offs=[ssm_off], precision=HI)
    (xs_re, xs_im), (full["ffn2_up"],) = _scan_fwd(bu_re, bu_im, a_re_row, a_im_row, seq, comm=gather("ffn2_up"))
    y_lin, (full["w_out"],) = _bd([(xs_re, wc_re), (xs_im, -wc_im)], F32, name="s5_cx", precision=HI,
                                  comm=gather("w_out"))

    def s5_post(y_lin, u, skip):
        y = y_lin + skip * u
        return y, _gelu_parts(y)[0].astype(BF16)

    u_cols = (z, ssm_w, pool_w // ssm_w)
    y_ssm, yg = _rowk(s5_post, [y_lin, u_cols], [row(d_skip)], [("row", ssm_w, F32), ("row", ssm_w, BF16)], name="s5_post")

    def glu(accs, ex, bc):
        q = accs[0] + bc[0]
        return q, ex[0].astype(F32) * jax.nn.sigmoid(q)

    q_glu, y_s5 = _matmul([(yg, full["w_glu"])], [F32, F32], name="s5_glu", extras=[yg], bcast=[row(b_glu)], epilogue=glu)

    def merge(yp_lin, ys, scale, gp, gs):
        yp = yp_lin * scale
        rp = lax.rsqrt(jnp.mean(yp * yp, axis=-1, keepdims=True) + NORM_EPS)
        rs = lax.rsqrt(jnp.mean(ys * ys, axis=-1, keepdims=True) + NORM_EPS)
        merged = jnp.concatenate([yp * rp * gp, ys * rs * gs], axis=-1)
        return merged.astype(BF16), rp, rs

    merged, r_pool, r_ssm = _rowk(merge, [y_pool_lin, y_s5], [row(pool_scale), row(pool_out_norm), row(ssm_out_norm)],
                                  [("row", pool_w + ssm_w, BF16), ("row", 1, F32), ("row", 1, F32)], name="mix_merge")
    h2 = _matmul([(merged, full["w_out"])], [F32], name="mix_out", extras=[h1],
                 epilogue=lambda accs, ex, bc: (ex[0] + accs[0],))

    def ffn2_down_weights(got):
        (full["ffn2_down"],) = got
        return full["ffn2_down"]

    h3, saved2, _, _ = _ffn_fwd(h2, row(ffn2_norm), full["ffn2_gate"], full["ffn2_up"], ffn2_down_weights, "ffn2",
                                gather("ffn2_down"), None)

    def head(h, tgt, gain):
        r = lax.rsqrt(jnp.mean(h * h, axis=-1, keepdims=True) + NORM_EPS)
        xh = h * r
        err = xh * gain - tgt
        loss = jnp.sum(0.5 * jnp.mean(err * err, axis=-1, keepdims=True), axis=0, keepdims=True)
        dout = err * (1.0 / d_model)
        dy = dout * gain
        dh = r * (dy - xh * jnp.mean(dy * xh, axis=-1, keepdims=True))
        return dh, (0.5 * dh).astype(BF16), _colsum(dout * xh), jnp.broadcast_to(loss, (1, V7X_LANES))

    dh3, dyb2, g_final, loss_part = _rowk(head, [h3, tgt2], [row(final_norm)],
                                          [("row", d_model, F32), ("row", d_model, BF16), ("sum", d_model), ("sum", V7X_LANES)],
                                          name="loss_head")

    grads = {}
    shard_rows = {n: (n_pool * pool_rows if n == "w_pool" else shard[n].shape[0]) for n in sharded}

    def to_sibling(group, name):
        got = _comm_only(_SiblingSwap([grads[n] for n in group], [shard_rows[n] for n in group]), name=name)
        for n, g in zip(group, got):
            chip_parts[n] = _chip_sum(grads[n], g, my_core, name=f"chip_sum_{n}")
        return [chip_parts[n] for n in group]

    my_core = lax.axis_index("c").astype(jnp.int32).reshape(1)
    my_chip = (2 * lax.axis_index("x") + lax.axis_index("y")).astype(jnp.int32).reshape(1)
    chip_parts, from_chips = {}, {}

    def to_chips(group, parts):
        return _ChipScatter(parts), group

    def take(group, got):
        for n, g in zip(group, got):
            from_chips[n] = g

    def ffn_backward(tag, dh, dyb, saved, norm_g, first):
        gate, up, down = f"{tag}_gate", f"{tag}_up", f"{tag}_down"
        n, a = saved[1], saved[5]
        plan, group = first
        grads[down], got = _with_comm(_matmul([(a, dyb)], [BF16], name=f"{tag}_dwdown", ta=True, comm=plan), plan)
        take(group, got)
        plan, group = to_chips([down], to_sibling([down], f"grads_to_sibling_{down}"))
        (dg, du), got = _ffn_bwd_act(dyb, saved, full[down], tag, plan)
        take(group, got)
        grads[gate] = _matmul([(dg, n)], [BF16], name=f"{tag}_dwgate", ta=True)
        plan, group = to_chips([gate], to_sibling([gate], f"grads_to_sibling_{gate}"))
        grads[up], got = _matmul([(du, n)], [BF16], name=f"{tag}_dwup", ta=True, comm=plan)
        take(group, got)
        plan, group = to_chips([up], to_sibling([up], f"grads_to_sibling_{up}"))
        dx, _, dx_b, grads[f"{tag}_norm"], got = _ffn_bwd_input(dh, dg, du, saved, norm_g, full[gate], full[up], tag, plan)
        take(group, got)
        return dx, dx_b

    dh2, dh2b = ffn_backward("ffn2", dh3, dyb2, saved2, row(ffn2_norm), (None, []))

    d_merged = _matmul([(dh2b, full["w_out"])], [F32], name="mix_out_dx", tb=True)
    grads["w_out"] = _matmul([(merged, dh2b)], [BF16], name="mix_out_dw", ta=True)

    def merge_bwd(dm, yp_lin, ys, rp, rs, scale, gp, gs):
        yp = yp_lin * scale
        d_yp, d_gp = _rms_bwd_rows(dm[:, :pool_w], yp, rp, gp)
        d_ys, d_gs = _rms_bwd_rows(dm[:, pool_w:], ys, rs, gs)
        return (d_yp * scale).astype(BF16), d_ys, _colsum(d_yp * yp_lin), d_gp, d_gs

    d_pool_lin, d_ys, grads["pool_scale"], grads["pool_out_norm"], grads["ssm_out_norm"] = _rowk(
        merge_bwd, [d_merged, y_pool_lin, y_s5, r_pool, r_ssm], [row(pool_scale), row(pool_out_norm), row(ssm_out_norm)],
        [("row", pool_w, BF16), ("row", ssm_w, F32), ("sum", pool_w), ("sum", pool_w), ("sum", ssm_w)], name="mix_merge_bwd")

    w_pool_t = jnp.swapaxes(full["w_pool"], 1, 2)
    dd_pool = _bd([(d_pool_lin, w_pool_t)], F32, name="pool_mix_dx")
    g_pool = _bd_wgrad(d_pool, d_pool_lin, pool_gw, pool_gw, n_pool, name="pool_mix_dw")
    grads["w_pool"] = g_pool.reshape(n_pool, N_DEV, pool_rows, pool_gw).transpose(1, 0, 2, 3).reshape(
        N_DEV * n_pool * pool_rows, pool_gw).astype(BF16)
    dz_pool = _pool_window(dd_pool, seq, pool_w, pool_gw, name="pool_window_bwd", transpose=True)

    def glu_bwd(d_ys, yg, q):
        sg = jax.nn.sigmoid(q)
        dq = d_ys * yg.astype(F32) * sg * (1.0 - sg)
        return dq.astype(BF16), d_ys * sg, _colsum(dq)

    dq, d_yg_direct, grads["b_glu"] = _rowk(glu_bwd, [d_ys, yg, q_glu], [],
                                            [("row", ssm_w, BF16), ("row", ssm_w, F32), ("sum", ssm_w)], name="s5_glu_bwd")
    d_yg_mm = _matmul([(dq, full["w_glu"])], [F32], name="s5_glu_dx", tb=True)
    grads["w_glu"] = _matmul([(yg, dq)], [BF16], name="s5_glu_dw", ta=True)

    def gelu_bwd(d1, d2, y, u, skip):
        dy = (d1 + d2) * _gelu_parts(y)[1]
        return dy, dy * skip, _colsum(dy * u)

    dy_ssm, du_skip, grads["d_skip"] = _rowk(gelu_bwd, [d_yg_direct, d_yg_mm, y_ssm, u_cols], [row(d_skip)],
                                             [("row", ssm_w, F32), ("row", ssm_w, F32), ("sum", ssm_w)], name="s5_gelu_bwd")
    wc_re_t, wc_im_t = jnp.swapaxes(wc_re, 1, 2), jnp.swapaxes(wc_im, 1, 2)
    gx_re = _bd([(dy_ssm, wc_re_t)], F32, name="s5_gx_re", precision=HI)
    gx_im = _bd([(dy_ssm, -wc_im_t)], F32, name="s5_gx_im", precision=HI)
    d_wc_re = _bd_wgrad(xs_re, dy_ssm, st_blk, ch_blk, n_blk, name="s5_dc_re", precision=HI)
    d_wc_im = _bd_wgrad(xs_im, dy_ssm, st_blk, ch_blk, n_blk, name="s5_dc_im", precision=HI, sign=-1.0)
    (lm_re, lm_im, da_re, da_im), _ = _scan_bwd(gx_re, gx_im, xs_re, xs_im, a_re_row, a_im_row, seq)
    d_wb_re = _bd_wgrad(z, lm_re, ch_blk, st_blk, n_blk, name="s5_db_re", off_a=ssm_off, precision=HI)
    d_wb_im = _bd_wgrad(z, lm_im, ch_blk, st_blk, n_blk, name="s5_db_im", off_a=ssm_off, precision=HI)
    wb_re_t, wb_im_t = jnp.swapaxes(wb_re, 1, 2), jnp.swapaxes(wb_im, 1, 2)
    du_lin = _bd([(lm_re, wb_re_t), (lm_im, wb_im_t)], F32, name="s5_du", precision=HI)
    grads["c_re"] = jnp.swapaxes(_block_diag_t(d_wc_re, grp_per_blk, n_state, n_ch), 1, 2)
    grads["c_im"] = jnp.swapaxes(_block_diag_t(d_wc_im, grp_per_blk, n_state, n_ch), 1, 2)
    d_bb_re_t = _block_diag_t(d_wb_re, grp_per_blk, n_ch, n_state)
    d_bb_im_t = _block_diag_t(d_wb_im, grp_per_blk, n_ch, n_state)
    g_lam_re, g_lam_im, g_log_dt, g_b_re_t, g_b_im_t = _discretize_bwd(
        disc_params, (da_re.reshape(n_grp, n_state), da_im.reshape(n_grp, n_state), d_bb_re_t, d_bb_im_t))
    grads["lam_re"], grads["lam_im"], grads["log_dt"] = g_lam_re, g_lam_im, g_log_dt.reshape(n_grp)
    grads["b_re"], grads["b_im"] = jnp.swapaxes(g_b_re_t, 1, 2), jnp.swapaxes(g_b_im_t, 1, 2)

    def join(dzp, du1, du2):
        return (jnp.concatenate([dzp, du1 + du2], axis=-1).astype(BF16),)

    (dz,) = _rowk(join, [dz_pool, du_lin, du_skip], [], [("row", pool_w + ssm_w, BF16)], name="mix_in_join")
    dn2 = _matmul([(dz, full["w_in"])], [F32], name="mix_in_dx", tb=True)
    grads["w_in"] = _matmul([(n2, dz)], [BF16], name="mix_in_dw", ta=True)
    mixer = ["w_out", "w_glu", "w_pool", "w_in"]
    parts_mixer = to_sibling(mixer, "grads_to_sibling_mixer")

    def mix_rms_bwd(dh, dn, h, r, gain):
        dx, dgain = _rms_bwd_rows(dn, h, r, gain)
        dx = dh + dx
        return dx, (0.5 * dx).astype(BF16), dgain

    dh1, dyb1, grads["mix_norm"] = _rowk(mix_rms_bwd, [dh2, dn2, h1, r2], [row(mix_norm)],
                                         [("row", d_model, F32), ("row", d_model, BF16), ("sum", d_model)], name="mix_rms_bwd")

    dx, _ = ffn_backward("ffn1", dh1, dyb1, saved1, row(ffn1_norm), to_chips(mixer, parts_mixer))
    grads["final_norm"] = g_final

    summed = {}

    small = [n for n in names if n not in sharded]
    packed_g, layout = _pack([grads[n].reshape(weights[n].shape) for n in small] + [loss_part])
    (gathered,) = _comm_only(_Broadcast([packed_g]), name="small_all_gather")
    small_sum = _sum_parts(gathered, name="sum_small")
    unpacked = _unpack(small_sum, layout)
    for n, g in zip(small, unpacked[:-1]):
        summed[n] = g
    loss = unpacked[-1][0, 0]

    packed_w, _ = _pack([weights[n] for n in small] + [jnp.zeros_like(loss_part)])
    packed_m, _ = _pack([moments_m[n] for n in small] + [jnp.zeros_like(loss_part)])
    packed_v, _ = _pack([moments_v[n] for n in small] + [jnp.ones_like(loss_part)])
    small_out = [_unpack(o, layout) for o in _adamw(packed_w, small_sum, packed_m, packed_v, name="adamw_small")]
    delta, new_m, new_v = {}, {}, {}
    for i, n in enumerate(small):
        delta[n], new_m[n], new_v[n] = small_out[0][i], small_out[1][i], small_out[2][i]
    for n in sharded:
        shape = weights[n].shape
        if n in transposed:
            to2d, back = (lambda a: a.T), (lambda a: a.T)
        else:
            to2d, back = (lambda a: a.reshape(-1, shape[-1])), (lambda a: a.reshape(shape))
        g, d, nm, nv = _sum_adamw(chip_parts[n], from_chips[n], my_chip, to2d(weights[n]), to2d(moments_m[n]),
                                  to2d(moments_v[n]), name=f"adamw_{n}")
        summed[n], delta[n], new_m[n], new_v[n] = back(g), back(d), back(nm), back(nv)

    return (loss, dx.reshape(x.shape), *[summed[n] for n in names], *[delta[n] for n in names],
            *[new_m[n] for n in names], *[new_v[n] for n in names])
```

```python
import functools
import math

import jax
import jax.numpy as jnp
from jax import lax
from jax.experimental import pallas as pl
from jax.experimental.pallas import tpu as pltpu

F32 = jnp.float32
BF16 = jnp.bfloat16
MESH = pl.DeviceIdType.MESH
ANY = pl.BlockSpec(memory_space=pl.ANY)

N_DEV = 8
N_CHIP = 4
V7X_LANES = 128
V7X_SUBLANES = 8
V7X_VMEM_LIMIT_BYTES = 56 * 1024 * 1024
FF_PAD_MULTIPLE = 1024
ADAMW_BLOCK_BYTES = 1 << 20

NORM_EPS = 1e-6
POOL_WINDOWS = (2, 4, 8, 16)
ADAM_LR, ADAM_B1, ADAM_B2, ADAM_EPS, ADAM_WD, ADAM_STEP = 0.001, 0.9, 0.999, 1e-08, 0.01, 10
GELU_C = math.sqrt(2.0 / math.pi)
GELU_A = 0.044715


def _tile(dim, pref, mult):
    t = min(pref, dim)
    t -= t % mult
    while t >= mult:
        if dim % t == 0:
            return t
        t -= mult
    return dim


def _pcall(body, *, name, grid, in_specs, out_specs, out_shape, operands, scratch_shapes=(), semantics=None, comm=None):
    if comm is None:
        params = pltpu.CompilerParams(dimension_semantics=semantics, vmem_limit_bytes=V7X_VMEM_LIMIT_BYTES)
        outs = pl.pallas_call(body, name=name, grid=grid, in_specs=list(in_specs), out_specs=list(out_specs),
                              out_shape=list(out_shape), scratch_shapes=list(scratch_shapes),
                              compiler_params=params)(*operands)
        return list(outs), []
    sizes = [len(in_specs), len(comm.inputs), len(out_shape), len(comm.out_shape), len(scratch_shapes),
             len(comm.sem_shapes)]

    def wrapped(*refs):
        groups, pos = [], 0
        for n in sizes:
            groups.append(refs[pos:pos + n])
            pos += n
        ins, c_ins, outs, c_outs, scratch, sems = groups
        if not grid:
            comm.start(c_ins, c_outs, sems)
            if body is not None:
                body(*ins, *outs, *scratch)
            comm.finish(c_ins, c_outs, sems)
            return
        ids = [pl.program_id(a) for a in range(len(grid))]
        first = functools.reduce(jnp.logical_and, [i == 0 for i in ids])
        last = functools.reduce(jnp.logical_and, [i == g - 1 for i, g in zip(ids, grid)])

        @pl.when(first)
        def _():
            comm.start(c_ins, c_outs, sems)

        body(*ins, *outs, *scratch)

        @pl.when(last)
        def _():
            comm.finish(c_ins, c_outs, sems)

    params = pltpu.CompilerParams(dimension_semantics=("arbitrary",) * len(grid), vmem_limit_bytes=V7X_VMEM_LIMIT_BYTES)
    res = pl.pallas_call(wrapped, name=name, grid=grid, in_specs=list(in_specs) + [ANY] * sizes[1],
                         out_specs=list(out_specs) + [ANY] * sizes[3], out_shape=list(out_shape) + list(comm.out_shape),
                         scratch_shapes=list(scratch_shapes) + list(comm.sem_shapes),
                         compiler_params=params)(*operands, *comm.inputs)
    return list(res[:sizes[2]]), list(res[sizes[2]:])


def _my_place():
    return lax.axis_index("x"), lax.axis_index("y"), lax.axis_index("c")


def _block_index(px, py, pc):
    return 4 * px + 2 * py + pc


class _Gather:
    def __init__(self, shards, pads):
        self.n, self.pads = len(shards), list(pads)
        zero_blocks = [jnp.zeros((p, s.shape[1]), s.dtype) for s, p in zip(shards, pads) if p]
        self.inputs = list(shards) + zero_blocks

        def full_shape(s, pad):
            if s.ndim == 2:
                return (N_DEV * s.shape[0] + pad, s.shape[1])
            return (s.shape[0], N_DEV * s.shape[1], s.shape[2])

        self.out_shape = [jax.ShapeDtypeStruct(full_shape(s, p), s.dtype) for s, p in zip(shards, pads)]
        self.sem_shapes = [pltpu.SemaphoreType.DMA((7 * self.n,)), pltpu.SemaphoreType.DMA((7 * self.n,)),
                           pltpu.SemaphoreType.DMA((self.n + len(zero_blocks),))]

    def _copies(self, ins, outs, sems):
        n = self.n
        send_sems, recv_sems, local_sems = sems
        x, y, c = _my_place()
        me, sibling = (x, y, c), (x, y, 1 - c)
        chips = [(1 - x, y), (x, 1 - y), (1 - x, 1 - y)]

        def rows(i, place):
            idx = _block_index(*place)
            r = ins[i].shape[-2]
            if ins[i].ndim == 2:
                return outs[i].at[pl.ds(idx * r, r), :]
            return outs[i].at[:, pl.ds(idx * r, r), :]

        def copy(i, k, block, to, src=None):
            return pltpu.make_async_remote_copy(
                src_ref=rows(i, block) if src is None else src, dst_ref=rows(i, block),
                send_sem=send_sems.at[7 * i + k], recv_sem=recv_sems.at[7 * i + k],
                device_id=to, device_id_type=MESH)

        local = [pltpu.make_async_copy(ins[i], rows(i, me), local_sems.at[i]) for i in range(n)]
        zi = 0
        for i in range(n):
            if self.pads[i]:
                start = N_DEV * ins[i].shape[0]
                local.append(pltpu.make_async_copy(ins[n + zi], outs[i].at[pl.ds(start, self.pads[i]), :],
                                                   local_sems.at[n + zi]))
                zi += 1
        first = []
        for i in range(n):
            first.append(copy(i, 0, me, sibling, src=ins[i]))
            first += [copy(i, 1 + j, me, (*chip, c), src=ins[i]) for j, chip in enumerate(chips)]
        return local, first, copy, chips, me, sibling, c

    def start(self, ins, outs, sems):
        local, first, *_ = self._copies(ins, outs, sems)
        for cp in local + first:
            cp.start()

    def finish(self, ins, outs, sems):
        local, first, copy, chips, me, sibling, c = self._copies(ins, outs, sems)
        passed = []
        for j, chip in enumerate(chips):
            for i in range(self.n):
                copy(i, 1 + j, (*chip, c), me).wait_recv()
                fwd = copy(i, 4 + j, (*chip, c), sibling)
                fwd.start()
                passed.append(fwd)
        for i in range(self.n):
            copy(i, 0, sibling, me).wait_recv()
        for j, chip in enumerate(chips):
            for i in range(self.n):
                copy(i, 4 + j, (*chip, 1 - c), me).wait_recv()
        for cp in first + passed:
            cp.wait_send()
        for cp in local:
            cp.wait()


class _SiblingSwap:
    def __init__(self, grads, rows):
        self.n, self.rows = len(grads), list(rows)
        self.inputs = list(grads)
        self.out_shape = [jax.ShapeDtypeStruct((N_CHIP * r, g.shape[1]), g.dtype) for g, r in zip(grads, rows)]
        self.sem_shapes = [pltpu.SemaphoreType.DMA((N_CHIP * self.n,))] * 2

    def _copies(self, ins, outs, sems):
        send_sems, recv_sems = sems
        x, y, c = _my_place()
        copies = []
        for i in range(self.n):
            r = self.rows[i]
            for q in range(N_CHIP):
                k = N_CHIP * i + q
                copies.append(pltpu.make_async_remote_copy(
                    src_ref=ins[i].at[pl.ds((2 * q + 1 - c) * r, r), :], dst_ref=outs[i].at[pl.ds(q * r, r), :],
                    send_sem=send_sems.at[k], recv_sem=recv_sems.at[k], device_id=(x, y, 1 - c), device_id_type=MESH))
        return copies

    def start(self, ins, outs, sems):
        for cp in self._copies(ins, outs, sems):
            cp.start()

    def finish(self, ins, outs, sems):
        copies = self._copies(ins, outs, sems)
        for cp in copies:
            cp.wait_recv()
        for cp in copies:
            cp.wait_send()


class _ChipScatter:
    FLIPS = [(0, 1), (1, 0), (1, 1)]

    def __init__(self, parts):
        self.n = len(parts)
        self.inputs = list(parts)
        self.out_shape = [jax.ShapeDtypeStruct((len(self.FLIPS) * (p.shape[0] // N_CHIP), p.shape[1]), p.dtype)
                          for p in parts]
        self.sem_shapes = [pltpu.SemaphoreType.DMA((3 * self.n,))] * 2

    def _copies(self, ins, outs, sems):
        send_sems, recv_sems = sems
        x, y, c = _my_place()
        copies = []
        for i in range(self.n):
            r = ins[i].shape[0] // N_CHIP
            for k, (fx, fy) in enumerate(self.FLIPS):
                px, py = x ^ fx, y ^ fy
                copies.append(pltpu.make_async_remote_copy(
                    src_ref=ins[i].at[pl.ds((2 * px + py) * r, r), :], dst_ref=outs[i].at[pl.ds(k * r, r), :],
                    send_sem=send_sems.at[3 * i + k], recv_sem=recv_sems.at[3 * i + k],
                    device_id=(px, py, c), device_id_type=MESH))
        return copies

    def start(self, ins, outs, sems):
        for cp in self._copies(ins, outs, sems):
            cp.start()

    def finish(self, ins, outs, sems):
        copies = self._copies(ins, outs, sems)
        for cp in copies:
            cp.wait_recv()
        for cp in copies:
            cp.wait_send()


_FLIPS = [(0, 0, 1), (0, 1, 0), (0, 1, 1), (1, 0, 0), (1, 0, 1), (1, 1, 0), (1, 1, 1)]


class _Broadcast:
    def __init__(self, arrays):
        self.n = len(arrays)
        self.inputs = list(arrays)
        self.out_shape = [jax.ShapeDtypeStruct((N_DEV, *a.shape), a.dtype) for a in arrays]
        self.sem_shapes = [pltpu.SemaphoreType.DMA((7 * self.n,)), pltpu.SemaphoreType.DMA((7 * self.n,)),
                           pltpu.SemaphoreType.DMA((self.n,))]

    def _copies(self, ins, outs, sems):
        send_sems, recv_sems, local_sems = sems
        x, y, c = _my_place()
        my_idx = _block_index(x, y, c)
        local, sends, recvs = [], [], []
        for i in range(self.n):
            local.append(pltpu.make_async_copy(ins[i], outs[i].at[my_idx], local_sems.at[i]))
            for k, (fx, fy, fc) in enumerate(_FLIPS):
                peer = (x ^ fx, y ^ fy, c ^ fc)
                common = dict(send_sem=send_sems.at[7 * i + k], recv_sem=recv_sems.at[7 * i + k],
                              device_id=peer, device_id_type=MESH)
                sends.append(pltpu.make_async_remote_copy(src_ref=ins[i], dst_ref=outs[i].at[my_idx], **common))
                recvs.append(pltpu.make_async_remote_copy(src_ref=ins[i], dst_ref=outs[i].at[_block_index(*peer)], **common))
        return local, sends, recvs

    def start(self, ins, outs, sems):
        local, sends, _ = self._copies(ins, outs, sems)
        for cp in local + sends:
            cp.start()

    def finish(self, ins, outs, sems):
        local, sends, recvs = self._copies(ins, outs, sems)
        for cp in recvs:
            cp.wait_recv()
        for cp in sends:
            cp.wait_send()
        for cp in local:
            cp.wait()


class _Both:
    def __init__(self, first, second):
        self.plans = (first, second)
        self.inputs = first.inputs + second.inputs
        self.out_shape = first.out_shape + second.out_shape
        self.sem_shapes = first.sem_shapes + second.sem_shapes

    def _split(self, ins, outs, sems):
        a = self.plans[0]
        n_in, n_out, n_sem = len(a.inputs), len(a.out_shape), len(a.sem_shapes)
        return [(ins[:n_in], outs[:n_out], sems[:n_sem]), (ins[n_in:], outs[n_out:], sems[n_sem:])]

    def start(self, ins, outs, sems):
        for plan, args in zip(self.plans, self._split(ins, outs, sems)):
            plan.start(*args)

    def finish(self, ins, outs, sems):
        for plan, args in zip(self.plans, self._split(ins, outs, sems)):
            plan.finish(*args)


def _comm_only(comm, *, name):
    return _pcall(None, name=name, grid=(), in_specs=[], out_specs=[], out_shape=[], operands=[], comm=comm)[1]


def _matmul(pairs, out_dtypes, *, name, ta=False, tb=False, separate=False, epilogue=None,
            extras=(), bcast=(), tm=512, tn=512, tk=4096, precision=None, comm=None):
    a0, b0 = pairs[0]
    m_dim, k_dim = (a0.shape[1], a0.shape[0]) if ta else a0.shape
    n_dim = b0.shape[0] if tb else b0.shape[1]
    tm = _tile(m_dim, tm, V7X_LANES if ta else 16)
    tn = _tile(n_dim, tn, V7X_LANES)
    tk = _tile(k_dim, tk, V7X_LANES)
    nk = k_dim // tk
    n_acc = len(pairs) if separate else 1
    n_ex, n_bc, n_out = len(extras), len(bcast), len(out_dtypes)
    dims = (((0 if ta else 1,), (1 if tb else 0,)), ((), ()))
    if epilogue is None:
        epilogue = lambda accs, ex, bc: tuple(accs)
    a_spec = pl.BlockSpec((tk, tm), lambda m, n, k: (k, m)) if ta else pl.BlockSpec((tm, tk), lambda m, n, k: (m, k))
    b_spec = pl.BlockSpec((tn, tk), lambda m, n, k: (n, k)) if tb else pl.BlockSpec((tk, tn), lambda m, n, k: (k, n))
    operands, operand_specs, pair_slots = [], [], []
    for pair in pairs:
        slots = []
        for arr, spec in zip(pair, (a_spec, b_spec)):
            found = [i for i, o in enumerate(operands) if o is arr]
            if not found:
                operands.append(arr)
                operand_specs.append(spec)
                found = [len(operands) - 1]
            slots.append(found[0])
        pair_slots.append(slots)
    n_ops = len(operands)

    def body(*refs):
        a_refs = [refs[sa] for sa, _ in pair_slots]
        b_refs = [refs[sb] for _, sb in pair_slots]
        ex_refs = refs[n_ops:n_ops + n_ex]
        bc_refs = refs[n_ops + n_ex:n_ops + n_ex + n_bc]
        out_refs = refs[n_ops + n_ex + n_bc:n_ops + n_ex + n_bc + n_out]
        acc_refs = refs[n_ops + n_ex + n_bc + n_out:]
        parts = [lax.dot_general(a[...], b[...], dims, preferred_element_type=F32, precision=precision)
                 for a, b in zip(a_refs, b_refs)]
        if not separate:
            parts = [functools.reduce(lambda p, q: p + q, parts)]

        def finish(accs):
            outs = epilogue(accs, [e[...] for e in ex_refs], [c[...] for c in bc_refs])
            for o_ref, o in zip(out_refs, outs):
                o_ref[...] = o.astype(o_ref.dtype)

        if nk == 1:
            finish(parts)
        else:
            k = pl.program_id(2)

            @pl.when(k == 0)
            def _():
                for acc, p in zip(acc_refs, parts):
                    acc[...] = p

            @pl.when(k > 0)
            def _():
                for acc, p in zip(acc_refs, parts):
                    acc[...] += p

            @pl.when(k == nk - 1)
            def _():
                finish([acc[...] for acc in acc_refs])

    mn_spec = pl.BlockSpec((tm, tn), lambda m, n, k: (m, n))
    bc_spec = pl.BlockSpec((1, tn), lambda m, n, k: (0, n))
    outs, c_outs = _pcall(
        body, name=name, grid=(m_dim // tm, n_dim // tn, nk),
        in_specs=operand_specs + [mn_spec] * n_ex + [bc_spec] * n_bc,
        out_specs=[mn_spec] * n_out,
        out_shape=[jax.ShapeDtypeStruct((m_dim, n_dim), d) for d in out_dtypes],
        scratch_shapes=[pltpu.VMEM((tm, tn), F32)] * (n_acc if nk > 1 else 0),
        semantics=("parallel", "parallel", "arbitrary"),
        operands=[*operands, *extras, *bcast], comm=comm)
    outs = outs[0] if n_out == 1 else outs
    return outs if comm is None else (outs, c_outs)


def _with_comm(result, comm):
    return (result, []) if comm is None else result


def _bd(pairs, out_dtype, *, name, offs=None, tm=512, operand_dtype=None, comm=None):
    n_blocks, ka, kb = pairs[0][1].shape
    t_dim = pairs[0][0].shape[0]
    tm = _tile(t_dim, tm, 16)
    offs = offs or [0] * len(pairs)
    n_pairs = len(pairs)

    def body(*refs):
        acc = None
        for i in range(n_pairs):
            a, w = refs[2 * i][...], refs[2 * i + 1][...]
            if operand_dtype is not None:
                a, w = a.astype(operand_dtype), w.astype(operand_dtype)
            p = jnp.dot(a, w, preferred_element_type=F32)
            acc = p if acc is None else acc + p
        refs[2 * n_pairs][...] = acc.astype(out_dtype)

    in_specs = []
    for off in offs:
        in_specs.append(pl.BlockSpec((tm, ka), lambda j, t, off=off: (t, j + off)))
        in_specs.append(pl.BlockSpec((None, ka, kb), lambda j, t: (j, 0, 0)))
    outs, c_outs = _pcall(
        body, name=name, grid=(n_blocks, t_dim // tm), in_specs=in_specs,
        out_specs=[pl.BlockSpec((tm, kb), lambda j, t: (t, j))],
        out_shape=[jax.ShapeDtypeStruct((t_dim, n_blocks * kb), out_dtype)],
        semantics=("parallel", "parallel"), operands=[t for p in pairs for t in p], comm=comm)
    return outs[0] if comm is None else (outs[0], c_outs)


def _bd_wgrad(a, b, ka, kb, n_blocks, *, name, off_a=0, off_b=0, tm=512, operand_dtype=None, sign=1.0):
    t_dim = a.shape[0]
    tm = _tile(t_dim, tm, 16)

    def body(a_ref, b_ref, o_ref):
        a, b = a_ref[...], b_ref[...]
        if operand_dtype is not None:
            a, b = a.astype(operand_dtype), b.astype(operand_dtype)
        p = lax.dot_general(a, b, (((0,), (0,)), ((), ())), preferred_element_type=F32)
        if sign != 1.0:
            p = p * sign

        @pl.when(pl.program_id(1) == 0)
        def _():
            o_ref[...] = p

        @pl.when(pl.program_id(1) > 0)
        def _():
            o_ref[...] += p

    return _pcall(
        body, name=name, grid=(n_blocks, t_dim // tm),
        in_specs=[pl.BlockSpec((tm, ka), lambda j, t: (t, j + off_a)),
                  pl.BlockSpec((tm, kb), lambda j, t: (t, j + off_b))],
        out_specs=[pl.BlockSpec((None, ka, kb), lambda j, t: (j, 0, 0))],
        out_shape=[jax.ShapeDtypeStruct((n_blocks, ka, kb), F32)],
        semantics=("parallel", "arbitrary"), operands=[a, b])[0][0]


def _rowk(fn, rows, bcast, outs, *, name, tm=256):
    rows = [r if isinstance(r, tuple) else (r, r.shape[1], 0) for r in rows]
    t_dim = rows[0][0].shape[0]
    tm = _tile(t_dim, tm, 16)
    n_rows, n_bc = len(rows), len(bcast)

    def body(*refs):
        ins = [r[...] for r in refs[:n_rows + n_bc]]
        vals = fn(*ins)
        first = pl.program_id(0) == 0
        for o_ref, v, spec in zip(refs[n_rows + n_bc:], vals, outs):
            if spec[0] == "row":
                o_ref[...] = v.astype(o_ref.dtype)
            else:
                @pl.when(first)
                def _(o_ref=o_ref, v=v):
                    o_ref[...] = v

                @pl.when(jnp.logical_not(first))
                def _(o_ref=o_ref, v=v):
                    o_ref[...] += v

    in_specs = [pl.BlockSpec((tm, w), lambda i, cb=cb: (i, cb)) for _, w, cb in rows]
    in_specs += [pl.BlockSpec((1, b.shape[1]), lambda i: (0, 0)) for b in bcast]
    out_specs, out_shape = [], []
    for spec in outs:
        if spec[0] == "row":
            out_specs.append(pl.BlockSpec((tm, spec[1]), lambda i: (i, 0)))
            out_shape.append(jax.ShapeDtypeStruct((t_dim, spec[1]), spec[2]))
        else:
            out_specs.append(pl.BlockSpec((1, spec[1]), lambda i: (0, 0)))
            out_shape.append(jax.ShapeDtypeStruct((1, spec[1]), F32))
    return _pcall(body, name=name, grid=(t_dim // tm,), in_specs=in_specs, out_specs=out_specs, out_shape=out_shape,
                  semantics=("arbitrary",), operands=[*[r[0] for r in rows], *bcast])[0]


def _colsum(v):
    return jnp.sum(v, axis=0, keepdims=True)


def _rms_fwd(x, g, *, name):
    def fn(x, g):
        r = lax.rsqrt(jnp.mean(x * x, axis=-1, keepdims=True) + NORM_EPS)
        return (x * r * g).astype(BF16), r
    d = x.shape[1]
    return _rowk(fn, [x], [g], [("row", d, BF16), ("row", 1, F32)], name=name)


def _rms_bwd_rows(dn, x, r, g):
    xh = x * r
    dy = dn * g
    dx = r * (dy - xh * jnp.mean(dy * xh, axis=-1, keepdims=True))
    return dx, _colsum(dn * xh)


def _silu_parts(g):
    sg = jax.nn.sigmoid(g)
    return g * sg, sg * (1.0 + g * (1.0 - sg))


def _ffn_fwd(h, norm_g, w_gate_t, w_up_t, get_w_down, tag, comm_gateup, comm_down):
    n, r = _rms_fwd(h, norm_g, name=f"{tag}_rms")

    def gate_up(accs, ex, bc):
        g, u = accs
        return g, u, _silu_parts(g)[0] * u

    (g, u, a), got1 = _matmul([(n, w_gate_t), (n, w_up_t)], [BF16, BF16, BF16], name=f"{tag}_gateup", tb=True,
                              separate=True, epilogue=gate_up, comm=comm_gateup)
    w_down = get_w_down(got1)
    out, got2 = _with_comm(_matmul([(a, w_down)], [F32], name=f"{tag}_down", extras=[h], tm=1024, tn=1024, tk=1024,
                                   epilogue=lambda accs, ex, bc: (ex[0] + 0.5 * accs[0],), comm=comm_down), comm_down)
    return out, (h, n, r, g, u, a), got1, got2


def _ffn_bwd_act(dyb, saved, w_down, tag, comm):
    g, u = saved[3], saved[4]

    def act_bwd(accs, ex, bc):
        da, g, u = accs[0], ex[0].astype(F32), ex[1].astype(F32)
        silu, dsilu = _silu_parts(g)
        return da * u * dsilu, da * silu

    return _matmul([(dyb, w_down)], [BF16, BF16], name=f"{tag}_dact", tb=True, extras=[g, u], epilogue=act_bwd, comm=comm)


def _ffn_bwd_input(dh, dg, du, saved, norm_g, w_gate_t, w_up_t, tag, comm):
    h, n, r = saved[0], saved[1], saved[2]
    dn, got = _with_comm(_matmul([(dg, w_gate_t), (du, w_up_t)], [F32], name=f"{tag}_dn", tm=1024, tn=1024, tk=1024,
                                 comm=comm), comm)

    def fn(dh, dn, h, r, gain):
        dx, dgain = _rms_bwd_rows(dn, h, r, gain)
        dx = dh + dx
        return dx, (0.5 * dx).astype(BF16), dx.astype(BF16), dgain

    d = h.shape[1]
    dx, dx_half_b, dx_b, d_norm = _rowk(fn, [dh, dn, h, r], [norm_g],
                                        [("row", d, F32), ("row", d, BF16), ("row", d, BF16), ("sum", d)],
                                        name=f"{tag}_rms_bwd")
    return dx, dx_half_b, dx_b, d_norm, got


def _pool_window(z, seq, width, group_width, *, name, transpose):
    assert POOL_WINDOWS == (2, 4, 8, 16)
    t_dim = z.shape[0]
    tc = _tile(group_width, 256, V7X_LANES)
    per_group = group_width // tc

    def body(z_ref, o_ref):
        gid = pl.program_id(1) // per_group
        v = z_ref[...]
        t = lax.broadcasted_iota(jnp.int32, v.shape, 0)
        win = jnp.where(gid == 0, 2, jnp.where(gid == 1, 4, jnp.where(gid == 2, 8, 16)))
        cnt = jnp.minimum(t + 1, win).astype(F32)
        if transpose:
            e = v / cnt
            shift = lambda q, k: jnp.where(t < seq - k, pltpu.roll(q, seq - k, 0), 0.0)
        else:
            e = v
            shift = lambda q, k: jnp.where(t >= k, pltpu.roll(q, k, 0), 0.0)
        s1 = e + shift(e, 1)
        s2 = s1 + shift(s1, 2)
        s3 = s2 + shift(s2, 4)
        s4 = s3 + shift(s3, 8)
        s = jnp.where(gid == 0, s1, jnp.where(gid == 1, s2, jnp.where(gid == 2, s3, s4)))
        if transpose:
            o_ref[...] = (s - v).astype(o_ref.dtype)
        else:
            o_ref[...] = (s / cnt - v).astype(o_ref.dtype)

    spec = pl.BlockSpec((seq, tc), lambda b, c: (b, c))
    return _pcall(body, name=name, grid=(t_dim // seq, width // tc), in_specs=[spec], out_specs=[spec],
                  out_shape=[jax.ShapeDtypeStruct((t_dim, width), F32 if transpose else BF16)],
                  semantics=("parallel", "parallel"), operands=[z])[0][0]


def _discretize(lam_re, lam_im, log_dt, b_re_t, b_im_t):
    dt = jnp.exp(log_dt)
    mag = jnp.exp(lam_re * dt)
    a_re = mag * jnp.cos(lam_im * dt)
    a_im = mag * jnp.sin(lam_im * dt)
    den = lam_re * lam_re + lam_im * lam_im
    f_re = ((a_re - 1.0) * lam_re + a_im * lam_im) / den
    f_im = (a_im * lam_re - (a_re - 1.0) * lam_im) / den
    f_re, f_im = f_re[:, None, :], f_im[:, None, :]
    return a_re, a_im, f_re * b_re_t - f_im * b_im_t, f_re * b_im_t + f_im * b_re_t


def _discretize_fwd(params):
    shapes = [jax.ShapeDtypeStruct(params[0].shape, F32)] * 2 + [jax.ShapeDtypeStruct(params[3].shape, F32)] * 2

    def body(*refs):
        outs = _discretize(*[r[...] for r in refs[:5]])
        for o_ref, o in zip(refs[5:], outs):
            o_ref[...] = o

    return pl.pallas_call(body, name="s5_discretize", out_shape=shapes,
                          compiler_params=pltpu.CompilerParams(vmem_limit_bytes=V7X_VMEM_LIMIT_BYTES))(*params)


def _discretize_bwd(params, cots):
    shapes = [jax.ShapeDtypeStruct(p.shape, F32) for p in params]

    def body(*refs):
        _, vjp = jax.vjp(_discretize, *[r[...] for r in refs[:5]])
        grads = vjp(tuple(r[...] for r in refs[5:9]))
        for o_ref, o in zip(refs[9:], grads):
            o_ref[...] = o

    return pl.pallas_call(body, name="s5_discretize_bwd", out_shape=shapes,
                          compiler_params=pltpu.CompilerParams(vmem_limit_bytes=V7X_VMEM_LIMIT_BYTES))(*params, *cots)


def _scan_tiles(t_dim, n_dim, seq):
    return _tile(seq, 256, V7X_SUBLANES), _tile(n_dim, 1024, V7X_LANES)


def _scan_fwd(bu_re, bu_im, a_re, a_im, seq, comm=None):
    t_dim, n_dim = bu_re.shape
    tt, tl = _scan_tiles(t_dim, n_dim, seq)
    per_seq = seq // tt

    def body(br_ref, bi_ref, ar_ref, ai_ref, xr_ref, xi_ref, cr_ref, ci_ref):
        @pl.when(pl.program_id(1) % per_seq == 0)
        def _():
            cr_ref[...] = jnp.zeros_like(cr_ref)
            ci_ref[...] = jnp.zeros_like(ci_ref)

        ar, ai = ar_ref[...], ai_ref[...]

        def step(i, carry):
            xr, xi = carry
            row = pl.ds(i, 1)
            nr = ar * xr - ai * xi + br_ref[row, :]
            ni = ai * xr + ar * xi + bi_ref[row, :]
            xr_ref[row, :] = nr
            xi_ref[row, :] = ni
            return nr, ni

        xr, xi = lax.fori_loop(0, tt, step, (cr_ref[...], ci_ref[...]), unroll=8)
        cr_ref[...] = xr
        ci_ref[...] = xi

    blk = pl.BlockSpec((tt, tl), lambda l, t: (t, l))
    vec = pl.BlockSpec((1, tl), lambda l, t: (0, l))
    outs, c_outs = _pcall(
        body, name="s5_scan", grid=(n_dim // tl, t_dim // tt), in_specs=[blk, blk, vec, vec], out_specs=[blk, blk],
        out_shape=[jax.ShapeDtypeStruct((t_dim, n_dim), F32)] * 2, scratch_shapes=[pltpu.VMEM((1, tl), F32)] * 2,
        semantics=("parallel", "arbitrary"), operands=[bu_re, bu_im, a_re, a_im], comm=comm)
    return outs, c_outs


def _scan_bwd(gx_re, gx_im, x_re, x_im, a_re, a_im, seq, comm=None):
    t_dim, n_dim = gx_re.shape
    tt, tl = _scan_tiles(t_dim, n_dim, seq)
    per_seq = seq // tt
    n_t = t_dim // tt
    prev_rows = V7X_SUBLANES

    def body(gr_ref, gi_ref, xr_ref, xi_ref, pr_ref, pi_ref, ar_ref, ai_ref,
             lr_ref, li_ref, dar_ref, dai_ref, cr_ref, ci_ref):
        step_id = pl.program_id(1)
        blk_id = n_t - 1 - step_id

        @pl.when((blk_id + 1) % per_seq == 0)
        def _():
            cr_ref[...] = jnp.zeros_like(cr_ref)
            ci_ref[...] = jnp.zeros_like(ci_ref)

        ar, ai = ar_ref[...], ai_ref[...]

        def step(j, carry):
            lr, li = carry
            row = pl.ds(tt - 1 - j, 1)
            nr = gr_ref[row, :] + ar * lr + ai * li
            ni = gi_ref[row, :] - ai * lr + ar * li
            lr_ref[row, :] = nr
            li_ref[row, :] = ni
            return nr, ni

        lr, li = lax.fori_loop(0, tt, step, (cr_ref[...], ci_ref[...]), unroll=8)
        cr_ref[...] = lr
        ci_ref[...] = li

        first_of_seq = blk_id % per_seq == 0
        keep = jnp.where(first_of_seq, 0.0, 1.0)
        t = lax.broadcasted_iota(jnp.int32, (tt, tl), 0)
        xr_prev = jnp.where(t == 0, pr_ref[prev_rows - 1:prev_rows, :] * keep, pltpu.roll(xr_ref[...], 1, 0))
        xi_prev = jnp.where(t == 0, pi_ref[prev_rows - 1:prev_rows, :] * keep, pltpu.roll(xi_ref[...], 1, 0))
        lam_r, lam_i = lr_ref[...], li_ref[...]
        d_re = _colsum(lam_r * xr_prev + lam_i * xi_prev)
        d_im = _colsum(lam_i * xr_prev - lam_r * xi_prev)

        @pl.when(step_id == 0)
        def _():
            dar_ref[...] = d_re
            dai_ref[...] = d_im

        @pl.when(step_id > 0)
        def _():
            dar_ref[...] += d_re
            dai_ref[...] += d_im

    blk = pl.BlockSpec((tt, tl), lambda l, t: (n_t - 1 - t, l))
    prev = pl.BlockSpec((prev_rows, tl), lambda l, t: (jnp.maximum((n_t - 1 - t) * (tt // prev_rows) - 1, 0), l))
    vec = pl.BlockSpec((1, tl), lambda l, t: (0, l))
    return _pcall(
        body, name="s5_scan_bwd", grid=(n_dim // tl, n_t),
        in_specs=[blk, blk, blk, blk, prev, prev, vec, vec], out_specs=[blk, blk, vec, vec],
        out_shape=[jax.ShapeDtypeStruct((t_dim, n_dim), F32)] * 2 + [jax.ShapeDtypeStruct((1, n_dim), F32)] * 2,
        scratch_shapes=[pltpu.VMEM((1, tl), F32)] * 2, semantics=("parallel", "arbitrary"),
        operands=[gx_re, gx_im, x_re, x_im, x_re, x_im, a_re, a_im], comm=comm)


def _block_diag(w, per_block):
    g, ka, kb = w.shape
    eye = jnp.eye(per_block, dtype=w.dtype)
    out = jnp.einsum("jakb,ac->jakcb", w.reshape(g // per_block, per_block, ka, kb), eye)
    return out.reshape(g // per_block, per_block * ka, per_block * kb)


def _block_diag_t(d, per_block, ka, kb):
    j = d.shape[0]
    eye = jnp.eye(per_block, dtype=d.dtype)
    picked = jnp.einsum("jakcb,ac->jakb", d.reshape(j, per_block, ka, per_block, kb), eye)
    return picked.reshape(j * per_block, ka, kb)


def _gelu_parts(y):
    inner = GELU_C * (y + GELU_A * y * y * y)
    th = jnp.tanh(inner)
    val = 0.5 * y * (1.0 + th)
    grad = 0.5 * (1.0 + th) + 0.5 * y * (1.0 - th * th) * GELU_C * (1.0 + 3.0 * GELU_A * y * y)
    return val, grad


def _chip_sum(grad, got, core, *, name):
    r, c = got.shape[0] // N_CHIP, got.shape[1]
    tr = _tile(r, 704, 16)
    tc = _tile(c, 2048, V7X_LANES)
    per = r // tr

    def body(core_ref, g_ref, s_ref, o_ref):
        o_ref[...] = (g_ref[...].astype(F32) + s_ref[...].astype(F32)).astype(o_ref.dtype)

    slot = pl.BlockSpec((tr, tc), lambda q, i, j, core: (q * per + i, j))
    grid_spec = pltpu.PrefetchScalarGridSpec(
        num_scalar_prefetch=1, grid=(N_CHIP, per, c // tc),
        in_specs=[pl.BlockSpec((tr, tc), lambda q, i, j, core: ((2 * q + core[0]) * per + i, j)), slot],
        out_specs=slot)
    return pl.pallas_call(
        body, name=name, grid_spec=grid_spec, out_shape=jax.ShapeDtypeStruct(got.shape, got.dtype),
        compiler_params=pltpu.CompilerParams(dimension_semantics=("parallel",) * 3,
                                             vmem_limit_bytes=V7X_VMEM_LIMIT_BYTES))(core, grad, got)


def _adamw_math(w, g, m, v):
    nm = ADAM_B1 * m + (1.0 - ADAM_B1) * g
    nv = ADAM_B2 * v + (1.0 - ADAM_B2) * (g * g)
    m_hat = nm * (1.0 / (1.0 - ADAM_B1 ** ADAM_STEP))
    v_hat = nv * (1.0 / (1.0 - ADAM_B2 ** ADAM_STEP))
    return -ADAM_LR * (m_hat / (jnp.sqrt(v_hat) + ADAM_EPS) + ADAM_WD * w), nm, nv


def _sum_adamw(parts, others, chip, w, m, v, *, name):
    r, c = w.shape
    n_others = others.shape[0] // r
    tr = _tile(r, 704, 16)
    tc = _tile(c, 512, V7X_LANES)
    per = r // tr

    def body(chip_ref, p_ref, *refs):
        o_refs, (w_ref, m_ref, v_ref), (g_ref, d_ref, nm_ref, nv_ref) = refs[:n_others], refs[n_others:n_others + 3], refs[n_others + 3:]
        g = p_ref[...].astype(F32)
        for o_ref in o_refs:
            g = g + o_ref[...].astype(F32)
        g_ref[...] = g
        d_ref[...], nm_ref[...], nv_ref[...] = _adamw_math(w_ref[...], g, m_ref[...], v_ref[...])

    own = pl.BlockSpec((tr, tc), lambda i, j, chip: (i, j))
    grid_spec = pltpu.PrefetchScalarGridSpec(
        num_scalar_prefetch=1, grid=(per, c // tc),
        in_specs=[pl.BlockSpec((tr, tc), lambda i, j, chip: (chip[0] * per + i, j))]
        + [pl.BlockSpec((tr, tc), lambda i, j, chip, s=s: (s * per + i, j)) for s in range(n_others)] + [own] * 3,
        out_specs=[own] * 4)
    return pl.pallas_call(
        body, name=name, grid_spec=grid_spec, out_shape=[jax.ShapeDtypeStruct((r, c), F32)] * 4,
        compiler_params=pltpu.CompilerParams(dimension_semantics=("parallel",) * 2,
                                             vmem_limit_bytes=V7X_VMEM_LIMIT_BYTES))(chip, parts, *[others] * n_others, w, m, v)


def _sum_parts(parts, *, name):
    n_parts, r, c = parts.shape
    tr = _tile(r, 704, 16)
    tc = _tile(c, 1024, V7X_LANES)

    def body(p_ref, o_ref):
        acc = p_ref[0].astype(F32)
        for s in range(1, n_parts):
            acc = acc + p_ref[s].astype(F32)
        o_ref[...] = acc

    return _pcall(body, name=name, grid=(r // tr, c // tc),
                  in_specs=[pl.BlockSpec((n_parts, tr, tc), lambda i, j: (0, i, j))],
                  out_specs=[pl.BlockSpec((tr, tc), lambda i, j: (i, j))],
                  out_shape=[jax.ShapeDtypeStruct((r, c), F32)], semantics=("parallel", "parallel"),
                  operands=[parts])[0][0]


def _adamw(w, g, m, v, *, name):
    r, c = w.shape
    tr = _tile(r, max(V7X_SUBLANES, ADAMW_BLOCK_BYTES // (4 * c)), V7X_SUBLANES)

    def body(w_ref, g_ref, m_ref, v_ref, d_ref, nm_ref, nv_ref):
        d_ref[...], nm_ref[...], nv_ref[...] = _adamw_math(w_ref[...], g_ref[...], m_ref[...], v_ref[...])

    spec = pl.BlockSpec((tr, c), lambda i: (i, 0))
    return _pcall(body, name=name, grid=(r // tr,), in_specs=[spec] * 4, out_specs=[spec] * 3,
                  out_shape=[jax.ShapeDtypeStruct((r, c), F32)] * 3, semantics=("parallel",), operands=[w, g, m, v])[0]


def _pack(arrays, width=V7X_LANES):
    tile = V7X_SUBLANES * width
    parts, layout, row = [], [], 0
    for a in arrays:
        n = a.size
        rows = -(-n // tile) * V7X_SUBLANES
        flat = jnp.pad(a.reshape(-1).astype(F32), (0, rows * width - n))
        parts.append(flat.reshape(rows, width))
        layout.append((row, rows, n, a.shape))
        row += rows
    return jnp.concatenate(parts, axis=0), layout


def _unpack(packed, layout):
    return [packed[row:row + rows].reshape(-1)[:n].reshape(shape) for row, rows, n, shape in layout]


def kernel(x, ffn1_norm, ffn1_gate, ffn1_up, ffn1_down, mix_norm, w_in, w_pool, pool_scale, lam_re, lam_im, log_dt, b_re, b_im, c_re, c_im, d_skip, w_glu, b_glu, pool_out_norm, ssm_out_norm, w_out, ffn2_norm, ffn2_gate, ffn2_up, ffn2_down, final_norm, loss_target, m_ffn1_norm, m_ffn1_gate, m_ffn1_up, m_ffn1_down, m_mix_norm, m_w_in, m_w_pool, m_pool_scale, m_lam_re, m_lam_im, m_log_dt, m_b_re, m_b_im, m_c_re, m_c_im, m_d_skip, m_w_glu, m_b_glu, m_pool_out_norm, m_ssm_out_norm, m_w_out, m_ffn2_norm, m_ffn2_gate, m_ffn2_up, m_ffn2_down, m_final_norm, v_ffn1_norm, v_ffn1_gate, v_ffn1_up, v_ffn1_down, v_mix_norm, v_w_in, v_w_pool, v_pool_scale, v_lam_re, v_lam_im, v_log_dt, v_b_re, v_b_im, v_c_re, v_c_im, v_d_skip, v_w_glu, v_b_glu, v_pool_out_norm, v_ssm_out_norm, v_w_out, v_ffn2_norm, v_ffn2_gate, v_ffn2_up, v_ffn2_down, v_final_norm):
    weights = dict(ffn1_norm=ffn1_norm, ffn1_gate=ffn1_gate, ffn1_up=ffn1_up, ffn1_down=ffn1_down, mix_norm=mix_norm, w_in=w_in, w_pool=w_pool, pool_scale=pool_scale, lam_re=lam_re, lam_im=lam_im, log_dt=log_dt, b_re=b_re, b_im=b_im, c_re=c_re, c_im=c_im, d_skip=d_skip, w_glu=w_glu, b_glu=b_glu, pool_out_norm=pool_out_norm, ssm_out_norm=ssm_out_norm, w_out=w_out, ffn2_norm=ffn2_norm, ffn2_gate=ffn2_gate, ffn2_up=ffn2_up, ffn2_down=ffn2_down, final_norm=final_norm)
    moments_m = dict(ffn1_norm=m_ffn1_norm, ffn1_gate=m_ffn1_gate, ffn1_up=m_ffn1_up, ffn1_down=m_ffn1_down, mix_norm=m_mix_norm, w_in=m_w_in, w_pool=m_w_pool, pool_scale=m_pool_scale, lam_re=m_lam_re, lam_im=m_lam_im, log_dt=m_log_dt, b_re=m_b_re, b_im=m_b_im, c_re=m_c_re, c_im=m_c_im, d_skip=m_d_skip, w_glu=m_w_glu, b_glu=m_b_glu, pool_out_norm=m_pool_out_norm, ssm_out_norm=m_ssm_out_norm, w_out=m_w_out, ffn2_norm=m_ffn2_norm, ffn2_gate=m_ffn2_gate, ffn2_up=m_ffn2_up, ffn2_down=m_ffn2_down, final_norm=m_final_norm)
    moments_v = dict(ffn1_norm=v_ffn1_norm, ffn1_gate=v_ffn1_gate, ffn1_up=v_ffn1_up, ffn1_down=v_ffn1_down, mix_norm=v_mix_norm, w_in=v_w_in, w_pool=v_w_pool, pool_scale=v_pool_scale, lam_re=v_lam_re, lam_im=v_lam_im, log_dt=v_log_dt, b_re=v_b_re, b_im=v_b_im, c_re=v_c_re, c_im=v_c_im, d_skip=v_d_skip, w_glu=v_w_glu, b_glu=v_b_glu, pool_out_norm=v_pool_out_norm, ssm_out_norm=v_ssm_out_norm, w_out=v_w_out, ffn2_norm=v_ffn2_norm, ffn2_gate=v_ffn2_gate, ffn2_up=v_ffn2_up, ffn2_down=v_ffn2_down, final_norm=v_final_norm)
    names = list(weights)

    n_seq, seq, d_model = x.shape
    t_dim = n_seq * seq
    ff_shard = ffn1_gate.shape[1]
    ff = N_DEV * ff_shard
    ff_pad = -(-ff // FF_PAD_MULTIPLE) * FF_PAD_MULTIPLE - ff
    n_pool, pool_gw = w_pool.shape[0], w_pool.shape[2]
    pool_w = n_pool * pool_gw
    pool_rows = pool_gw // N_DEV
    n_grp, n_state, n_ch = b_re.shape
    ssm_w = n_grp * n_ch
    grp_per_blk = V7X_LANES // n_ch
    n_blk = n_grp // grp_per_blk
    ch_blk, st_blk = grp_per_blk * n_ch, grp_per_blk * n_state
    ssm_off = pool_w // ch_blk

    x2 = x.reshape(t_dim, d_model)
    tgt2 = loss_target.reshape(t_dim, d_model)
    row = lambda p: p.reshape(1, -1)

    sharded = ["ffn1_gate", "ffn1_up", "ffn1_down", "ffn2_gate", "ffn2_up", "ffn2_down", "w_in", "w_out", "w_glu", "w_pool"]
    transposed = {"ffn1_gate", "ffn1_up", "ffn2_gate", "ffn2_up"}
    shard = {n: (weights[n].T if n in transposed else weights[n]).astype(BF16) for n in sharded}

    def gather(*group):
        return _Gather([shard[n] for n in group], [ff_pad if n.startswith("ffn") else 0 for n in group])

    full = {}
    full["ffn1_gate"], full["ffn1_up"] = _comm_only(gather("ffn1_gate", "ffn1_up"), name="gather_first")

    def ffn1_down_weights(got):
        full["ffn1_down"], full["w_in"] = got
        return full["ffn1_down"]

    h1, saved1, _, got = _ffn_fwd(x2, row(ffn1_norm), full["ffn1_gate"], full["ffn1_up"], ffn1_down_weights, "ffn1",
                                  gather("ffn1_down", "w_in"), gather("ffn2_gate"))
    (full["ffn2_gate"],) = got
    n2, r2 = _rms_fwd(h1, row(mix_norm), name="mix_rms")
    z, (full["w_glu"], full["w_pool"]) = _matmul([(n2, full["w_in"])], [F32], name="mix_in", comm=gather("w_glu", "w_pool"))

    d_pool = _pool_window(z, seq, pool_w, pool_gw, name="pool_window", transpose=False)
    y_pool_lin = _bd([(d_pool, full["w_pool"])], F32, name="pool_mix")

    disc_params = (lam_re, lam_im, log_dt.reshape(n_grp, 1), jnp.swapaxes(b_re, 1, 2), jnp.swapaxes(b_im, 1, 2))
    a_re, a_im, bb_re_t, bb_im_t = _discretize_fwd(disc_params)
    a_re_row, a_im_row = row(a_re), row(a_im)
    wb_re, wb_im = _block_diag(bb_re_t, grp_per_blk), _block_diag(bb_im_t, grp_per_blk)
    wc_re = _block_diag(jnp.swapaxes(c_re, 1, 2), grp_per_blk)
    wc_im = _block_diag(jnp.swapaxes(c_im, 1, 2), grp_per_blk)
    bu_re = _bd([(z, wb_re)], F32, name="s5_bu_re", offs=[ssm_off], operand_dtype=BF16)
    bu_im = _bd([(z, wb_im)], F32, name="s5_bu_im", offs=[ssm_off], operand_dtype=BF16)
    (xs_re, xs_im), (full["ffn2_up"],) = _scan_fwd(bu_re, bu_im, a_re_row, a_im_row, seq, comm=gather("ffn2_up"))
    y_lin, (full["w_out"],) = _bd([(xs_re, wc_re), (xs_im, -wc_im)], F32, name="s5_cx", operand_dtype=BF16,
                                  comm=gather("w_out"))

    def s5_post(y_lin, u, skip):
        y = y_lin + skip * u
        return y, _gelu_parts(y)[0].astype(BF16)

    u_cols = (z, ssm_w, pool_w // ssm_w)
    y_ssm, yg = _rowk(s5_post, [y_lin, u_cols], [row(d_skip)], [("row", ssm_w, F32), ("row", ssm_w, BF16)], name="s5_post")

    def glu(accs, ex, bc):
        q = accs[0] + bc[0]
        return q, ex[0].astype(F32) * jax.nn.sigmoid(q)

    q_glu, y_s5 = _matmul([(yg, full["w_glu"])], [F32, F32], name="s5_glu", extras=[yg], bcast=[row(b_glu)], epilogue=glu)

    def merge(yp_lin, ys, scale, gp, gs):
        yp = yp_lin * scale
        rp = lax.rsqrt(jnp.mean(yp * yp, axis=-1, keepdims=True) + NORM_EPS)
        rs = lax.rsqrt(jnp.mean(ys * ys, axis=-1, keepdims=True) + NORM_EPS)
        merged = jnp.concatenate([yp * rp * gp, ys * rs * gs], axis=-1)
        return merged.astype(BF16), rp, rs

    merged, r_pool, r_ssm = _rowk(merge, [y_pool_lin, y_s5], [row(pool_scale), row(pool_out_norm), row(ssm_out_norm)],
                                  [("row", pool_w + ssm_w, BF16), ("row", 1, F32), ("row", 1, F32)], name="mix_merge")
    h2 = _matmul([(merged, full["w_out"])], [F32], name="mix_out", extras=[h1],
                 epilogue=lambda accs, ex, bc: (ex[0] + accs[0],))

    def ffn2_down_weights(got):
        (full["ffn2_down"],) = got
        return full["ffn2_down"]

    h3, saved2, _, _ = _ffn_fwd(h2, row(ffn2_norm), full["ffn2_gate"], full["ffn2_up"], ffn2_down_weights, "ffn2",
                                gather("ffn2_down"), None)

    def head(h, tgt, gain):
        r = lax.rsqrt(jnp.mean(h * h, axis=-1, keepdims=True) + NORM_EPS)
        xh = h * r
        err = xh * gain - tgt
        loss = jnp.sum(0.5 * jnp.mean(err * err, axis=-1, keepdims=True), axis=0, keepdims=True)
        dout = err * (1.0 / d_model)
        dy = dout * gain
        dh = r * (dy - xh * jnp.mean(dy * xh, axis=-1, keepdims=True))
        return dh, (0.5 * dh).astype(BF16), _colsum(dout * xh), jnp.broadcast_to(loss, (1, V7X_LANES))

    dh3, dyb2, g_final, loss_part = _rowk(head, [h3, tgt2], [row(final_norm)],
                                          [("row", d_model, F32), ("row", d_model, BF16), ("sum", d_model), ("sum", V7X_LANES)],
                                          name="loss_head")

    grads = {}
    shard_rows = {n: (n_pool * pool_rows if n == "w_pool" else shard[n].shape[0]) for n in sharded}

    def to_sibling(group, name):
        got = _comm_only(_SiblingSwap([grads[n] for n in group], [shard_rows[n] for n in group]), name=name)
        for n, g in zip(group, got):
            chip_parts[n] = _chip_sum(grads[n], g, my_core, name=f"chip_sum_{n}")
        return [chip_parts[n] for n in group]

    my_core = lax.axis_index("c").astype(jnp.int32).reshape(1)
    my_chip = (2 * lax.axis_index("x") + lax.axis_index("y")).astype(jnp.int32).reshape(1)
    chip_parts, from_chips = {}, {}

    def to_chips(group, parts):
        return _ChipScatter(parts), group

    def take(group, got):
        for n, g in zip(group, got):
            from_chips[n] = g

    def ffn_backward(tag, dh, dyb, saved, norm_g, first, last=None):
        gate, up, down = f"{tag}_gate", f"{tag}_up", f"{tag}_down"
        n, a = saved[1], saved[5]
        plan, group = first
        grads[down], got = _with_comm(_matmul([(a, dyb)], [BF16], name=f"{tag}_dwdown", ta=True, comm=plan), plan)
        take(group, got)
        plan, group = to_chips([down], to_sibling([down], f"grads_to_sibling_{down}"))
        (dg, du), got = _ffn_bwd_act(dyb, saved, full[down], tag, plan)
        take(group, got)
        grads[gate] = _matmul([(dg, n)], [BF16], name=f"{tag}_dwgate", ta=True)
        plan, group = to_chips([gate], to_sibling([gate], f"grads_to_sibling_{gate}"))
        grads[up], got = _matmul([(du, n)], [BF16], name=f"{tag}_dwup", ta=True, comm=plan)
        take(group, got)
        plan, group = to_chips([up], to_sibling([up], f"grads_to_sibling_{up}"))
        if last is not None:
            plan = _Both(plan, last)
        dx, _, dx_b, grads[f"{tag}_norm"], got = _ffn_bwd_input(dh, dg, du, saved, norm_g, full[gate], full[up], tag, plan)
        take(group, got)
        return dx, dx_b, got[len(group):]

    dh2, dh2b, _ = ffn_backward("ffn2", dh3, dyb2, saved2, row(ffn2_norm), (None, []))

    d_merged = _matmul([(dh2b, full["w_out"])], [F32], name="mix_out_dx", tb=True)
    grads["w_out"] = _matmul([(merged, dh2b)], [BF16], name="mix_out_dw", ta=True)

    def merge_bwd(dm, yp_lin, ys, rp, rs, scale, gp, gs):
        yp = yp_lin * scale
        d_yp, d_gp = _rms_bwd_rows(dm[:, :pool_w], yp, rp, gp)
        d_ys, d_gs = _rms_bwd_rows(dm[:, pool_w:], ys, rs, gs)
        return (d_yp * scale).astype(BF16), d_ys, _colsum(d_yp * yp_lin), d_gp, d_gs

    d_pool_lin, d_ys, grads["pool_scale"], grads["pool_out_norm"], grads["ssm_out_norm"] = _rowk(
        merge_bwd, [d_merged, y_pool_lin, y_s5, r_pool, r_ssm], [row(pool_scale), row(pool_out_norm), row(ssm_out_norm)],
        [("row", pool_w, BF16), ("row", ssm_w, F32), ("sum", pool_w), ("sum", pool_w), ("sum", ssm_w)], name="mix_merge_bwd")

    w_pool_t = jnp.swapaxes(full["w_pool"], 1, 2)
    dd_pool = _bd([(d_pool_lin, w_pool_t)], F32, name="pool_mix_dx")
    g_pool = _bd_wgrad(d_pool, d_pool_lin, pool_gw, pool_gw, n_pool, name="pool_mix_dw")
    grads["w_pool"] = g_pool.reshape(n_pool, N_DEV, pool_rows, pool_gw).transpose(1, 0, 2, 3).reshape(
        N_DEV * n_pool * pool_rows, pool_gw).astype(BF16)
    dz_pool = _pool_window(dd_pool, seq, pool_w, pool_gw, name="pool_window_bwd", transpose=True)

    def glu_bwd(d_ys, yg, q):
        sg = jax.nn.sigmoid(q)
        dq = d_ys * yg.astype(F32) * sg * (1.0 - sg)
        return dq.astype(BF16), d_ys * sg, _colsum(dq)

    dq, d_yg_direct, grads["b_glu"] = _rowk(glu_bwd, [d_ys, yg, q_glu], [],
                                            [("row", ssm_w, BF16), ("row", ssm_w, F32), ("sum", ssm_w)], name="s5_glu_bwd")
    d_yg_mm = _matmul([(dq, full["w_glu"])], [F32], name="s5_glu_dx", tb=True)
    grads["w_glu"] = _matmul([(yg, dq)], [BF16], name="s5_glu_dw", ta=True)

    def gelu_bwd(d1, d2, y, u, skip):
        dy = (d1 + d2) * _gelu_parts(y)[1]
        return dy, dy * skip, _colsum(dy * u)

    dy_ssm, du_skip, grads["d_skip"] = _rowk(gelu_bwd, [d_yg_direct, d_yg_mm, y_ssm, u_cols], [row(d_skip)],
                                             [("row", ssm_w, F32), ("row", ssm_w, F32), ("sum", ssm_w)], name="s5_gelu_bwd")
    wc_re_t, wc_im_t = jnp.swapaxes(wc_re, 1, 2), jnp.swapaxes(wc_im, 1, 2)
    gx_re = _bd([(dy_ssm, wc_re_t)], F32, name="s5_gx_re", operand_dtype=BF16)
    gx_im = _bd([(dy_ssm, -wc_im_t)], F32, name="s5_gx_im", operand_dtype=BF16)
    d_wc_re = _bd_wgrad(xs_re, dy_ssm, st_blk, ch_blk, n_blk, name="s5_dc_re", operand_dtype=BF16)
    d_wc_im = _bd_wgrad(xs_im, dy_ssm, st_blk, ch_blk, n_blk, name="s5_dc_im", operand_dtype=BF16, sign=-1.0)
    (lm_re, lm_im, da_re, da_im), _ = _scan_bwd(gx_re, gx_im, xs_re, xs_im, a_re_row, a_im_row, seq)
    d_wb_re = _bd_wgrad(z, lm_re, ch_blk, st_blk, n_blk, name="s5_db_re", off_a=ssm_off, operand_dtype=BF16)
    d_wb_im = _bd_wgrad(z, lm_im, ch_blk, st_blk, n_blk, name="s5_db_im", off_a=ssm_off, operand_dtype=BF16)
    wb_re_t, wb_im_t = jnp.swapaxes(wb_re, 1, 2), jnp.swapaxes(wb_im, 1, 2)
    du_lin = _bd([(lm_re, wb_re_t), (lm_im, wb_im_t)], F32, name="s5_du", operand_dtype=BF16)
    grads["c_re"] = jnp.swapaxes(_block_diag_t(d_wc_re, grp_per_blk, n_state, n_ch), 1, 2)
    grads["c_im"] = jnp.swapaxes(_block_diag_t(d_wc_im, grp_per_blk, n_state, n_ch), 1, 2)
    d_bb_re_t = _block_diag_t(d_wb_re, grp_per_blk, n_ch, n_state)
    d_bb_im_t = _block_diag_t(d_wb_im, grp_per_blk, n_ch, n_state)
    g_lam_re, g_lam_im, g_log_dt, g_b_re_t, g_b_im_t = _discretize_bwd(
        disc_params, (da_re.reshape(n_grp, n_state), da_im.reshape(n_grp, n_state), d_bb_re_t, d_bb_im_t))
    grads["lam_re"], grads["lam_im"], grads["log_dt"] = g_lam_re, g_lam_im, g_log_dt.reshape(n_grp)
    grads["b_re"], grads["b_im"] = jnp.swapaxes(g_b_re_t, 1, 2), jnp.swapaxes(g_b_im_t, 1, 2)

    def join(dzp, du1, du2):
        return (jnp.concatenate([dzp, du1 + du2], axis=-1).astype(BF16),)

    (dz,) = _rowk(join, [dz_pool, du_lin, du_skip], [], [("row", pool_w + ssm_w, BF16)], name="mix_in_join")
    dn2 = _matmul([(dz, full["w_in"])], [F32], name="mix_in_dx", tb=True)
    grads["w_in"] = _matmul([(n2, dz)], [BF16], name="mix_in_dw", ta=True)
    mixer = ["w_out", "w_glu", "w_pool", "w_in"]
    parts_mixer = to_sibling(mixer, "grads_to_sibling_mixer")

    def mix_rms_bwd(dh, dn, h, r, gain):
        dx, dgain = _rms_bwd_rows(dn, h, r, gain)
        dx = dh + dx
        return dx, (0.5 * dx).astype(BF16), dgain

    dh1, dyb1, grads["mix_norm"] = _rowk(mix_rms_bwd, [dh2, dn2, h1, r2], [row(mix_norm)],
                                         [("row", d_model, F32), ("row", d_model, BF16), ("sum", d_model)], name="mix_rms_bwd")

    grads["final_norm"] = g_final
    late = ["ffn1_norm"]
    early = [n for n in names if n not in sharded and n not in late]
    packed_early, layout = _pack([grads[n].reshape(weights[n].shape) for n in early] + [loss_part])
    dx, _, (gathered_early,) = ffn_backward("ffn1", dh1, dyb1, saved1, row(ffn1_norm), to_chips(mixer, parts_mixer),
                                            last=_Broadcast([packed_early]))
    packed_late, layout_late = _pack([grads[n].reshape(weights[n].shape) for n in late])
    (gathered_late,) = _comm_only(_Broadcast([packed_late]), name="small_all_gather_last")
    layout = layout + [(packed_early.shape[0] + r0, rows, n, shape) for r0, rows, n, shape in layout_late]
    small_sum = _sum_parts(jnp.concatenate([gathered_early, gathered_late], axis=1), name="sum_small")
    unpacked = _unpack(small_sum, layout)
    summed = dict(zip(early, unpacked[:len(early)]))
    summed.update(zip(late, unpacked[len(early) + 1:]))
    loss = unpacked[len(early)][0, 0]

    def packed(values, loss_slot):
        return _pack([values[n] for n in early] + [loss_slot] + [values[n] for n in late])[0]

    small_out = [_unpack(o, layout) for o in _adamw(
        packed(weights, jnp.zeros_like(loss_part)), small_sum, packed(moments_m, jnp.zeros_like(loss_part)),
        packed(moments_v, jnp.ones_like(loss_part)), name="adamw_small")]
    delta, new_m, new_v = {}, {}, {}
    for i, n in enumerate(early + [None] + late):
        if n is not None:
            delta[n], new_m[n], new_v[n] = small_out[0][i], small_out[1][i], small_out[2][i]
    for n in sharded:
        shape = weights[n].shape
        if n in transposed:
            to2d, back = (lambda a: a.T), (lambda a: a.T)
        else:
            to2d, back = (lambda a: a.reshape(-1, shape[-1])), (lambda a: a.reshape(shape))
        g, d, nm, nv = _sum_adamw(chip_parts[n], from_chips[n], my_chip, to2d(weights[n]), to2d(moments_m[n]),
                                  to2d(moments_v[n]), name=f"adamw_{n}")
        summed[n], delta[n], new_m[n], new_v[n] = back(g), back(d), back(nm), back(nv)

    return (loss, dx.reshape(x.shape), *[summed[n] for n in names], *[delta[n] for n in names],
            *[new_m[n] for n in names], *[new_v[n] for n in names])
```

```python
import functools
import math

import jax
import jax.numpy as jnp
from jax import lax
from jax.experimental import pallas as pl
from jax.experimental.pallas import tpu as pltpu

F32 = jnp.float32
BF16 = jnp.bfloat16
MESH = pl.DeviceIdType.MESH
ANY = pl.BlockSpec(memory_space=pl.ANY)

N_DEV = 8
N_CHIP = 4
V7X_LANES = 128
V7X_SUBLANES = 8
V7X_VMEM_LIMIT_BYTES = 56 * 1024 * 1024
FF_PAD_MULTIPLE = 1024
ADAMW_BLOCK_BYTES = 1 << 20

NORM_EPS = 1e-6
POOL_WINDOWS = (2, 4, 8, 16)
ADAM_LR, ADAM_B1, ADAM_B2, ADAM_EPS, ADAM_WD, ADAM_STEP = 0.001, 0.9, 0.999, 1e-08, 0.01, 10
GELU_C = math.sqrt(2.0 / math.pi)
GELU_A = 0.044715


def _tile(dim, pref, mult):
    t = min(pref, dim)
    t -= t % mult
    while t >= mult:
        if dim % t == 0:
            return t
        t -= mult
    return dim


def _pcall(body, *, name, grid, in_specs, out_specs, out_shape, operands, scratch_shapes=(), semantics=None, comm=None):
    if comm is None:
        params = pltpu.CompilerParams(dimension_semantics=semantics, vmem_limit_bytes=V7X_VMEM_LIMIT_BYTES)
        outs = pl.pallas_call(body, name=name, grid=grid, in_specs=list(in_specs), out_specs=list(out_specs),
                              out_shape=list(out_shape), scratch_shapes=list(scratch_shapes),
                              compiler_params=params)(*operands)
        return list(outs), []
    sizes = [len(in_specs), len(comm.inputs), len(out_shape), len(comm.out_shape), len(scratch_shapes),
             len(comm.sem_shapes)]

    def wrapped(*refs):
        groups, pos = [], 0
        for n in sizes:
            groups.append(refs[pos:pos + n])
            pos += n
        ins, c_ins, outs, c_outs, scratch, sems = groups
        if not grid:
            comm.start(c_ins, c_outs, sems)
            if body is not None:
                body(*ins, *outs, *scratch)
            comm.finish(c_ins, c_outs, sems)
            return
        ids = [pl.program_id(a) for a in range(len(grid))]
        first = functools.reduce(jnp.logical_and, [i == 0 for i in ids])
        last = functools.reduce(jnp.logical_and, [i == g - 1 for i, g in zip(ids, grid)])

        @pl.when(first)
        def _():
            comm.start(c_ins, c_outs, sems)

        body(*ins, *outs, *scratch)

        @pl.when(last)
        def _():
            comm.finish(c_ins, c_outs, sems)

    params = pltpu.CompilerParams(dimension_semantics=("arbitrary",) * len(grid), vmem_limit_bytes=V7X_VMEM_LIMIT_BYTES)
    res = pl.pallas_call(wrapped, name=name, grid=grid, in_specs=list(in_specs) + [ANY] * sizes[1],
                         out_specs=list(out_specs) + [ANY] * sizes[3], out_shape=list(out_shape) + list(comm.out_shape),
                         scratch_shapes=list(scratch_shapes) + list(comm.sem_shapes),
                         compiler_params=params)(*operands, *comm.inputs)
    return list(res[:sizes[2]]), list(res[sizes[2]:])


def _my_place():
    return lax.axis_index("x"), lax.axis_index("y"), lax.axis_index("c")


def _block_index(px, py, pc):
    return 4 * px + 2 * py + pc


class _Gather:
    def __init__(self, shards, pads):
        self.n, self.pads = len(shards), list(pads)
        zero_blocks = [jnp.zeros((p, s.shape[1]), s.dtype) for s, p in zip(shards, pads) if p]
        self.inputs = list(shards) + zero_blocks

        def full_shape(s, pad):
            if s.ndim == 2:
                return (N_DEV * s.shape[0] + pad, s.shape[1])
            return (s.shape[0], N_DEV * s.shape[1], s.shape[2])

        self.out_shape = [jax.ShapeDtypeStruct(full_shape(s, p), s.dtype) for s, p in zip(shards, pads)]
        self.sem_shapes = [pltpu.SemaphoreType.DMA((7 * self.n,)), pltpu.SemaphoreType.DMA((7 * self.n,)),
                           pltpu.SemaphoreType.DMA((self.n + len(zero_blocks),))]

    def _copies(self, ins, outs, sems):
        n = self.n
        send_sems, recv_sems, local_sems = sems
        x, y, c = _my_place()
        me, sibling = (x, y, c), (x, y, 1 - c)
        chips = [(1 - x, y), (x, 1 - y), (1 - x, 1 - y)]

        def rows(i, place):
            idx = _block_index(*place)
            r = ins[i].shape[-2]
            if ins[i].ndim == 2:
                return outs[i].at[pl.ds(idx * r, r), :]
            return outs[i].at[:, pl.ds(idx * r, r), :]

        def copy(i, k, block, to, src=None):
            return pltpu.make_async_remote_copy(
                src_ref=rows(i, block) if src is None else src, dst_ref=rows(i, block),
                send_sem=send_sems.at[7 * i + k], recv_sem=recv_sems.at[7 * i + k],
                device_id=to, device_id_type=MESH)

        local = [pltpu.make_async_copy(ins[i], rows(i, me), local_sems.at[i]) for i in range(n)]
        zi = 0
        for i in range(n):
            if self.pads[i]:
                start = N_DEV * ins[i].shape[0]
                local.append(pltpu.make_async_copy(ins[n + zi], outs[i].at[pl.ds(start, self.pads[i]), :],
                                                   local_sems.at[n + zi]))
                zi += 1
        first = []
        for i in range(n):
            first.append(copy(i, 0, me, sibling, src=ins[i]))
            first += [copy(i, 1 + j, me, (*chip, c), src=ins[i]) for j, chip in enumerate(chips)]
        return local, first, copy, chips, me, sibling, c

    def start(self, ins, outs, sems):
        local, first, *_ = self._copies(ins, outs, sems)
        for cp in local + first:
            cp.start()

    def finish(self, ins, outs, sems):
        local, first, copy, chips, me, sibling, c = self._copies(ins, outs, sems)
        passed = []
        for j, chip in enumerate(chips):
            for i in range(self.n):
                copy(i, 1 + j, (*chip, c), me).wait_recv()
                fwd = copy(i, 4 + j, (*chip, c), sibling)
                fwd.start()
                passed.append(fwd)
        for i in range(self.n):
            copy(i, 0, sibling, me).wait_recv()
        for j, chip in enumerate(chips):
            for i in range(self.n):
                copy(i, 4 + j, (*chip, 1 - c), me).wait_recv()
        for cp in first + passed:
            cp.wait_send()
        for cp in local:
            cp.wait()


class _SiblingSwap:
    def __init__(self, grads, rows):
        self.n, self.rows = len(grads), list(rows)
        self.inputs = list(grads)
        self.out_shape = [jax.ShapeDtypeStruct((N_CHIP * r, g.shape[1]), g.dtype) for g, r in zip(grads, rows)]
        self.sem_shapes = [pltpu.SemaphoreType.DMA((N_CHIP * self.n,))] * 2

    def _copies(self, ins, outs, sems):
        send_sems, recv_sems = sems
        x, y, c = _my_place()
        copies = []
        for i in range(self.n):
            r = self.rows[i]
            for q in range(N_CHIP):
                k = N_CHIP * i + q
                copies.append(pltpu.make_async_remote_copy(
                    src_ref=ins[i].at[pl.ds((2 * q + 1 - c) * r, r), :], dst_ref=outs[i].at[pl.ds(q * r, r), :],
                    send_sem=send_sems.at[k], recv_sem=recv_sems.at[k], device_id=(x, y, 1 - c), device_id_type=MESH))
        return copies

    def start(self, ins, outs, sems):
        for cp in self._copies(ins, outs, sems):
            cp.start()

    def finish(self, ins, outs, sems):
        copies = self._copies(ins, outs, sems)
        for cp in copies:
            cp.wait_recv()
        for cp in copies:
            cp.wait_send()


class _ChipScatter:
    FLIPS = [(0, 1), (1, 0), (1, 1)]

    def __init__(self, parts):
        self.n = len(parts)
        self.inputs = list(parts)
        self.out_shape = [jax.ShapeDtypeStruct((len(self.FLIPS) * (p.shape[0] // N_CHIP), p.shape[1]), p.dtype)
                          for p in parts]
        self.sem_shapes = [pltpu.SemaphoreType.DMA((3 * self.n,))] * 2

    def _copies(self, ins, outs, sems):
        send_sems, recv_sems = sems
        x, y, c = _my_place()
        copies = []
        for i in range(self.n):
            r = ins[i].shape[0] // N_CHIP
            for k, (fx, fy) in enumerate(self.FLIPS):
                px, py = x ^ fx, y ^ fy
                copies.append(pltpu.make_async_remote_copy(
                    src_ref=ins[i].at[pl.ds((2 * px + py) * r, r), :], dst_ref=outs[i].at[pl.ds(k * r, r), :],
                    send_sem=send_sems.at[3 * i + k], recv_sem=recv_sems.at[3 * i + k],
                    device_id=(px, py, c), device_id_type=MESH))
        return copies

    def start(self, ins, outs, sems):
        for cp in self._copies(ins, outs, sems):
            cp.start()

    def finish(self, ins, outs, sems):
        copies = self._copies(ins, outs, sems)
        for cp in copies:
            cp.wait_recv()
        for cp in copies:
            cp.wait_send()


_FLIPS = [(0, 0, 1), (0, 1, 0), (0, 1, 1), (1, 0, 0), (1, 0, 1), (1, 1, 0), (1, 1, 1)]


class _Broadcast:
    def __init__(self, arrays):
        self.n = len(arrays)
        self.inputs = list(arrays)
        self.out_shape = [jax.ShapeDtypeStruct((N_DEV, *a.shape), a.dtype) for a in arrays]
        self.sem_shapes = [pltpu.SemaphoreType.DMA((7 * self.n,)), pltpu.SemaphoreType.DMA((7 * self.n,)),
                           pltpu.SemaphoreType.DMA((self.n,))]

    def _copies(self, ins, outs, sems):
        send_sems, recv_sems, local_sems = sems
        x, y, c = _my_place()
        my_idx = _block_index(x, y, c)
        local, sends, recvs = [], [], []
        for i in range(self.n):
            local.append(pltpu.make_async_copy(ins[i], outs[i].at[my_idx], local_sems.at[i]))
            for k, (fx, fy, fc) in enumerate(_FLIPS):
                peer = (x ^ fx, y ^ fy, c ^ fc)
                common = dict(send_sem=send_sems.at[7 * i + k], recv_sem=recv_sems.at[7 * i + k],
                              device_id=peer, device_id_type=MESH)
                sends.append(pltpu.make_async_remote_copy(src_ref=ins[i], dst_ref=outs[i].at[my_idx], **common))
                recvs.append(pltpu.make_async_remote_copy(src_ref=ins[i], dst_ref=outs[i].at[_block_index(*peer)], **common))
        return local, sends, recvs

    def start(self, ins, outs, sems):
        local, sends, _ = self._copies(ins, outs, sems)
        for cp in local + sends:
            cp.start()

    def finish(self, ins, outs, sems):
        local, sends, recvs = self._copies(ins, outs, sems)
        for cp in recvs:
            cp.wait_recv()
        for cp in sends:
            cp.wait_send()
        for cp in local:
            cp.wait()


class _Both:
    def __init__(self, first, second):
        self.plans = (first, second)
        self.inputs = first.inputs + second.inputs
        self.out_shape = first.out_shape + second.out_shape
        self.sem_shapes = first.sem_shapes + second.sem_shapes

    def _split(self, ins, outs, sems):
        a = self.plans[0]
        n_in, n_out, n_sem = len(a.inputs), len(a.out_shape), len(a.sem_shapes)
        return [(ins[:n_in], outs[:n_out], sems[:n_sem]), (ins[n_in:], outs[n_out:], sems[n_sem:])]

    def start(self, ins, outs, sems):
        for plan, args in zip(self.plans, self._split(ins, outs, sems)):
            plan.start(*args)

    def finish(self, ins, outs, sems):
        for plan, args in zip(self.plans, self._split(ins, outs, sems)):
            plan.finish(*args)


def _comm_only(comm, *, name):
    return _pcall(None, name=name, grid=(), in_specs=[], out_specs=[], out_shape=[], operands=[], comm=comm)[1]


def _matmul(pairs, out_dtypes, *, name, ta=False, tb=False, separate=False, epilogue=None,
            extras=(), bcast=(), tm=512, tn=512, tk=4096, precision=None, comm=None):
    a0, b0 = pairs[0]
    m_dim, k_dim = (a0.shape[1], a0.shape[0]) if ta else a0.shape
    n_dim = b0.shape[0] if tb else b0.shape[1]
    tm = _tile(m_dim, tm, V7X_LANES if ta else 16)
    tn = _tile(n_dim, tn, V7X_LANES)
    tk = _tile(k_dim, tk, V7X_LANES)
    nk = k_dim // tk
    n_acc = len(pairs) if separate else 1
    n_ex, n_bc, n_out = len(extras), len(bcast), len(out_dtypes)
    dims = (((0 if ta else 1,), (1 if tb else 0,)), ((), ()))
    if epilogue is None:
        epilogue = lambda accs, ex, bc: tuple(accs)
    a_spec = pl.BlockSpec((tk, tm), lambda m, n, k: (k, m)) if ta else pl.BlockSpec((tm, tk), lambda m, n, k: (m, k))
    b_spec = pl.BlockSpec((tn, tk), lambda m, n, k: (n, k)) if tb else pl.BlockSpec((tk, tn), lambda m, n, k: (k, n))
    operands, operand_specs, pair_slots = [], [], []
    for pair in pairs:
        slots = []
        for arr, spec in zip(pair, (a_spec, b_spec)):
            found = [i for i, o in enumerate(operands) if o is arr]
            if not found:
                operands.append(arr)
                operand_specs.append(spec)
                found = [len(operands) - 1]
            slots.append(found[0])
        pair_slots.append(slots)
    n_ops = len(operands)

    def body(*refs):
        a_refs = [refs[sa] for sa, _ in pair_slots]
        b_refs = [refs[sb] for _, sb in pair_slots]
        ex_refs = refs[n_ops:n_ops + n_ex]
        bc_refs = refs[n_ops + n_ex:n_ops + n_ex + n_bc]
        out_refs = refs[n_ops + n_ex + n_bc:n_ops + n_ex + n_bc + n_out]
        acc_refs = refs[n_ops + n_ex + n_bc + n_out:]
        parts = [lax.dot_general(a[...], b[...], dims, preferred_element_type=F32, precision=precision)
                 for a, b in zip(a_refs, b_refs)]
        if not separate:
            parts = [functools.reduce(lambda p, q: p + q, parts)]

        def finish(accs):
            outs = epilogue(accs, [e[...] for e in ex_refs], [c[...] for c in bc_refs])
            for o_ref, o in zip(out_refs, outs):
                o_ref[...] = o.astype(o_ref.dtype)

        if nk == 1:
            finish(parts)
        else:
            k = pl.program_id(2)

            @pl.when(k == 0)
            def _():
                for acc, p in zip(acc_refs, parts):
                    acc[...] = p

            @pl.when(k > 0)
            def _():
                for acc, p in zip(acc_refs, parts):
                    acc[...] += p

            @pl.when(k == nk - 1)
            def _():
                finish([acc[...] for acc in acc_refs])

    mn_spec = pl.BlockSpec((tm, tn), lambda m, n, k: (m, n))
    bc_spec = pl.BlockSpec((1, tn), lambda m, n, k: (0, n))
    outs, c_outs = _pcall(
        body, name=name, grid=(m_dim // tm, n_dim // tn, nk),
        in_specs=operand_specs + [mn_spec] * n_ex + [bc_spec] * n_bc,
        out_specs=[mn_spec] * n_out,
        out_shape=[jax.ShapeDtypeStruct((m_dim, n_dim), d) for d in out_dtypes],
        scratch_shapes=[pltpu.VMEM((tm, tn), F32)] * (n_acc if nk > 1 else 0),
        semantics=("parallel", "parallel", "arbitrary"),
        operands=[*operands, *extras, *bcast], comm=comm)
    outs = outs[0] if n_out == 1 else outs
    return outs if comm is None else (outs, c_outs)


def _with_comm(result, comm):
    return (result, []) if comm is None else result


def _bd(pairs, out_dtype, *, name, offs=None, tm=1024, operand_dtype=None, comm=None):
    n_blocks, ka, kb = pairs[0][1].shape
    t_dim = pairs[0][0].shape[0]
    tm = _tile(t_dim, tm, 16)
    offs = offs or [0] * len(pairs)
    n_pairs = len(pairs)

    def body(*refs):
        acc = None
        for i in range(n_pairs):
            a, w = refs[2 * i][...], refs[2 * i + 1][...]
            if operand_dtype is not None:
                a, w = a.astype(operand_dtype), w.astype(operand_dtype)
            p = jnp.dot(a, w, preferred_element_type=F32)
            acc = p if acc is None else acc + p
        refs[2 * n_pairs][...] = acc.astype(out_dtype)

    in_specs = []
    for off in offs:
        in_specs.append(pl.BlockSpec((tm, ka), lambda j, t, off=off: (t, j + off)))
        in_specs.append(pl.BlockSpec((None, ka, kb), lambda j, t: (j, 0, 0)))
    outs, c_outs = _pcall(
        body, name=name, grid=(n_blocks, t_dim // tm), in_specs=in_specs,
        out_specs=[pl.BlockSpec((tm, kb), lambda j, t: (t, j))],
        out_shape=[jax.ShapeDtypeStruct((t_dim, n_blocks * kb), out_dtype)],
        semantics=("parallel", "parallel"), operands=[t for p in pairs for t in p], comm=comm)
    return outs[0] if comm is None else (outs[0], c_outs)


def _bd_wgrad(a, b, ka, kb, n_blocks, *, name, off_a=0, off_b=0, tm=1024, operand_dtype=None, sign=1.0):
    t_dim = a.shape[0]
    tm = _tile(t_dim, tm, 16)

    def body(a_ref, b_ref, o_ref):
        a, b = a_ref[...], b_ref[...]
        if operand_dtype is not None:
            a, b = a.astype(operand_dtype), b.astype(operand_dtype)
        p = lax.dot_general(a, b, (((0,), (0,)), ((), ())), preferred_element_type=F32)
        if sign != 1.0:
            p = p * sign

        @pl.when(pl.program_id(1) == 0)
        def _():
            o_ref[...] = p

        @pl.when(pl.program_id(1) > 0)
        def _():
            o_ref[...] += p

    return _pcall(
        body, name=name, grid=(n_blocks, t_dim // tm),
        in_specs=[pl.BlockSpec((tm, ka), lambda j, t: (t, j + off_a)),
                  pl.BlockSpec((tm, kb), lambda j, t: (t, j + off_b))],
        out_specs=[pl.BlockSpec((None, ka, kb), lambda j, t: (j, 0, 0))],
        out_shape=[jax.ShapeDtypeStruct((n_blocks, ka, kb), F32)],
        semantics=("parallel", "arbitrary"), operands=[a, b])[0][0]


def _rowk(fn, rows, bcast, outs, *, name, tm=256):
    rows = [r if isinstance(r, tuple) else (r, r.shape[1], 0) for r in rows]
    t_dim = rows[0][0].shape[0]
    tm = _tile(t_dim, tm, 16)
    n_rows, n_bc = len(rows), len(bcast)

    def body(*refs):
        ins = [r[...] for r in refs[:n_rows + n_bc]]
        vals = fn(*ins)
        first = pl.program_id(0) == 0
        for o_ref, v, spec in zip(refs[n_rows + n_bc:], vals, outs):
            if spec[0] == "row":
                o_ref[...] = v.astype(o_ref.dtype)
            else:
                @pl.when(first)
                def _(o_ref=o_ref, v=v):
                    o_ref[...] = v

                @pl.when(jnp.logical_not(first))
                def _(o_ref=o_ref, v=v):
                    o_ref[...] += v

    in_specs = [pl.BlockSpec((tm, w), lambda i, cb=cb: (i, cb)) for _, w, cb in rows]
    in_specs += [pl.BlockSpec((1, b.shape[1]), lambda i: (0, 0)) for b in bcast]
    out_specs, out_shape = [], []
    for spec in outs:
        if spec[0] == "row":
            out_specs.append(pl.BlockSpec((tm, spec[1]), lambda i: (i, 0)))
            out_shape.append(jax.ShapeDtypeStruct((t_dim, spec[1]), spec[2]))
        else:
            out_specs.append(pl.BlockSpec((1, spec[1]), lambda i: (0, 0)))
            out_shape.append(jax.ShapeDtypeStruct((1, spec[1]), F32))
    return _pcall(body, name=name, grid=(t_dim // tm,), in_specs=in_specs, out_specs=out_specs, out_shape=out_shape,
                  semantics=("arbitrary",), operands=[*[r[0] for r in rows], *bcast])[0]


def _colsum(v):
    return jnp.sum(v, axis=0, keepdims=True)


def _rms_fwd(x, g, *, name):
    def fn(x, g):
        r = lax.rsqrt(jnp.mean(x * x, axis=-1, keepdims=True) + NORM_EPS)
        return (x * r * g).astype(BF16), r
    d = x.shape[1]
    return _rowk(fn, [x], [g], [("row", d, BF16), ("row", 1, F32)], name=name)


def _rms_bwd_rows(dn, x, r, g):
    xh = x * r
    dy = dn * g
    dx = r * (dy - xh * jnp.mean(dy * xh, axis=-1, keepdims=True))
    return dx, _colsum(dn * xh)


def _silu_parts(g):
    sg = jax.nn.sigmoid(g)
    return g * sg, sg * (1.0 + g * (1.0 - sg))


def _ffn_fwd(h, norm_g, w_gate_t, w_up_t, get_w_down, tag, comm_gateup, comm_down):
    n, r = _rms_fwd(h, norm_g, name=f"{tag}_rms")

    def gate_up(accs, ex, bc):
        g, u = accs
        return g, u, _silu_parts(g)[0] * u

    (g, u, a), got1 = _matmul([(n, w_gate_t), (n, w_up_t)], [BF16, BF16, BF16], name=f"{tag}_gateup", tb=True,
                              separate=True, epilogue=gate_up, comm=comm_gateup)
    w_down = get_w_down(got1)
    out, got2 = _with_comm(_matmul([(a, w_down)], [F32], name=f"{tag}_down", extras=[h], tm=1024, tn=1024, tk=1024,
                                   epilogue=lambda accs, ex, bc: (ex[0] + 0.5 * accs[0],), comm=comm_down), comm_down)
    return out, (h, n, r, g, u, a), got1, got2


def _ffn_bwd_act(dyb, saved, w_down, tag, comm):
    g, u = saved[3], saved[4]

    def act_bwd(accs, ex, bc):
        da, g, u = accs[0], ex[0].astype(F32), ex[1].astype(F32)
        silu, dsilu = _silu_parts(g)
        return da * u * dsilu, da * silu

    return _matmul([(dyb, w_down)], [BF16, BF16], name=f"{tag}_dact", tb=True, extras=[g, u], epilogue=act_bwd, tm=1024,
                   comm=comm)


def _ffn_bwd_input(dh, dg, du, saved, norm_g, w_gate_t, w_up_t, tag, comm):
    h, n, r = saved[0], saved[1], saved[2]
    dn, got = _with_comm(_matmul([(dg, w_gate_t), (du, w_up_t)], [F32], name=f"{tag}_dn", tm=1024, tn=1024, tk=1024,
                                 comm=comm), comm)

    def fn(dh, dn, h, r, gain):
        dx, dgain = _rms_bwd_rows(dn, h, r, gain)
        dx = dh + dx
        return dx, (0.5 * dx).astype(BF16), dx.astype(BF16), dgain

    d = h.shape[1]
    dx, dx_half_b, dx_b, d_norm = _rowk(fn, [dh, dn, h, r], [norm_g],
                                        [("row", d, F32), ("row", d, BF16), ("row", d, BF16), ("sum", d)],
                                        name=f"{tag}_rms_bwd")
    return dx, dx_half_b, dx_b, d_norm, got


def _pool_window(z, seq, width, group_width, *, name, transpose):
    assert POOL_WINDOWS == (2, 4, 8, 16)
    t_dim = z.shape[0]
    tc = _tile(group_width, 256, V7X_LANES)
    per_group = group_width // tc

    def body(z_ref, o_ref):
        gid = pl.program_id(1) // per_group
        v = z_ref[...]
        t = lax.broadcasted_iota(jnp.int32, v.shape, 0)
        win = jnp.where(gid == 0, 2, jnp.where(gid == 1, 4, jnp.where(gid == 2, 8, 16)))
        cnt = jnp.minimum(t + 1, win).astype(F32)
        if transpose:
            e = v / cnt
            shift = lambda q, k: jnp.where(t < seq - k, pltpu.roll(q, seq - k, 0), 0.0)
        else:
            e = v
            shift = lambda q, k: jnp.where(t >= k, pltpu.roll(q, k, 0), 0.0)
        s1 = e + shift(e, 1)
        s2 = s1 + shift(s1, 2)
        s3 = s2 + shift(s2, 4)
        s4 = s3 + shift(s3, 8)
        s = jnp.where(gid == 0, s1, jnp.where(gid == 1, s2, jnp.where(gid == 2, s3, s4)))
        if transpose:
            o_ref[...] = (s - v).astype(o_ref.dtype)
        else:
            o_ref[...] = (s / cnt - v).astype(o_ref.dtype)

    spec = pl.BlockSpec((seq, tc), lambda b, c: (b, c))
    return _pcall(body, name=name, grid=(t_dim // seq, width // tc), in_specs=[spec], out_specs=[spec],
                  out_shape=[jax.ShapeDtypeStruct((t_dim, width), F32 if transpose else BF16)],
                  semantics=("parallel", "parallel"), operands=[z])[0][0]


def _discretize(lam_re, lam_im, log_dt, b_re_t, b_im_t):
    dt = jnp.exp(log_dt)
    mag = jnp.exp(lam_re * dt)
    a_re = mag * jnp.cos(lam_im * dt)
    a_im = mag * jnp.sin(lam_im * dt)
    den = lam_re * lam_re + lam_im * lam_im
    f_re = ((a_re - 1.0) * lam_re + a_im * lam_im) / den
    f_im = (a_im * lam_re - (a_re - 1.0) * lam_im) / den
    f_re, f_im = f_re[:, None, :], f_im[:, None, :]
    return a_re, a_im, f_re * b_re_t - f_im * b_im_t, f_re * b_im_t + f_im * b_re_t


def _discretize_fwd(params):
    shapes = [jax.ShapeDtypeStruct(params[0].shape, F32)] * 2 + [jax.ShapeDtypeStruct(params[3].shape, F32)] * 2

    def body(*refs):
        outs = _discretize(*[r[...] for r in refs[:5]])
        for o_ref, o in zip(refs[5:], outs):
            o_ref[...] = o

    return pl.pallas_call(body, name="s5_discretize", out_shape=shapes,
                          compiler_params=pltpu.CompilerParams(vmem_limit_bytes=V7X_VMEM_LIMIT_BYTES))(*params)


def _discretize_bwd(params, cots):
    shapes = [jax.ShapeDtypeStruct(p.shape, F32) for p in params]

    def body(*refs):
        _, vjp = jax.vjp(_discretize, *[r[...] for r in refs[:5]])
        grads = vjp(tuple(r[...] for r in refs[5:9]))
        for o_ref, o in zip(refs[9:], grads):
            o_ref[...] = o

    return pl.pallas_call(body, name="s5_discretize_bwd", out_shape=shapes,
                          compiler_params=pltpu.CompilerParams(vmem_limit_bytes=V7X_VMEM_LIMIT_BYTES))(*params, *cots)


def _scan_tiles(t_dim, n_dim, seq):
    return _tile(seq, 256, V7X_SUBLANES), _tile(n_dim, 1024, V7X_LANES)


def _scan_fwd(bu_re, bu_im, a_re, a_im, seq, comm=None):
    t_dim, n_dim = bu_re.shape
    tt, tl = _scan_tiles(t_dim, n_dim, seq)
    per_seq = seq // tt

    def body(br_ref, bi_ref, ar_ref, ai_ref, xr_ref, xi_ref, cr_ref, ci_ref):
        @pl.when(pl.program_id(1) % per_seq == 0)
        def _():
            cr_ref[...] = jnp.zeros_like(cr_ref)
            ci_ref[...] = jnp.zeros_like(ci_ref)

        ar, ai = ar_ref[...], ai_ref[...]

        def step(i, carry):
            xr, xi = carry
            row = pl.ds(i, 1)
            nr = ar * xr - ai * xi + br_ref[row, :]
            ni = ai * xr + ar * xi + bi_ref[row, :]
            xr_ref[row, :] = nr
            xi_ref[row, :] = ni
            return nr, ni

        xr, xi = lax.fori_loop(0, tt, step, (cr_ref[...], ci_ref[...]), unroll=8)
        cr_ref[...] = xr
        ci_ref[...] = xi

    blk = pl.BlockSpec((tt, tl), lambda l, t: (t, l))
    vec = pl.BlockSpec((1, tl), lambda l, t: (0, l))
    outs, c_outs = _pcall(
        body, name="s5_scan", grid=(n_dim // tl, t_dim // tt), in_specs=[blk, blk, vec, vec], out_specs=[blk, blk],
        out_shape=[jax.ShapeDtypeStruct((t_dim, n_dim), F32)] * 2, scratch_shapes=[pltpu.VMEM((1, tl), F32)] * 2,
        semantics=("parallel", "arbitrary"), operands=[bu_re, bu_im, a_re, a_im], comm=comm)
    return outs, c_outs


def _scan_bwd(gx_re, gx_im, x_re, x_im, a_re, a_im, seq, comm=None):
    t_dim, n_dim = gx_re.shape
    tt, tl = _scan_tiles(t_dim, n_dim, seq)
    per_seq = seq // tt
    n_t = t_dim // tt
    prev_rows = V7X_SUBLANES

    def body(gr_ref, gi_ref, xr_ref, xi_ref, pr_ref, pi_ref, ar_ref, ai_ref,
             lr_ref, li_ref, dar_ref, dai_ref, cr_ref, ci_ref):
        step_id = pl.program_id(1)
        blk_id = n_t - 1 - step_id

        @pl.when((blk_id + 1) % per_seq == 0)
        def _():
            cr_ref[...] = jnp.zeros_like(cr_ref)
            ci_ref[...] = jnp.zeros_like(ci_ref)

        ar, ai = ar_ref[...], ai_ref[...]

        def step(j, carry):
            lr, li = carry
            row = pl.ds(tt - 1 - j, 1)
            nr = gr_ref[row, :] + ar * lr + ai * li
            ni = gi_ref[row, :] - ai * lr + ar * li
            lr_ref[row, :] = nr
            li_ref[row, :] = ni
            return nr, ni

        lr, li = lax.fori_loop(0, tt, step, (cr_ref[...], ci_ref[...]), unroll=8)
        cr_ref[...] = lr
        ci_ref[...] = li

        first_of_seq = blk_id % per_seq == 0
        keep = jnp.where(first_of_seq, 0.0, 1.0)
        t = lax.broadcasted_iota(jnp.int32, (tt, tl), 0)
        xr_prev = jnp.where(t == 0, pr_ref[prev_rows - 1:prev_rows, :] * keep, pltpu.roll(xr_ref[...], 1, 0))
        xi_prev = jnp.where(t == 0, pi_ref[prev_rows - 1:prev_rows, :] * keep, pltpu.roll(xi_ref[...], 1, 0))
        lam_r, lam_i = lr_ref[...], li_ref[...]
        d_re = _colsum(lam_r * xr_prev + lam_i * xi_prev)
        d_im = _colsum(lam_i * xr_prev - lam_r * xi_prev)

        @pl.when(step_id == 0)
        def _():
            dar_ref[...] = d_re
            dai_ref[...] = d_im

        @pl.when(step_id > 0)
        def _():
            dar_ref[...] += d_re
            dai_ref[...] += d_im

    blk = pl.BlockSpec((tt, tl), lambda l, t: (n_t - 1 - t, l))
    prev = pl.BlockSpec((prev_rows, tl), lambda l, t: (jnp.maximum((n_t - 1 - t) * (tt // prev_rows) - 1, 0), l))
    vec = pl.BlockSpec((1, tl), lambda l, t: (0, l))
    return _pcall(
        body, name="s5_scan_bwd", grid=(n_dim // tl, n_t),
        in_specs=[blk, blk, blk, blk, prev, prev, vec, vec], out_specs=[blk, blk, vec, vec],
        out_shape=[jax.ShapeDtypeStruct((t_dim, n_dim), F32)] * 2 + [jax.ShapeDtypeStruct((1, n_dim), F32)] * 2,
        scratch_shapes=[pltpu.VMEM((1, tl), F32)] * 2, semantics=("parallel", "arbitrary"),
        operands=[gx_re, gx_im, x_re, x_im, x_re, x_im, a_re, a_im], comm=comm)


def _block_diag(w, per_block):
    g, ka, kb = w.shape
    eye = jnp.eye(per_block, dtype=w.dtype)
    out = jnp.einsum("jakb,ac->jakcb", w.reshape(g // per_block, per_block, ka, kb), eye)
    return out.reshape(g // per_block, per_block * ka, per_block * kb)


def _block_diag_t(d, per_block, ka, kb):
    j = d.shape[0]
    eye = jnp.eye(per_block, dtype=d.dtype)
    picked = jnp.einsum("jakcb,ac->jakb", d.reshape(j, per_block, ka, per_block, kb), eye)
    return picked.reshape(j * per_block, ka, kb)


def _gelu_parts(y):
    inner = GELU_C * (y + GELU_A * y * y * y)
    th = jnp.tanh(inner)
    val = 0.5 * y * (1.0 + th)
    grad = 0.5 * (1.0 + th) + 0.5 * y * (1.0 - th * th) * GELU_C * (1.0 + 3.0 * GELU_A * y * y)
    return val, grad


def _chip_sum(grad, got, core, *, name):
    r, c = got.shape[0] // N_CHIP, got.shape[1]
    tr = _tile(r, 704, 16)
    tc = _tile(c, 2048, V7X_LANES)
    per = r // tr

    def body(core_ref, g_ref, s_ref, o_ref):
        o_ref[...] = (g_ref[...].astype(F32) + s_ref[...].astype(F32)).astype(o_ref.dtype)

    slot = pl.BlockSpec((tr, tc), lambda q, i, j, core: (q * per + i, j))
    grid_spec = pltpu.PrefetchScalarGridSpec(
        num_scalar_prefetch=1, grid=(N_CHIP, per, c // tc),
        in_specs=[pl.BlockSpec((tr, tc), lambda q, i, j, core: ((2 * q + core[0]) * per + i, j)), slot],
        out_specs=slot)
    return pl.pallas_call(
        body, name=name, grid_spec=grid_spec, out_shape=jax.ShapeDtypeStruct(got.shape, got.dtype),
        compiler_params=pltpu.CompilerParams(dimension_semantics=("parallel",) * 3,
                                             vmem_limit_bytes=V7X_VMEM_LIMIT_BYTES))(core, grad, got)


def _adamw_math(w, g, m, v):
    nm = ADAM_B1 * m + (1.0 - ADAM_B1) * g
    nv = ADAM_B2 * v + (1.0 - ADAM_B2) * (g * g)
    m_hat = nm * (1.0 / (1.0 - ADAM_B1 ** ADAM_STEP))
    v_hat = nv * (1.0 / (1.0 - ADAM_B2 ** ADAM_STEP))
    return -ADAM_LR * (m_hat / (jnp.sqrt(v_hat) + ADAM_EPS) + ADAM_WD * w), nm, nv


def _sum_adamw(parts, others, chip, w, m, v, *, name):
    r, c = w.shape
    n_others = others.shape[0] // r
    tr = _tile(r, 704, 16)
    tc = _tile(c, 512, V7X_LANES)
    per = r // tr

    def body(chip_ref, p_ref, *refs):
        o_refs, (w_ref, m_ref, v_ref), (g_ref, d_ref, nm_ref, nv_ref) = refs[:n_others], refs[n_others:n_others + 3], refs[n_others + 3:]
        g = p_ref[...].astype(F32)
        for o_ref in o_refs:
            g = g + o_ref[...].astype(F32)
        g_ref[...] = g
        d_ref[...], nm_ref[...], nv_ref[...] = _adamw_math(w_ref[...], g, m_ref[...], v_ref[...])

    own = pl.BlockSpec((tr, tc), lambda i, j, chip: (i, j))
    grid_spec = pltpu.PrefetchScalarGridSpec(
        num_scalar_prefetch=1, grid=(per, c // tc),
        in_specs=[pl.BlockSpec((tr, tc), lambda i, j, chip: (chip[0] * per + i, j))]
        + [pl.BlockSpec((tr, tc), lambda i, j, chip, s=s: (s * per + i, j)) for s in range(n_others)] + [own] * 3,
        out_specs=[own] * 4)
    return pl.pallas_call(
        body, name=name, grid_spec=grid_spec, out_shape=[jax.ShapeDtypeStruct((r, c), F32)] * 4,
        compiler_params=pltpu.CompilerParams(dimension_semantics=("parallel",) * 2,
                                             vmem_limit_bytes=V7X_VMEM_LIMIT_BYTES))(chip, parts, *[others] * n_others, w, m, v)


def _sum_parts(parts, *, name):
    n_parts, r, c = parts.shape
    tr = _tile(r, 704, 16)
    tc = _tile(c, 1024, V7X_LANES)

    def body(p_ref, o_ref):
        acc = p_ref[0].astype(F32)
        for s in range(1, n_parts):
            acc = acc + p_ref[s].astype(F32)
        o_ref[...] = acc

    return _pcall(body, name=name, grid=(r // tr, c // tc),
                  in_specs=[pl.BlockSpec((n_parts, tr, tc), lambda i, j: (0, i, j))],
                  out_specs=[pl.BlockSpec((tr, tc), lambda i, j: (i, j))],
                  out_shape=[jax.ShapeDtypeStruct((r, c), F32)], semantics=("parallel", "parallel"),
                  operands=[parts])[0][0]


def _adamw(w, g, m, v, *, name):
    r, c = w.shape
    tr = _tile(r, max(V7X_SUBLANES, ADAMW_BLOCK_BYTES // (4 * c)), V7X_SUBLANES)

    def body(w_ref, g_ref, m_ref, v_ref, d_ref, nm_ref, nv_ref):
        d_ref[...], nm_ref[...], nv_ref[...] = _adamw_math(w_ref[...], g_ref[...], m_ref[...], v_ref[...])

    spec = pl.BlockSpec((tr, c), lambda i: (i, 0))
    return _pcall(body, name=name, grid=(r // tr,), in_specs=[spec] * 4, out_specs=[spec] * 3,
                  out_shape=[jax.ShapeDtypeStruct((r, c), F32)] * 3, semantics=("parallel",), operands=[w, g, m, v])[0]


def _pack(arrays, width=V7X_LANES):
    tile = V7X_SUBLANES * width
    parts, layout, row = [], [], 0
    for a in arrays:
        n = a.size
        rows = -(-n // tile) * V7X_SUBLANES
        flat = jnp.pad(a.reshape(-1).astype(F32), (0, rows * width - n))
        parts.append(flat.reshape(rows, width))
        layout.append((row, rows, n, a.shape))
        row += rows
    return jnp.concatenate(parts, axis=0), layout


def _unpack(packed, layout):
    return [packed[row:row + rows].reshape(-1)[:n].reshape(shape) for row, rows, n, shape in layout]


def kernel(x, ffn1_norm, ffn1_gate, ffn1_up, ffn1_down, mix_norm, w_in, w_pool, pool_scale, lam_re, lam_im, log_dt, b_re, b_im, c_re, c_im, d_skip, w_glu, b_glu, pool_out_norm, ssm_out_norm, w_out, ffn2_norm, ffn2_gate, ffn2_up, ffn2_down, final_norm, loss_target, m_ffn1_norm, m_ffn1_gate, m_ffn1_up, m_ffn1_down, m_mix_norm, m_w_in, m_w_pool, m_pool_scale, m_lam_re, m_lam_im, m_log_dt, m_b_re, m_b_im, m_c_re, m_c_im, m_d_skip, m_w_glu, m_b_glu, m_pool_out_norm, m_ssm_out_norm, m_w_out, m_ffn2_norm, m_ffn2_gate, m_ffn2_up, m_ffn2_down, m_final_norm, v_ffn1_norm, v_ffn1_gate, v_ffn1_up, v_ffn1_down, v_mix_norm, v_w_in, v_w_pool, v_pool_scale, v_lam_re, v_lam_im, v_log_dt, v_b_re, v_b_im, v_c_re, v_c_im, v_d_skip, v_w_glu, v_b_glu, v_pool_out_norm, v_ssm_out_norm, v_w_out, v_ffn2_norm, v_ffn2_gate, v_ffn2_up, v_ffn2_down, v_final_norm):
    weights = dict(ffn1_norm=ffn1_norm, ffn1_gate=ffn1_gate, ffn1_up=ffn1_up, ffn1_down=ffn1_down, mix_norm=mix_norm, w_in=w_in, w_pool=w_pool, pool_scale=pool_scale, lam_re=lam_re, lam_im=lam_im, log_dt=log_dt, b_re=b_re, b_im=b_im, c_re=c_re, c_im=c_im, d_skip=d_skip, w_glu=w_glu, b_glu=b_glu, pool_out_norm=pool_out_norm, ssm_out_norm=ssm_out_norm, w_out=w_out, ffn2_norm=ffn2_norm, ffn2_gate=ffn2_gate, ffn2_up=ffn2_up, ffn2_down=ffn2_down, final_norm=final_norm)
    moments_m = dict(ffn1_norm=m_ffn1_norm, ffn1_gate=m_ffn1_gate, ffn1_up=m_ffn1_up, ffn1_down=m_ffn1_down, mix_norm=m_mix_norm, w_in=m_w_in, w_pool=m_w_pool, pool_scale=m_pool_scale, lam_re=m_lam_re, lam_im=m_lam_im, log_dt=m_log_dt, b_re=m_b_re, b_im=m_b_im, c_re=m_c_re, c_im=m_c_im, d_skip=m_d_skip, w_glu=m_w_glu, b_glu=m_b_glu, pool_out_norm=m_pool_out_norm, ssm_out_norm=m_ssm_out_norm, w_out=m_w_out, ffn2_norm=m_ffn2_norm, ffn2_gate=m_ffn2_gate, ffn2_up=m_ffn2_up, ffn2_down=m_ffn2_down, final_norm=m_final_norm)
    moments_v = dict(ffn1_norm=v_ffn1_norm, ffn1_gate=v_ffn1_gate, ffn1_up=v_ffn1_up, ffn1_down=v_ffn1_down, mix_norm=v_mix_norm, w_in=v_w_in, w_pool=v_w_pool, pool_scale=v_pool_scale, lam_re=v_lam_re, lam_im=v_lam_im, log_dt=v_log_dt, b_re=v_b_re, b_im=v_b_im, c_re=v_c_re, c_im=v_c_im, d_skip=v_d_skip, w_glu=v_w_glu, b_glu=v_b_glu, pool_out_norm=v_pool_out_norm, ssm_out_norm=v_ssm_out_norm, w_out=v_w_out, ffn2_norm=v_ffn2_norm, ffn2_gate=v_ffn2_gate, ffn2_up=v_ffn2_up, ffn2_down=v_ffn2_down, final_norm=v_final_norm)
    names = list(weights)

    n_seq, seq, d_model = x.shape
    t_dim = n_seq * seq
    ff_shard = ffn1_gate.shape[1]
    ff = N_DEV * ff_shard
    ff_pad = -(-ff // FF_PAD_MULTIPLE) * FF_PAD_MULTIPLE - ff
    n_pool, pool_gw = w_pool.shape[0], w_pool.shape[2]
    pool_w = n_pool * pool_gw
    pool_rows = pool_gw // N_DEV
    n_grp, n_state, n_ch = b_re.shape
    ssm_w = n_grp * n_ch
    grp_per_blk = V7X_LANES // n_ch
    n_blk = n_grp // grp_per_blk
    ch_blk, st_blk = grp_per_blk * n_ch, grp_per_blk * n_state
    ssm_off = pool_w // ch_blk

    x2 = x.reshape(t_dim, d_model)
    tgt2 = loss_target.reshape(t_dim, d_model)
    row = lambda p: p.reshape(1, -1)

    sharded = ["ffn1_gate", "ffn1_up", "ffn1_down", "ffn2_gate", "ffn2_up", "ffn2_down", "w_in", "w_out", "w_glu", "w_pool"]
    transposed = {"ffn1_gate", "ffn1_up", "ffn2_gate", "ffn2_up"}
    shard = {n: (weights[n].T if n in transposed else weights[n]).astype(BF16) for n in sharded}

    def gather(*group):
        return _Gather([shard[n] for n in group], [ff_pad if n.startswith("ffn") else 0 for n in group])

    full = {}
    full["ffn1_gate"], full["ffn1_up"] = _comm_only(gather("ffn1_gate", "ffn1_up"), name="gather_first")

    def ffn1_down_weights(got):
        full["ffn1_down"], full["w_in"] = got
        return full["ffn1_down"]

    h1, saved1, _, got = _ffn_fwd(x2, row(ffn1_norm), full["ffn1_gate"], full["ffn1_up"], ffn1_down_weights, "ffn1",
                                  gather("ffn1_down", "w_in"), gather("ffn2_gate"))
    (full["ffn2_gate"],) = got
    n2, r2 = _rms_fwd(h1, row(mix_norm), name="mix_rms")
    z, (full["w_glu"], full["w_pool"]) = _matmul([(n2, full["w_in"])], [F32], name="mix_in", comm=gather("w_glu", "w_pool"))

    d_pool = _pool_window(z, seq, pool_w, pool_gw, name="pool_window", transpose=False)
    y_pool_lin = _bd([(d_pool, full["w_pool"])], F32, name="pool_mix")

    disc_params = (lam_re, lam_im, log_dt.reshape(n_grp, 1), jnp.swapaxes(b_re, 1, 2), jnp.swapaxes(b_im, 1, 2))
    a_re, a_im, bb_re_t, bb_im_t = _discretize_fwd(disc_params)
    a_re_row, a_im_row = row(a_re), row(a_im)
    wb_re, wb_im = _block_diag(bb_re_t, grp_per_blk), _block_diag(bb_im_t, grp_per_blk)
    wc_re = _block_diag(jnp.swapaxes(c_re, 1, 2), grp_per_blk)
    wc_im = _block_diag(jnp.swapaxes(c_im, 1, 2), grp_per_blk)
    bu_re = _bd([(z, wb_re)], F32, name="s5_bu_re", offs=[ssm_off], operand_dtype=BF16)
    bu_im = _bd([(z, wb_im)], F32, name="s5_bu_im", offs=[ssm_off], operand_dtype=BF16)
    (xs_re, xs_im), (full["ffn2_up"],) = _scan_fwd(bu_re, bu_im, a_re_row, a_im_row, seq, comm=gather("ffn2_up"))
    y_lin, (full["w_out"],) = _bd([(xs_re, wc_re), (xs_im, -wc_im)], F32, name="s5_cx", operand_dtype=BF16,
                                  comm=gather("w_out"))

    def s5_post(y_lin, u, skip):
        y = y_lin + skip * u
        return y, _gelu_parts(y)[0].astype(BF16)

    u_cols = (z, ssm_w, pool_w // ssm_w)
    y_ssm, yg = _rowk(s5_post, [y_lin, u_cols], [row(d_skip)], [("row", ssm_w, F32), ("row", ssm_w, BF16)], name="s5_post")

    def glu(accs, ex, bc):
        q = accs[0] + bc[0]
        return q, ex[0].astype(F32) * jax.nn.sigmoid(q)

    q_glu, y_s5 = _matmul([(yg, full["w_glu"])], [F32, F32], name="s5_glu", extras=[yg], bcast=[row(b_glu)], epilogue=glu)

    def merge(yp_lin, ys, scale, gp, gs):
        yp = yp_lin * scale
        rp = lax.rsqrt(jnp.mean(yp * yp, axis=-1, keepdims=True) + NORM_EPS)
        rs = lax.rsqrt(jnp.mean(ys * ys, axis=-1, keepdims=True) + NORM_EPS)
        merged = jnp.concatenate([yp * rp * gp, ys * rs * gs], axis=-1)
        return merged.astype(BF16), rp, rs

    merged, r_pool, r_ssm = _rowk(merge, [y_pool_lin, y_s5], [row(pool_scale), row(pool_out_norm), row(ssm_out_norm)],
                                  [("row", pool_w + ssm_w, BF16), ("row", 1, F32), ("row", 1, F32)], name="mix_merge")
    h2 = _matmul([(merged, full["w_out"])], [F32], name="mix_out", extras=[h1],
                 epilogue=lambda accs, ex, bc: (ex[0] + accs[0],))

    def ffn2_down_weights(got):
        (full["ffn2_down"],) = got
        return full["ffn2_down"]

    h3, saved2, _, _ = _ffn_fwd(h2, row(ffn2_norm), full["ffn2_gate"], full["ffn2_up"], ffn2_down_weights, "ffn2",
                                gather("ffn2_down"), None)

    def head(h, tgt, gain):
        r = lax.rsqrt(jnp.mean(h * h, axis=-1, keepdims=True) + NORM_EPS)
        xh = h * r
        err = xh * gain - tgt
        loss = jnp.sum(0.5 * jnp.mean(err * err, axis=-1, keepdims=True), axis=0, keepdims=True)
        dout = err * (1.0 / d_model)
        dy = dout * gain
        dh = r * (dy - xh * jnp.mean(dy * xh, axis=-1, keepdims=True))
        return dh, (0.5 * dh).astype(BF16), _colsum(dout * xh), jnp.broadcast_to(loss, (1, V7X_LANES))

    dh3, dyb2, g_final, loss_part = _rowk(head, [h3, tgt2], [row(final_norm)],
                                          [("row", d_model, F32), ("row", d_model, BF16), ("sum", d_model), ("sum", V7X_LANES)],
                                          name="loss_head")

    grads = {}
    shard_rows = {n: (n_pool * pool_rows if n == "w_pool" else shard[n].shape[0]) for n in sharded}

    def to_sibling(group, name):
        got = _comm_only(_SiblingSwap([grads[n] for n in group], [shard_rows[n] for n in group]), name=name)
        for n, g in zip(group, got):
            chip_parts[n] = _chip_sum(grads[n], g, my_core, name=f"chip_sum_{n}")
        return [chip_parts[n] for n in group]

    my_core = lax.axis_index("c").astype(jnp.int32).reshape(1)
    my_chip = (2 * lax.axis_index("x") + lax.axis_index("y")).astype(jnp.int32).reshape(1)
    chip_parts, from_chips = {}, {}

    def to_chips(group, parts):
        return _ChipScatter(parts), group

    def take(group, got):
        for n, g in zip(group, got):
            from_chips[n] = g

    def ffn_backward(tag, dh, dyb, saved, norm_g, first, last=None):
        gate, up, down = f"{tag}_gate", f"{tag}_up", f"{tag}_down"
        n, a = saved[1], saved[5]
        plan, group = first
        grads[down], got = _with_comm(_matmul([(a, dyb)], [BF16], name=f"{tag}_dwdown", ta=True, tn=1024, comm=plan), plan)
        take(group, got)
        plan, group = to_chips([down], to_sibling([down], f"grads_to_sibling_{down}"))
        (dg, du), got = _ffn_bwd_act(dyb, saved, full[down], tag, plan)
        take(group, got)
        grads[gate] = _matmul([(dg, n)], [BF16], name=f"{tag}_dwgate", ta=True, tn=1024)
        plan, group = to_chips([gate], to_sibling([gate], f"grads_to_sibling_{gate}"))
        grads[up], got = _matmul([(du, n)], [BF16], name=f"{tag}_dwup", ta=True, tn=1024, comm=plan)
        take(group, got)
        plan, group = to_chips([up], to_sibling([up], f"grads_to_sibling_{up}"))
        if last is not None:
            plan = _Both(plan, last)
        dx, _, dx_b, grads[f"{tag}_norm"], got = _ffn_bwd_input(dh, dg, du, saved, norm_g, full[gate], full[up], tag, plan)
        take(group, got)
        return dx, dx_b, got[len(group):]

    dh2, dh2b, _ = ffn_backward("ffn2", dh3, dyb2, saved2, row(ffn2_norm), (None, []))

    d_merged = _matmul([(dh2b, full["w_out"])], [F32], name="mix_out_dx", tb=True)
    grads["w_out"] = _matmul([(merged, dh2b)], [BF16], name="mix_out_dw", ta=True)

    def merge_bwd(dm, yp_lin, ys, rp, rs, scale, gp, gs):
        yp = yp_lin * scale
        d_yp, d_gp = _rms_bwd_rows(dm[:, :pool_w], yp, rp, gp)
        d_ys, d_gs = _rms_bwd_rows(dm[:, pool_w:], ys, rs, gs)
        return (d_yp * scale).astype(BF16), d_ys, _colsum(d_yp * yp_lin), d_gp, d_gs

    d_pool_lin, d_ys, grads["pool_scale"], grads["pool_out_norm"], grads["ssm_out_norm"] = _rowk(
        merge_bwd, [d_merged, y_pool_lin, y_s5, r_pool, r_ssm], [row(pool_scale), row(pool_out_norm), row(ssm_out_norm)],
        [("row", pool_w, BF16), ("row", ssm_w, F32), ("sum", pool_w), ("sum", pool_w), ("sum", ssm_w)], name="mix_merge_bwd")

    w_pool_t = jnp.swapaxes(full["w_pool"], 1, 2)
    dd_pool = _bd([(d_pool_lin, w_pool_t)], F32, name="pool_mix_dx")
    g_pool = _bd_wgrad(d_pool, d_pool_lin, pool_gw, pool_gw, n_pool, name="pool_mix_dw")
    grads["w_pool"] = g_pool.reshape(n_pool, N_DEV, pool_rows, pool_gw).transpose(1, 0, 2, 3).reshape(
        N_DEV * n_pool * pool_rows, pool_gw).astype(BF16)
    dz_pool = _pool_window(dd_pool, seq, pool_w, pool_gw, name="pool_window_bwd", transpose=True)

    def glu_bwd(d_ys, yg, q):
        sg = jax.nn.sigmoid(q)
        dq = d_ys * yg.astype(F32) * sg * (1.0 - sg)
        return dq.astype(BF16), d_ys * sg, _colsum(dq)

    dq, d_yg_direct, grads["b_glu"] = _rowk(glu_bwd, [d_ys, yg, q_glu], [],
                                            [("row", ssm_w, BF16), ("row", ssm_w, F32), ("sum", ssm_w)], name="s5_glu_bwd")
    d_yg_mm = _matmul([(dq, full["w_glu"])], [F32], name="s5_glu_dx", tb=True)
    grads["w_glu"] = _matmul([(yg, dq)], [BF16], name="s5_glu_dw", ta=True)

    def gelu_bwd(d1, d2, y, u, skip):
        dy = (d1 + d2) * _gelu_parts(y)[1]
        return dy, dy * skip, _colsum(dy * u)

    dy_ssm, du_skip, grads["d_skip"] = _rowk(gelu_bwd, [d_yg_direct, d_yg_mm, y_ssm, u_cols], [row(d_skip)],
                                             [("row", ssm_w, F32), ("row", ssm_w, F32), ("sum", ssm_w)], name="s5_gelu_bwd")
    wc_re_t, wc_im_t = jnp.swapaxes(wc_re, 1, 2), jnp.swapaxes(wc_im, 1, 2)
    gx_re = _bd([(dy_ssm, wc_re_t)], F32, name="s5_gx_re", operand_dtype=BF16)
    gx_im = _bd([(dy_ssm, -wc_im_t)], F32, name="s5_gx_im", operand_dtype=BF16)
    d_wc_re = _bd_wgrad(xs_re, dy_ssm, st_blk, ch_blk, n_blk, name="s5_dc_re", operand_dtype=BF16)
    d_wc_im = _bd_wgrad(xs_im, dy_ssm, st_blk, ch_blk, n_blk, name="s5_dc_im", operand_dtype=BF16, sign=-1.0)
    (lm_re, lm_im, da_re, da_im), _ = _scan_bwd(gx_re, gx_im, xs_re, xs_im, a_re_row, a_im_row, seq)
    d_wb_re = _bd_wgrad(z, lm_re, ch_blk, st_blk, n_blk, name="s5_db_re", off_a=ssm_off, operand_dtype=BF16)
    d_wb_im = _bd_wgrad(z, lm_im, ch_blk, st_blk, n_blk, name="s5_db_im", off_a=ssm_off, operand_dtype=BF16)
    wb_re_t, wb_im_t = jnp.swapaxes(wb_re, 1, 2), jnp.swapaxes(wb_im, 1, 2)
    du_lin = _bd([(lm_re, wb_re_t), (lm_im, wb_im_t)], F32, name="s5_du", operand_dtype=BF16)
    grads["c_re"] = jnp.swapaxes(_block_diag_t(d_wc_re, grp_per_blk, n_state, n_ch), 1, 2)
    grads["c_im"] = jnp.swapaxes(_block_diag_t(d_wc_im, grp_per_blk, n_state, n_ch), 1, 2)
    d_bb_re_t = _block_diag_t(d_wb_re, grp_per_blk, n_ch, n_state)
    d_bb_im_t = _block_diag_t(d_wb_im, grp_per_blk, n_ch, n_state)
    g_lam_re, g_lam_im, g_log_dt, g_b_re_t, g_b_im_t = _discretize_bwd(
        disc_params, (da_re.reshape(n_grp, n_state), da_im.reshape(n_grp, n_state), d_bb_re_t, d_bb_im_t))
    grads["lam_re"], grads["lam_im"], grads["log_dt"] = g_lam_re, g_lam_im, g_log_dt.reshape(n_grp)
    grads["b_re"], grads["b_im"] = jnp.swapaxes(g_b_re_t, 1, 2), jnp.swapaxes(g_b_im_t, 1, 2)

    def join(dzp, du1, du2):
        return (jnp.concatenate([dzp, du1 + du2], axis=-1).astype(BF16),)

    (dz,) = _rowk(join, [dz_pool, du_lin, du_skip], [], [("row", pool_w + ssm_w, BF16)], name="mix_in_join")
    dn2 = _matmul([(dz, full["w_in"])], [F32], name="mix_in_dx", tb=True)
    grads["w_in"] = _matmul([(n2, dz)], [BF16], name="mix_in_dw", ta=True)
    mixer = ["w_out", "w_glu", "w_pool", "w_in"]
    parts_mixer = to_sibling(mixer, "grads_to_sibling_mixer")

    def mix_rms_bwd(dh, dn, h, r, gain):
        dx, dgain = _rms_bwd_rows(dn, h, r, gain)
        dx = dh + dx
        return dx, (0.5 * dx).astype(BF16), dgain

    dh1, dyb1, grads["mix_norm"] = _rowk(mix_rms_bwd, [dh2, dn2, h1, r2], [row(mix_norm)],
                                         [("row", d_model, F32), ("row", d_model, BF16), ("sum", d_model)], name="mix_rms_bwd")

    grads["final_norm"] = g_final
    late = ["ffn1_norm"]
    early = [n for n in names if n not in sharded and n not in late]
    packed_early, layout = _pack([grads[n].reshape(weights[n].shape) for n in early] + [loss_part])
    dx, _, (gathered_early,) = ffn_backward("ffn1", dh1, dyb1, saved1, row(ffn1_norm), to_chips(mixer, parts_mixer),
                                            last=_Broadcast([packed_early]))
    packed_late, layout_late = _pack([grads[n].reshape(weights[n].shape) for n in late])
    (gathered_late,) = _comm_only(_Broadcast([packed_late]), name="small_all_gather_last")
    layout = layout + [(packed_early.shape[0] + r0, rows, n, shape) for r0, rows, n, shape in layout_late]
    small_sum = _sum_parts(jnp.concatenate([gathered_early, gathered_late], axis=1), name="sum_small")
    unpacked = _unpack(small_sum, layout)
    summed = dict(zip(early, unpacked[:len(early)]))
    summed.update(zip(late, unpacked[len(early) + 1:]))
    loss = unpacked[len(early)][0, 0]

    def packed(values, loss_slot):
        return _pack([values[n] for n in early] + [loss_slot] + [values[n] for n in late])[0]

    small_out = [_unpack(o, layout) for o in _adamw(
        packed(weights, jnp.zeros_like(loss_part)), small_sum, packed(moments_m, jnp.zeros_like(loss_part)),
        packed(moments_v, jnp.ones_like(loss_part)), name="adamw_small")]
    delta, new_m, new_v = {}, {}, {}
    for i, n in enumerate(early + [None] + late):
        if n is not None:
            delta[n], new_m[n], new_v[n] = small_out[0][i], small_out[1][i], small_out[2][i]
    for n in sharded:
        shape = weights[n].shape
        if n in transposed:
            to2d, back = (lambda a: a.T), (lambda a: a.T)
        else:
            to2d, back = (lambda a: a.reshape(-1, shape[-1])), (lambda a: a.reshape(shape))
        g, d, nm, nv = _sum_adamw(chip_parts[n], from_chips[n], my_chip, to2d(weights[n]), to2d(moments_m[n]),
                                  to2d(moments_v[n]), name=f"adamw_{n}")
        summed[n], delta[n], new_m[n], new_v[n] = back(g), back(d), back(nm), back(nv)

    return (loss, dx.reshape(x.shape), *[summed[n] for n in names], *[delta[n] for n in names],
            *[new_m[n] for n in names], *[new_v[n] for n in names])
```

```python
import functools
import math

import jax
import jax.numpy as jnp
from jax import lax
from jax.experimental import pallas as pl
from jax.experimental.pallas import tpu as pltpu

F32 = jnp.float32
BF16 = jnp.bfloat16
MESH = pl.DeviceIdType.MESH
ANY = pl.BlockSpec(memory_space=pl.ANY)

N_DEV = 8
N_CHIP = 4
V7X_LANES = 128
V7X_SUBLANES = 8
V7X_VMEM_LIMIT_BYTES = 56 * 1024 * 1024
FF_PAD_MULTIPLE = 1024
ADAMW_BLOCK_BYTES = 1 << 20

NORM_EPS = 1e-6
POOL_WINDOWS = (2, 4, 8, 16)
ADAM_LR, ADAM_B1, ADAM_B2, ADAM_EPS, ADAM_WD, ADAM_STEP = 0.001, 0.9, 0.999, 1e-08, 0.01, 10
GELU_C = math.sqrt(2.0 / math.pi)
GELU_A = 0.044715


def _tile(dim, pref, mult):
    t = min(pref, dim)
    t -= t % mult
    while t >= mult:
        if dim % t == 0:
            return t
        t -= mult
    return dim


def _pcall(body, *, name, grid, in_specs, out_specs, out_shape, operands, scratch_shapes=(), semantics=None, comm=None):
    if comm is None:
        params = pltpu.CompilerParams(dimension_semantics=semantics, vmem_limit_bytes=V7X_VMEM_LIMIT_BYTES)
        outs = pl.pallas_call(body, name=name, grid=grid, in_specs=list(in_specs), out_specs=list(out_specs),
                              out_shape=list(out_shape), scratch_shapes=list(scratch_shapes),
                              compiler_params=params)(*operands)
        return list(outs), []
    sizes = [len(in_specs), len(comm.inputs), len(out_shape), len(comm.out_shape), len(scratch_shapes),
             len(comm.sem_shapes)]

    def wrapped(*refs):
        groups, pos = [], 0
        for n in sizes:
            groups.append(refs[pos:pos + n])
            pos += n
        ins, c_ins, outs, c_outs, scratch, sems = groups
        if not grid:
            comm.start(c_ins, c_outs, sems)
            if body is not None:
                body(*ins, *outs, *scratch)
            comm.finish(c_ins, c_outs, sems)
            return
        ids = [pl.program_id(a) for a in range(len(grid))]
        first = functools.reduce(jnp.logical_and, [i == 0 for i in ids])
        last = functools.reduce(jnp.logical_and, [i == g - 1 for i, g in zip(ids, grid)])

        @pl.when(first)
        def _():
            comm.start(c_ins, c_outs, sems)

        body(*ins, *outs, *scratch)

        @pl.when(last)
        def _():
            comm.finish(c_ins, c_outs, sems)

    params = pltpu.CompilerParams(dimension_semantics=("arbitrary",) * len(grid), vmem_limit_bytes=V7X_VMEM_LIMIT_BYTES)
    res = pl.pallas_call(wrapped, name=name, grid=grid, in_specs=list(in_specs) + [ANY] * sizes[1],
                         out_specs=list(out_specs) + [ANY] * sizes[3], out_shape=list(out_shape) + list(comm.out_shape),
                         scratch_shapes=list(scratch_shapes) + list(comm.sem_shapes),
                         compiler_params=params)(*operands, *comm.inputs)
    return list(res[:sizes[2]]), list(res[sizes[2]:])


def _my_place():
    return lax.axis_index("x"), lax.axis_index("y"), lax.axis_index("c")


def _block_index(px, py, pc):
    return 4 * px + 2 * py + pc


class _Gather:
    def __init__(self, shards, pads):
        self.n, self.pads = len(shards), list(pads)
        zero_blocks = [jnp.zeros((p, s.shape[1]), s.dtype) for s, p in zip(shards, pads) if p]
        self.inputs = list(shards) + zero_blocks

        def full_shape(s, pad):
            if s.ndim == 2:
                return (N_DEV * s.shape[0] + pad, s.shape[1])
            return (s.shape[0], N_DEV * s.shape[1], s.shape[2])

        self.out_shape = [jax.ShapeDtypeStruct(full_shape(s, p), s.dtype) for s, p in zip(shards, pads)]
        self.sem_shapes = [pltpu.SemaphoreType.DMA((7 * self.n,)), pltpu.SemaphoreType.DMA((7 * self.n,)),
                           pltpu.SemaphoreType.DMA((self.n + len(zero_blocks),))]

    def _copies(self, ins, outs, sems):
        n = self.n
        send_sems, recv_sems, local_sems = sems
        x, y, c = _my_place()
        me, sibling = (x, y, c), (x, y, 1 - c)
        chips = [(1 - x, y), (x, 1 - y), (1 - x, 1 - y)]

        def rows(i, place):
            idx = _block_index(*place)
            r = ins[i].shape[-2]
            if ins[i].ndim == 2:
                return outs[i].at[pl.ds(idx * r, r), :]
            return outs[i].at[:, pl.ds(idx * r, r), :]

        def copy(i, k, block, to, src=None):
            return pltpu.make_async_remote_copy(
                src_ref=rows(i, block) if src is None else src, dst_ref=rows(i, block),
                send_sem=send_sems.at[7 * i + k], recv_sem=recv_sems.at[7 * i + k],
                device_id=to, device_id_type=MESH)

        local = [pltpu.make_async_copy(ins[i], rows(i, me), local_sems.at[i]) for i in range(n)]
        zi = 0
        for i in range(n):
            if self.pads[i]:
                start = N_DEV * ins[i].shape[0]
                local.append(pltpu.make_async_copy(ins[n + zi], outs[i].at[pl.ds(start, self.pads[i]), :],
                                                   local_sems.at[n + zi]))
                zi += 1
        first = []
        for i in range(n):
            first.append(copy(i, 0, me, sibling, src=ins[i]))
            first += [copy(i, 1 + j, me, (*chip, c), src=ins[i]) for j, chip in enumerate(chips)]
        return local, first, copy, chips, me, sibling, c

    def start(self, ins, outs, sems):
        local, first, *_ = self._copies(ins, outs, sems)
        for cp in local + first:
            cp.start()

    def finish(self, ins, outs, sems):
        local, first, copy, chips, me, sibling, c = self._copies(ins, outs, sems)
        passed = []
        for j, chip in enumerate(chips):
            for i in range(self.n):
                copy(i, 1 + j, (*chip, c), me).wait_recv()
                fwd = copy(i, 4 + j, (*chip, c), sibling)
                fwd.start()
                passed.append(fwd)
        for i in range(self.n):
            copy(i, 0, sibling, me).wait_recv()
        for j, chip in enumerate(chips):
            for i in range(self.n):
                copy(i, 4 + j, (*chip, 1 - c), me).wait_recv()
        for cp in first + passed:
            cp.wait_send()
        for cp in local:
            cp.wait()


class _SiblingSwap:
    def __init__(self, grads, rows):
        self.n, self.rows = len(grads), list(rows)
        self.inputs = list(grads)
        self.out_shape = [jax.ShapeDtypeStruct((N_CHIP * r, g.shape[1]), g.dtype) for g, r in zip(grads, rows)]
        self.sem_shapes = [pltpu.SemaphoreType.DMA((N_CHIP * self.n,))] * 2

    def _copies(self, ins, outs, sems):
        send_sems, recv_sems = sems
        x, y, c = _my_place()
        copies = []
        for i in range(self.n):
            r = self.rows[i]
            for q in range(N_CHIP):
                k = N_CHIP * i + q
                copies.append(pltpu.make_async_remote_copy(
                    src_ref=ins[i].at[pl.ds((2 * q + 1 - c) * r, r), :], dst_ref=outs[i].at[pl.ds(q * r, r), :],
                    send_sem=send_sems.at[k], recv_sem=recv_sems.at[k], device_id=(x, y, 1 - c), device_id_type=MESH))
        return copies

    def start(self, ins, outs, sems):
        for cp in self._copies(ins, outs, sems):
            cp.start()

    def finish(self, ins, outs, sems):
        copies = self._copies(ins, outs, sems)
        for cp in copies:
            cp.wait_recv()
        for cp in copies:
            cp.wait_send()


class _ChipScatter:
    FLIPS = [(0, 1), (1, 0), (1, 1)]

    def __init__(self, parts):
        self.n = len(parts)
        self.inputs = list(parts)
        self.out_shape = [jax.ShapeDtypeStruct((len(self.FLIPS) * (p.shape[0] // N_CHIP), p.shape[1]), p.dtype)
                          for p in parts]
        self.sem_shapes = [pltpu.SemaphoreType.DMA((3 * self.n,))] * 2

    def _copies(self, ins, outs, sems):
        send_sems, recv_sems = sems
        x, y, c = _my_place()
        copies = []
        for i in range(self.n):
            r = ins[i].shape[0] // N_CHIP
            for k, (fx, fy) in enumerate(self.FLIPS):
                px, py = x ^ fx, y ^ fy
                copies.append(pltpu.make_async_remote_copy(
                    src_ref=ins[i].at[pl.ds((2 * px + py) * r, r), :], dst_ref=outs[i].at[pl.ds(k * r, r), :],
                    send_sem=send_sems.at[3 * i + k], recv_sem=recv_sems.at[3 * i + k],
                    device_id=(px, py, c), device_id_type=MESH))
        return copies

    def start(self, ins, outs, sems):
        for cp in self._copies(ins, outs, sems):
            cp.start()

    def finish(self, ins, outs, sems):
        copies = self._copies(ins, outs, sems)
        for cp in copies:
            cp.wait_recv()
        for cp in copies:
            cp.wait_send()


_FLIPS = [(0, 0, 1), (0, 1, 0), (0, 1, 1), (1, 0, 0), (1, 0, 1), (1, 1, 0), (1, 1, 1)]


class _Broadcast:
    def __init__(self, arrays):
        self.n = len(arrays)
        self.inputs = list(arrays)
        self.out_shape = [jax.ShapeDtypeStruct((N_DEV, *a.shape), a.dtype) for a in arrays]
        self.sem_shapes = [pltpu.SemaphoreType.DMA((7 * self.n,)), pltpu.SemaphoreType.DMA((7 * self.n,)),
                           pltpu.SemaphoreType.DMA((self.n,))]

    def _copies(self, ins, outs, sems):
        send_sems, recv_sems, local_sems = sems
        x, y, c = _my_place()
        my_idx = _block_index(x, y, c)
        local, sends, recvs = [], [], []
        for i in range(self.n):
            local.append(pltpu.make_async_copy(ins[i], outs[i].at[my_idx], local_sems.at[i]))
            for k, (fx, fy, fc) in enumerate(_FLIPS):
                peer = (x ^ fx, y ^ fy, c ^ fc)
                common = dict(send_sem=send_sems.at[7 * i + k], recv_sem=recv_sems.at[7 * i + k],
                              device_id=peer, device_id_type=MESH)
                sends.append(pltpu.make_async_remote_copy(src_ref=ins[i], dst_ref=outs[i].at[my_idx], **common))
                recvs.append(pltpu.make_async_remote_copy(src_ref=ins[i], dst_ref=outs[i].at[_block_index(*peer)], **common))
        return local, sends, recvs

    def start(self, ins, outs, sems):
        local, sends, _ = self._copies(ins, outs, sems)
        for cp in local + sends:
            cp.start()

    def finish(self, ins, outs, sems):
        local, sends, recvs = self._copies(ins, outs, sems)
        for cp in recvs:
            cp.wait_recv()
        for cp in sends:
            cp.wait_send()
        for cp in local:
            cp.wait()


class _Both:
    def __init__(self, first, second):
        self.plans = (first, second)
        self.inputs = first.inputs + second.inputs
        self.out_shape = first.out_shape + second.out_shape
        self.sem_shapes = first.sem_shapes + second.sem_shapes

    def _split(self, ins, outs, sems):
        a = self.plans[0]
        n_in, n_out, n_sem = len(a.inputs), len(a.out_shape), len(a.sem_shapes)
        return [(ins[:n_in], outs[:n_out], sems[:n_sem]), (ins[n_in:], outs[n_out:], sems[n_sem:])]

    def start(self, ins, outs, sems):
        for plan, args in zip(self.plans, self._split(ins, outs, sems)):
            plan.start(*args)

    def finish(self, ins, outs, sems):
        for plan, args in zip(self.plans, self._split(ins, outs, sems)):
            plan.finish(*args)


def _comm_only(comm, *, name):
    return _pcall(None, name=name, grid=(), in_specs=[], out_specs=[], out_shape=[], operands=[], comm=comm)[1]


def _matmul(pairs, out_dtypes, *, name, ta=False, tb=False, separate=False, epilogue=None,
            extras=(), bcast=(), tm=512, tn=512, tk=4096, precision=None, comm=None):
    a0, b0 = pairs[0]
    m_dim, k_dim = (a0.shape[1], a0.shape[0]) if ta else a0.shape
    n_dim = b0.shape[0] if tb else b0.shape[1]
    tm = _tile(m_dim, tm, V7X_LANES if ta else 16)
    tn = _tile(n_dim, tn, V7X_LANES)
    tk = _tile(k_dim, tk, V7X_LANES)
    nk = k_dim // tk
    n_acc = len(pairs) if separate else 1
    n_ex, n_bc, n_out = len(extras), len(bcast), len(out_dtypes)
    dims = (((0 if ta else 1,), (1 if tb else 0,)), ((), ()))
    if epilogue is None:
        epilogue = lambda accs, ex, bc: tuple(accs)
    a_spec = pl.BlockSpec((tk, tm), lambda m, n, k: (k, m)) if ta else pl.BlockSpec((tm, tk), lambda m, n, k: (m, k))
    b_spec = pl.BlockSpec((tn, tk), lambda m, n, k: (n, k)) if tb else pl.BlockSpec((tk, tn), lambda m, n, k: (k, n))
    operands, operand_specs, pair_slots = [], [], []
    for pair in pairs:
        slots = []
        for arr, spec in zip(pair, (a_spec, b_spec)):
            found = [i for i, o in enumerate(operands) if o is arr]
            if not found:
                operands.append(arr)
                operand_specs.append(spec)
                found = [len(operands) - 1]
            slots.append(found[0])
        pair_slots.append(slots)
    n_ops = len(operands)

    def body(*refs):
        a_refs = [refs[sa] for sa, _ in pair_slots]
        b_refs = [refs[sb] for _, sb in pair_slots]
        ex_refs = refs[n_ops:n_ops + n_ex]
        bc_refs = refs[n_ops + n_ex:n_ops + n_ex + n_bc]
        out_refs = refs[n_ops + n_ex + n_bc:n_ops + n_ex + n_bc + n_out]
        acc_refs = refs[n_ops + n_ex + n_bc + n_out:]
        parts = [lax.dot_general(a[...], b[...], dims, preferred_element_type=F32, precision=precision)
                 for a, b in zip(a_refs, b_refs)]
        if not separate:
            parts = [functools.reduce(lambda p, q: p + q, parts)]

        def finish(accs):
            outs = epilogue(accs, [e[...] for e in ex_refs], [c[...] for c in bc_refs])
            for o_ref, o in zip(out_refs, outs):
                o_ref[...] = o.astype(o_ref.dtype)

        if nk == 1:
            finish(parts)
        else:
            k = pl.program_id(2)

            @pl.when(k == 0)
            def _():
                for acc, p in zip(acc_refs, parts):
                    acc[...] = p

            @pl.when(k > 0)
            def _():
                for acc, p in zip(acc_refs, parts):
                    acc[...] += p

            @pl.when(k == nk - 1)
            def _():
                finish([acc[...] for acc in acc_refs])

    mn_spec = pl.BlockSpec((tm, tn), lambda m, n, k: (m, n))
    bc_spec = pl.BlockSpec((1, tn), lambda m, n, k: (0, n))
    outs, c_outs = _pcall(
        body, name=name, grid=(m_dim // tm, n_dim // tn, nk),
        in_specs=operand_specs + [mn_spec] * n_ex + [bc_spec] * n_bc,
        out_specs=[mn_spec] * n_out,
        out_shape=[jax.ShapeDtypeStruct((m_dim, n_dim), d) for d in out_dtypes],
        scratch_shapes=[pltpu.VMEM((tm, tn), F32)] * (n_acc if nk > 1 else 0),
        semantics=("parallel", "parallel", "arbitrary"),
        operands=[*operands, *extras, *bcast], comm=comm)
    outs = outs[0] if n_out == 1 else outs
    return outs if comm is None else (outs, c_outs)


def _with_comm(result, comm):
    return (result, []) if comm is None else result


def _bd(pairs, out_dtype, *, name, offs=None, tm=1024, operand_dtype=None, comm=None):
    n_blocks, ka, kb = pairs[0][1].shape
    t_dim = pairs[0][0].shape[0]
    tm = _tile(t_dim, tm, 16)
    offs = offs or [0] * len(pairs)
    n_pairs = len(pairs)

    def body(*refs):
        acc = None
        for i in range(n_pairs):
            a, w = refs[2 * i][...], refs[2 * i + 1][...]
            if operand_dtype is not None:
                a, w = a.astype(operand_dtype), w.astype(operand_dtype)
            p = jnp.dot(a, w, preferred_element_type=F32)
            acc = p if acc is None else acc + p
        refs[2 * n_pairs][...] = acc.astype(out_dtype)

    in_specs = []
    for off in offs:
        in_specs.append(pl.BlockSpec((tm, ka), lambda j, t, off=off: (t, j + off)))
        in_specs.append(pl.BlockSpec((None, ka, kb), lambda j, t: (j, 0, 0)))
    outs, c_outs = _pcall(
        body, name=name, grid=(n_blocks, t_dim // tm), in_specs=in_specs,
        out_specs=[pl.BlockSpec((tm, kb), lambda j, t: (t, j))],
        out_shape=[jax.ShapeDtypeStruct((t_dim, n_blocks * kb), out_dtype)],
        semantics=("parallel", "parallel"), operands=[t for p in pairs for t in p], comm=comm)
    return outs[0] if comm is None else (outs[0], c_outs)


def _bd_wgrad(a, b, ka, kb, n_blocks, *, name, off_a=0, off_b=0, tm=1024, operand_dtype=None, sign=1.0):
    t_dim = a.shape[0]
    tm = _tile(t_dim, tm, 16)

    def body(a_ref, b_ref, o_ref):
        a, b = a_ref[...], b_ref[...]
        if operand_dtype is not None:
            a, b = a.astype(operand_dtype), b.astype(operand_dtype)
        p = lax.dot_general(a, b, (((0,), (0,)), ((), ())), preferred_element_type=F32)
        if sign != 1.0:
            p = p * sign

        @pl.when(pl.program_id(1) == 0)
        def _():
            o_ref[...] = p

        @pl.when(pl.program_id(1) > 0)
        def _():
            o_ref[...] += p

    return _pcall(
        body, name=name, grid=(n_blocks, t_dim // tm),
        in_specs=[pl.BlockSpec((tm, ka), lambda j, t: (t, j + off_a)),
                  pl.BlockSpec((tm, kb), lambda j, t: (t, j + off_b))],
        out_specs=[pl.BlockSpec((None, ka, kb), lambda j, t: (j, 0, 0))],
        out_shape=[jax.ShapeDtypeStruct((n_blocks, ka, kb), F32)],
        semantics=("parallel", "arbitrary"), operands=[a, b])[0][0]


def _rowk(fn, rows, bcast, outs, *, name, tm=256):
    rows = [r if isinstance(r, tuple) else (r, r.shape[1], 0) for r in rows]
    t_dim = rows[0][0].shape[0]
    tm = _tile(t_dim, tm, 16)
    n_rows, n_bc = len(rows), len(bcast)

    def body(*refs):
        ins = [r[...] for r in refs[:n_rows + n_bc]]
        vals = fn(*ins)
        first = pl.program_id(0) == 0
        for o_ref, v, spec in zip(refs[n_rows + n_bc:], vals, outs):
            if spec[0] == "row":
                o_ref[...] = v.astype(o_ref.dtype)
            else:
                @pl.when(first)
                def _(o_ref=o_ref, v=v):
                    o_ref[...] = v

                @pl.when(jnp.logical_not(first))
                def _(o_ref=o_ref, v=v):
                    o_ref[...] += v

    in_specs = [pl.BlockSpec((tm, w), lambda i, cb=cb: (i, cb)) for _, w, cb in rows]
    in_specs += [pl.BlockSpec((1, b.shape[1]), lambda i: (0, 0)) for b in bcast]
    out_specs, out_shape = [], []
    for spec in outs:
        if spec[0] == "row":
            out_specs.append(pl.BlockSpec((tm, spec[1]), lambda i: (i, 0)))
            out_shape.append(jax.ShapeDtypeStruct((t_dim, spec[1]), spec[2]))
        else:
            out_specs.append(pl.BlockSpec((1, spec[1]), lambda i: (0, 0)))
            out_shape.append(jax.ShapeDtypeStruct((1, spec[1]), F32))
    return _pcall(body, name=name, grid=(t_dim // tm,), in_specs=in_specs, out_specs=out_specs, out_shape=out_shape,
                  semantics=("arbitrary",), operands=[*[r[0] for r in rows], *bcast])[0]


def _colsum(v):
    return jnp.sum(v, axis=0, keepdims=True)


def _rms_fwd(x, g, *, name):
    def fn(x, g):
        r = lax.rsqrt(jnp.mean(x * x, axis=-1, keepdims=True) + NORM_EPS)
        return (x * r * g).astype(BF16), r
    d = x.shape[1]
    return _rowk(fn, [x], [g], [("row", d, BF16), ("row", 1, F32)], name=name)


def _rms_bwd_rows(dn, x, r, g):
    xh = x * r
    dy = dn * g
    dx = r * (dy - xh * jnp.mean(dy * xh, axis=-1, keepdims=True))
    return dx, _colsum(dn * xh)


def _silu_parts(g):
    sg = jax.nn.sigmoid(g)
    return g * sg, sg * (1.0 + g * (1.0 - sg))


def _ffn_fwd(h, norm_g, w_gate_t, w_up_t, get_w_down, tag, comm_gateup, comm_down):
    n, r = _rms_fwd(h, norm_g, name=f"{tag}_rms")

    def gate_up(accs, ex, bc):
        g, u = accs
        return g, u, _silu_parts(g)[0] * u

    (g, u, a), got1 = _matmul([(n, w_gate_t), (n, w_up_t)], [BF16, BF16, BF16], name=f"{tag}_gateup", tb=True,
                              separate=True, epilogue=gate_up, tm=1024, comm=comm_gateup)
    w_down = get_w_down(got1)
    out, got2 = _with_comm(_matmul([(a, w_down)], [F32], name=f"{tag}_down", extras=[h], tm=1024, tn=1024, tk=1024,
                                   epilogue=lambda accs, ex, bc: (ex[0] + 0.5 * accs[0],), comm=comm_down), comm_down)
    return out, (h, n, r, g, u, a), got1, got2


def _ffn_bwd_act(dyb, saved, w_down, tag, comm):
    g, u = saved[3], saved[4]

    def act_bwd(accs, ex, bc):
        da, g, u = accs[0], ex[0].astype(F32), ex[1].astype(F32)
        silu, dsilu = _silu_parts(g)
        return da * u * dsilu, da * silu

    return _matmul([(dyb, w_down)], [BF16, BF16], name=f"{tag}_dact", tb=True, extras=[g, u], epilogue=act_bwd, tm=1024,
                   comm=comm)


def _ffn_bwd_input(dh, dg, du, saved, norm_g, w_gate_t, w_up_t, tag, comm):
    h, n, r = saved[0], saved[1], saved[2]
    dn, got = _with_comm(_matmul([(dg, w_gate_t), (du, w_up_t)], [F32], name=f"{tag}_dn", tm=1024, tn=1024, tk=1024,
                                 comm=comm), comm)

    def fn(dh, dn, h, r, gain):
        dx, dgain = _rms_bwd_rows(dn, h, r, gain)
        dx = dh + dx
        return dx, (0.5 * dx).astype(BF16), dx.astype(BF16), dgain

    d = h.shape[1]
    dx, dx_half_b, dx_b, d_norm = _rowk(fn, [dh, dn, h, r], [norm_g],
                                        [("row", d, F32), ("row", d, BF16), ("row", d, BF16), ("sum", d)],
                                        name=f"{tag}_rms_bwd")
    return dx, dx_half_b, dx_b, d_norm, got


def _pool_window(z, seq, width, group_width, *, name, transpose):
    assert POOL_WINDOWS == (2, 4, 8, 16)
    t_dim = z.shape[0]
    tc = _tile(group_width, 256, V7X_LANES)
    per_group = group_width // tc

    def body(z_ref, o_ref):
        gid = pl.program_id(1) // per_group
        v = z_ref[...]
        t = lax.broadcasted_iota(jnp.int32, v.shape, 0)
        win = jnp.where(gid == 0, 2, jnp.where(gid == 1, 4, jnp.where(gid == 2, 8, 16)))
        cnt = jnp.minimum(t + 1, win).astype(F32)
        if transpose:
            e = v / cnt
            shift = lambda q, k: jnp.where(t < seq - k, pltpu.roll(q, seq - k, 0), 0.0)
        else:
            e = v
            shift = lambda q, k: jnp.where(t >= k, pltpu.roll(q, k, 0), 0.0)
        s1 = e + shift(e, 1)
        s2 = s1 + shift(s1, 2)
        s3 = s2 + shift(s2, 4)
        s4 = s3 + shift(s3, 8)
        s = jnp.where(gid == 0, s1, jnp.where(gid == 1, s2, jnp.where(gid == 2, s3, s4)))
        if transpose:
            o_ref[...] = (s - v).astype(o_ref.dtype)
        else:
            o_ref[...] = (s / cnt - v).astype(o_ref.dtype)

    spec = pl.BlockSpec((seq, tc), lambda b, c: (b, c))
    return _pcall(body, name=name, grid=(t_dim // seq, width // tc), in_specs=[spec], out_specs=[spec],
                  out_shape=[jax.ShapeDtypeStruct((t_dim, width), F32 if transpose else BF16)],
                  semantics=("parallel", "parallel"), operands=[z])[0][0]


def _discretize(lam_re, lam_im, log_dt, b_re_t, b_im_t):
    dt = jnp.exp(log_dt)
    mag = jnp.exp(lam_re * dt)
    a_re = mag * jnp.cos(lam_im * dt)
    a_im = mag * jnp.sin(lam_im * dt)
    den = lam_re * lam_re + lam_im * lam_im
    f_re = ((a_re - 1.0) * lam_re + a_im * lam_im) / den
    f_im = (a_im * lam_re - (a_re - 1.0) * lam_im) / den
    f_re, f_im = f_re[:, None, :], f_im[:, None, :]
    return a_re, a_im, f_re * b_re_t - f_im * b_im_t, f_re * b_im_t + f_im * b_re_t


def _discretize_fwd(params):
    shapes = [jax.ShapeDtypeStruct(params[0].shape, F32)] * 2 + [jax.ShapeDtypeStruct(params[3].shape, F32)] * 2

    def body(*refs):
        outs = _discretize(*[r[...] for r in refs[:5]])
        for o_ref, o in zip(refs[5:], outs):
            o_ref[...] = o

    return pl.pallas_call(body, name="s5_discretize", out_shape=shapes,
                          compiler_params=pltpu.CompilerParams(vmem_limit_bytes=V7X_VMEM_LIMIT_BYTES))(*params)


def _discretize_bwd(params, cots):
    shapes = [jax.ShapeDtypeStruct(p.shape, F32) for p in params]

    def body(*refs):
        _, vjp = jax.vjp(_discretize, *[r[...] for r in refs[:5]])
        grads = vjp(tuple(r[...] for r in refs[5:9]))
        for o_ref, o in zip(refs[9:], grads):
            o_ref[...] = o

    return pl.pallas_call(body, name="s5_discretize_bwd", out_shape=shapes,
                          compiler_params=pltpu.CompilerParams(vmem_limit_bytes=V7X_VMEM_LIMIT_BYTES))(*params, *cots)


def _scan_tiles(t_dim, n_dim, seq):
    return _tile(seq, 256, V7X_SUBLANES), _tile(n_dim, 1024, V7X_LANES)


def _scan_fwd(bu_re, bu_im, a_re, a_im, seq, comm=None):
    t_dim, n_dim = bu_re.shape
    tt, tl = _scan_tiles(t_dim, n_dim, seq)
    per_seq = seq // tt

    def body(br_ref, bi_ref, ar_ref, ai_ref, xr_ref, xi_ref, cr_ref, ci_ref):
        @pl.when(pl.program_id(1) % per_seq == 0)
        def _():
            cr_ref[...] = jnp.zeros_like(cr_ref)
            ci_ref[...] = jnp.zeros_like(ci_ref)

        ar, ai = ar_ref[...], ai_ref[...]

        def step(i, carry):
            xr, xi = carry
            row = pl.ds(i, 1)
            nr = ar * xr - ai * xi + br_ref[row, :]
            ni = ai * xr + ar * xi + bi_ref[row, :]
            xr_ref[row, :] = nr
            xi_ref[row, :] = ni
            return nr, ni

        xr, xi = lax.fori_loop(0, tt, step, (cr_ref[...], ci_ref[...]), unroll=8)
        cr_ref[...] = xr
        ci_ref[...] = xi

    blk = pl.BlockSpec((tt, tl), lambda l, t: (t, l))
    vec = pl.BlockSpec((1, tl), lambda l, t: (0, l))
    outs, c_outs = _pcall(
        body, name="s5_scan", grid=(n_dim // tl, t_dim // tt), in_specs=[blk, blk, vec, vec], out_specs=[blk, blk],
        out_shape=[jax.ShapeDtypeStruct((t_dim, n_dim), F32)] * 2, scratch_shapes=[pltpu.VMEM((1, tl), F32)] * 2,
        semantics=("parallel", "arbitrary"), operands=[bu_re, bu_im, a_re, a_im], comm=comm)
    return outs, c_outs


def _scan_bwd(gx_re, gx_im, x_re, x_im, a_re, a_im, seq, comm=None):
    t_dim, n_dim = gx_re.shape
    tt, tl = _scan_tiles(t_dim, n_dim, seq)
    per_seq = seq // tt
    n_t = t_dim // tt
    prev_rows = V7X_SUBLANES

    def body(gr_ref, gi_ref, xr_ref, xi_ref, pr_ref, pi_ref, ar_ref, ai_ref,
             lr_ref, li_ref, dar_ref, dai_ref, cr_ref, ci_ref):
        step_id = pl.program_id(1)
        blk_id = n_t - 1 - step_id

        @pl.when((blk_id + 1) % per_seq == 0)
        def _():
            cr_ref[...] = jnp.zeros_like(cr_ref)
            ci_ref[...] = jnp.zeros_like(ci_ref)

        ar, ai = ar_ref[...], ai_ref[...]

        def step(j, carry):
            lr, li = carry
            row = pl.ds(tt - 1 - j, 1)
            nr = gr_ref[row, :] + ar * lr + ai * li
            ni = gi_ref[row, :] - ai * lr + ar * li
            lr_ref[row, :] = nr
            li_ref[row, :] = ni
            return nr, ni

        lr, li = lax.fori_loop(0, tt, step, (cr_ref[...], ci_ref[...]), unroll=8)
        cr_ref[...] = lr
        ci_ref[...] = li

        first_of_seq = blk_id % per_seq == 0
        keep = jnp.where(first_of_seq, 0.0, 1.0)
        t = lax.broadcasted_iota(jnp.int32, (tt, tl), 0)
        xr_prev = jnp.where(t == 0, pr_ref[prev_rows - 1:prev_rows, :] * keep, pltpu.roll(xr_ref[...], 1, 0))
        xi_prev = jnp.where(t == 0, pi_ref[prev_rows - 1:prev_rows, :] * keep, pltpu.roll(xi_ref[...], 1, 0))
        lam_r, lam_i = lr_ref[...], li_ref[...]
        d_re = _colsum(lam_r * xr_prev + lam_i * xi_prev)
        d_im = _colsum(lam_i * xr_prev - lam_r * xi_prev)

        @pl.when(step_id == 0)
        def _():
            dar_ref[...] = d_re
            dai_ref[...] = d_im

        @pl.when(step_id > 0)
        def _():
            dar_ref[...] += d_re
            dai_ref[...] += d_im

    blk = pl.BlockSpec((tt, tl), lambda l, t: (n_t - 1 - t, l))
    prev = pl.BlockSpec((prev_rows, tl), lambda l, t: (jnp.maximum((n_t - 1 - t) * (tt // prev_rows) - 1, 0), l))
    vec = pl.BlockSpec((1, tl), lambda l, t: (0, l))
    return _pcall(
        body, name="s5_scan_bwd", grid=(n_dim // tl, n_t),
        in_specs=[blk, blk, blk, blk, prev, prev, vec, vec], out_specs=[blk, blk, vec, vec],
        out_shape=[jax.ShapeDtypeStruct((t_dim, n_dim), F32)] * 2 + [jax.ShapeDtypeStruct((1, n_dim), F32)] * 2,
        scratch_shapes=[pltpu.VMEM((1, tl), F32)] * 2, semantics=("parallel", "arbitrary"),
        operands=[gx_re, gx_im, x_re, x_im, x_re, x_im, a_re, a_im], comm=comm)


def _block_diag(w, per_block):
    g, ka, kb = w.shape
    eye = jnp.eye(per_block, dtype=w.dtype)
    out = jnp.einsum("jakb,ac->jakcb", w.reshape(g // per_block, per_block, ka, kb), eye)
    return out.reshape(g // per_block, per_block * ka, per_block * kb)


def _block_diag_t(d, per_block, ka, kb):
    j = d.shape[0]
    eye = jnp.eye(per_block, dtype=d.dtype)
    picked = jnp.einsum("jakcb,ac->jakb", d.reshape(j, per_block, ka, per_block, kb), eye)
    return picked.reshape(j * per_block, ka, kb)


def _gelu_parts(y):
    inner = GELU_C * (y + GELU_A * y * y * y)
    th = jnp.tanh(inner)
    val = 0.5 * y * (1.0 + th)
    grad = 0.5 * (1.0 + th) + 0.5 * y * (1.0 - th * th) * GELU_C * (1.0 + 3.0 * GELU_A * y * y)
    return val, grad


def _chip_sum(grad, got, core, *, name):
    r, c = got.shape[0] // N_CHIP, got.shape[1]
    tr = _tile(r, 704, 16)
    tc = _tile(c, 2048, V7X_LANES)
    per = r // tr

    def body(core_ref, g_ref, s_ref, o_ref):
        o_ref[...] = (g_ref[...].astype(F32) + s_ref[...].astype(F32)).astype(o_ref.dtype)

    slot = pl.BlockSpec((tr, tc), lambda q, i, j, core: (q * per + i, j))
    grid_spec = pltpu.PrefetchScalarGridSpec(
        num_scalar_prefetch=1, grid=(N_CHIP, per, c // tc),
        in_specs=[pl.BlockSpec((tr, tc), lambda q, i, j, core: ((2 * q + core[0]) * per + i, j)), slot],
        out_specs=slot)
    return pl.pallas_call(
        body, name=name, grid_spec=grid_spec, out_shape=jax.ShapeDtypeStruct(got.shape, got.dtype),
        compiler_params=pltpu.CompilerParams(dimension_semantics=("parallel",) * 3,
                                             vmem_limit_bytes=V7X_VMEM_LIMIT_BYTES))(core, grad, got)


def _adamw_math(w, g, m, v):
    nm = ADAM_B1 * m + (1.0 - ADAM_B1) * g
    nv = ADAM_B2 * v + (1.0 - ADAM_B2) * (g * g)
    m_hat = nm * (1.0 / (1.0 - ADAM_B1 ** ADAM_STEP))
    v_hat = nv * (1.0 / (1.0 - ADAM_B2 ** ADAM_STEP))
    return -ADAM_LR * (m_hat / (jnp.sqrt(v_hat) + ADAM_EPS) + ADAM_WD * w), nm, nv


def _sum_adamw(parts, others, chip, w, m, v, *, name):
    r, c = w.shape
    n_others = others.shape[0] // r
    tr = _tile(r, 704, 16)
    tc = _tile(c, 512, V7X_LANES)
    per = r // tr

    def body(chip_ref, p_ref, *refs):
        o_refs, (w_ref, m_ref, v_ref), (g_ref, d_ref, nm_ref, nv_ref) = refs[:n_others], refs[n_others:n_others + 3], refs[n_others + 3:]
        g = p_ref[...].astype(F32)
        for o_ref in o_refs:
            g = g + o_ref[...].astype(F32)
        g_ref[...] = g
        d_ref[...], nm_ref[...], nv_ref[...] = _adamw_math(w_ref[...], g, m_ref[...], v_ref[...])

    own = pl.BlockSpec((tr, tc), lambda i, j, chip: (i, j))
    grid_spec = pltpu.PrefetchScalarGridSpec(
        num_scalar_prefetch=1, grid=(per, c // tc),
        in_specs=[pl.BlockSpec((tr, tc), lambda i, j, chip: (chip[0] * per + i, j))]
        + [pl.BlockSpec((tr, tc), lambda i, j, chip, s=s: (s * per + i, j)) for s in range(n_others)] + [own] * 3,
        out_specs=[own] * 4)
    return pl.pallas_call(
        body, name=name, grid_spec=grid_spec, out_shape=[jax.ShapeDtypeStruct((r, c), F32)] * 4,
        compiler_params=pltpu.CompilerParams(dimension_semantics=("parallel",) * 2,
                                             vmem_limit_bytes=V7X_VMEM_LIMIT_BYTES))(chip, parts, *[others] * n_others, w, m, v)


def _sum_parts(parts, *, name):
    n_parts, r, c = parts.shape
    tr = _tile(r, 704, 16)
    tc = _tile(c, 1024, V7X_LANES)

    def body(p_ref, o_ref):
        acc = p_ref[0].astype(F32)
        for s in range(1, n_parts):
            acc = acc + p_ref[s].astype(F32)
        o_ref[...] = acc

    return _pcall(body, name=name, grid=(r // tr, c // tc),
                  in_specs=[pl.BlockSpec((n_parts, tr, tc), lambda i, j: (0, i, j))],
                  out_specs=[pl.BlockSpec((tr, tc), lambda i, j: (i, j))],
                  out_shape=[jax.ShapeDtypeStruct((r, c), F32)], semantics=("parallel", "parallel"),
                  operands=[parts])[0][0]


def _adamw(w, g, m, v, *, name):
    r, c = w.shape
    tr = _tile(r, max(V7X_SUBLANES, ADAMW_BLOCK_BYTES // (4 * c)), V7X_SUBLANES)

    def body(w_ref, g_ref, m_ref, v_ref, d_ref, nm_ref, nv_ref):
        d_ref[...], nm_ref[...], nv_ref[...] = _adamw_math(w_ref[...], g_ref[...], m_ref[...], v_ref[...])

    spec = pl.BlockSpec((tr, c), lambda i: (i, 0))
    return _pcall(body, name=name, grid=(r // tr,), in_specs=[spec] * 4, out_specs=[spec] * 3,
                  out_shape=[jax.ShapeDtypeStruct((r, c), F32)] * 3, semantics=("parallel",), operands=[w, g, m, v])[0]


def _pack(arrays, width=V7X_LANES):
    tile = V7X_SUBLANES * width
    parts, layout, row = [], [], 0
    for a in arrays:
        n = a.size
        rows = -(-n // tile) * V7X_SUBLANES
        flat = jnp.pad(a.reshape(-1).astype(F32), (0, rows * width - n))
        parts.append(flat.reshape(rows, width))
        layout.append((row, rows, n, a.shape))
        row += rows
    return jnp.concatenate(parts, axis=0), layout


def _unpack(packed, layout):
    return [packed[row:row + rows].reshape(-1)[:n].reshape(shape) for row, rows, n, shape in layout]


def kernel(x, ffn1_norm, ffn1_gate, ffn1_up, ffn1_down, mix_norm, w_in, w_pool, pool_scale, lam_re, lam_im, log_dt, b_re, b_im, c_re, c_im, d_skip, w_glu, b_glu, pool_out_norm, ssm_out_norm, w_out, ffn2_norm, ffn2_gate, ffn2_up, ffn2_down, final_norm, loss_target, m_ffn1_norm, m_ffn1_gate, m_ffn1_up, m_ffn1_down, m_mix_norm, m_w_in, m_w_pool, m_pool_scale, m_lam_re, m_lam_im, m_log_dt, m_b_re, m_b_im, m_c_re, m_c_im, m_d_skip, m_w_glu, m_b_glu, m_pool_out_norm, m_ssm_out_norm, m_w_out, m_ffn2_norm, m_ffn2_gate, m_ffn2_up, m_ffn2_down, m_final_norm, v_ffn1_norm, v_ffn1_gate, v_ffn1_up, v_ffn1_down, v_mix_norm, v_w_in, v_w_pool, v_pool_scale, v_lam_re, v_lam_im, v_log_dt, v_b_re, v_b_im, v_c_re, v_c_im, v_d_skip, v_w_glu, v_b_glu, v_pool_out_norm, v_ssm_out_norm, v_w_out, v_ffn2_norm, v_ffn2_gate, v_ffn2_up, v_ffn2_down, v_final_norm):
    weights = dict(ffn1_norm=ffn1_norm, ffn1_gate=ffn1_gate, ffn1_up=ffn1_up, ffn1_down=ffn1_down, mix_norm=mix_norm, w_in=w_in, w_pool=w_pool, pool_scale=pool_scale, lam_re=lam_re, lam_im=lam_im, log_dt=log_dt, b_re=b_re, b_im=b_im, c_re=c_re, c_im=c_im, d_skip=d_skip, w_glu=w_glu, b_glu=b_glu, pool_out_norm=pool_out_norm, ssm_out_norm=ssm_out_norm, w_out=w_out, ffn2_norm=ffn2_norm, ffn2_gate=ffn2_gate, ffn2_up=ffn2_up, ffn2_down=ffn2_down, final_norm=final_norm)
    moments_m = dict(ffn1_norm=m_ffn1_norm, ffn1_gate=m_ffn1_gate, ffn1_up=m_ffn1_up, ffn1_down=m_ffn1_down, mix_norm=m_mix_norm, w_in=m_w_in, w_pool=m_w_pool, pool_scale=m_pool_scale, lam_re=m_lam_re, lam_im=m_lam_im, log_dt=m_log_dt, b_re=m_b_re, b_im=m_b_im, c_re=m_c_re, c_im=m_c_im, d_skip=m_d_skip, w_glu=m_w_glu, b_glu=m_b_glu, pool_out_norm=m_pool_out_norm, ssm_out_norm=m_ssm_out_norm, w_out=m_w_out, ffn2_norm=m_ffn2_norm, ffn2_gate=m_ffn2_gate, ffn2_up=m_ffn2_up, ffn2_down=m_ffn2_down, final_norm=m_final_norm)
    moments_v = dict(ffn1_norm=v_ffn1_norm, ffn1_gate=v_ffn1_gate, ffn1_up=v_ffn1_up, ffn1_down=v_ffn1_down, mix_norm=v_mix_norm, w_in=v_w_in, w_pool=v_w_pool, pool_scale=v_pool_scale, lam_re=v_lam_re, lam_im=v_lam_im, log_dt=v_log_dt, b_re=v_b_re, b_im=v_b_im, c_re=v_c_re, c_im=v_c_im, d_skip=v_d_skip, w_glu=v_w_glu, b_glu=v_b_glu, pool_out_norm=v_pool_out_norm, ssm_out_norm=v_ssm_out_norm, w_out=v_w_out, ffn2_norm=v_ffn2_norm, ffn2_gate=v_ffn2_gate, ffn2_up=v_ffn2_up, ffn2_down=v_ffn2_down, final_norm=v_final_norm)
    names = list(weights)

    n_seq, seq, d_model = x.shape
    t_dim = n_seq * seq
    ff_shard = ffn1_gate.shape[1]
    ff = N_DEV * ff_shard
    ff_pad = -(-ff // FF_PAD_MULTIPLE) * FF_PAD_MULTIPLE - ff
    n_pool, pool_gw = w_pool.shape[0], w_pool.shape[2]
    pool_w = n_pool * pool_gw
    pool_rows = pool_gw // N_DEV
    n_grp, n_state, n_ch = b_re.shape
    ssm_w = n_grp * n_ch
    grp_per_blk = V7X_LANES // n_ch
    n_blk = n_grp // grp_per_blk
    ch_blk, st_blk = grp_per_blk * n_ch, grp_per_blk * n_state
    ssm_off = pool_w // ch_blk

    x2 = x.reshape(t_dim, d_model)
    tgt2 = loss_target.reshape(t_dim, d_model)
    row = lambda p: p.reshape(1, -1)

    sharded = ["ffn1_gate", "ffn1_up", "ffn1_down", "ffn2_gate", "ffn2_up", "ffn2_down", "w_in", "w_out", "w_glu", "w_pool"]
    transposed = {"ffn1_gate", "ffn1_up", "ffn2_gate", "ffn2_up"}
    shard = {n: (weights[n].T if n in transposed else weights[n]).astype(BF16) for n in sharded}

    def gather(*group):
        return _Gather([shard[n] for n in group], [ff_pad if n.startswith("ffn") else 0 for n in group])

    full = {}
    full["ffn1_gate"], full["ffn1_up"] = _comm_only(gather("ffn1_gate", "ffn1_up"), name="gather_first")

    def ffn1_down_weights(got):
        full["ffn1_down"], full["w_in"] = got
        return full["ffn1_down"]

    h1, saved1, _, got = _ffn_fwd(x2, row(ffn1_norm), full["ffn1_gate"], full["ffn1_up"], ffn1_down_weights, "ffn1",
                                  gather("ffn1_down", "w_in"), gather("ffn2_gate"))
    (full["ffn2_gate"],) = got
    n2, r2 = _rms_fwd(h1, row(mix_norm), name="mix_rms")
    z, (full["w_glu"], full["w_pool"]) = _matmul([(n2, full["w_in"])], [F32], name="mix_in", tm=1024,
                                                 comm=gather("w_glu", "w_pool"))

    d_pool = _pool_window(z, seq, pool_w, pool_gw, name="pool_window", transpose=False)
    y_pool_lin = _bd([(d_pool, full["w_pool"])], F32, name="pool_mix")

    disc_params = (lam_re, lam_im, log_dt.reshape(n_grp, 1), jnp.swapaxes(b_re, 1, 2), jnp.swapaxes(b_im, 1, 2))
    a_re, a_im, bb_re_t, bb_im_t = _discretize_fwd(disc_params)
    a_re_row, a_im_row = row(a_re), row(a_im)
    wb_re, wb_im = _block_diag(bb_re_t, grp_per_blk), _block_diag(bb_im_t, grp_per_blk)
    wc_re = _block_diag(jnp.swapaxes(c_re, 1, 2), grp_per_blk)
    wc_im = _block_diag(jnp.swapaxes(c_im, 1, 2), grp_per_blk)
    bu_re = _bd([(z, wb_re)], F32, name="s5_bu_re", offs=[ssm_off], operand_dtype=BF16)
    bu_im = _bd([(z, wb_im)], F32, name="s5_bu_im", offs=[ssm_off], operand_dtype=BF16)
    (xs_re, xs_im), (full["ffn2_up"],) = _scan_fwd(bu_re, bu_im, a_re_row, a_im_row, seq, comm=gather("ffn2_up"))
    y_lin, (full["w_out"],) = _bd([(xs_re, wc_re), (xs_im, -wc_im)], F32, name="s5_cx", operand_dtype=BF16,
                                  comm=gather("w_out"))

    def s5_post(y_lin, u, skip):
        y = y_lin + skip * u
        return y, _gelu_parts(y)[0].astype(BF16)

    u_cols = (z, ssm_w, pool_w // ssm_w)
    y_ssm, yg = _rowk(s5_post, [y_lin, u_cols], [row(d_skip)], [("row", ssm_w, F32), ("row", ssm_w, BF16)], name="s5_post")

    def glu(accs, ex, bc):
        q = accs[0] + bc[0]
        return q, ex[0].astype(F32) * jax.nn.sigmoid(q)

    q_glu, y_s5 = _matmul([(yg, full["w_glu"])], [F32, F32], name="s5_glu", extras=[yg], bcast=[row(b_glu)], epilogue=glu)

    def merge(yp_lin, ys, scale, gp, gs):
        yp = yp_lin * scale
        rp = lax.rsqrt(jnp.mean(yp * yp, axis=-1, keepdims=True) + NORM_EPS)
        rs = lax.rsqrt(jnp.mean(ys * ys, axis=-1, keepdims=True) + NORM_EPS)
        merged = jnp.concatenate([yp * rp * gp, ys * rs * gs], axis=-1)
        return merged.astype(BF16), rp, rs

    merged, r_pool, r_ssm = _rowk(merge, [y_pool_lin, y_s5], [row(pool_scale), row(pool_out_norm), row(ssm_out_norm)],
                                  [("row", pool_w + ssm_w, BF16), ("row", 1, F32), ("row", 1, F32)], name="mix_merge")
    h2 = _matmul([(merged, full["w_out"])], [F32], name="mix_out", extras=[h1], tm=1024,
                 epilogue=lambda accs, ex, bc: (ex[0] + accs[0],))

    def ffn2_down_weights(got):
        (full["ffn2_down"],) = got
        return full["ffn2_down"]

    h3, saved2, _, _ = _ffn_fwd(h2, row(ffn2_norm), full["ffn2_gate"], full["ffn2_up"], ffn2_down_weights, "ffn2",
                                gather("ffn2_down"), None)

    def head(h, tgt, gain):
        r = lax.rsqrt(jnp.mean(h * h, axis=-1, keepdims=True) + NORM_EPS)
        xh = h * r
        err = xh * gain - tgt
        loss = jnp.sum(0.5 * jnp.mean(err * err, axis=-1, keepdims=True), axis=0, keepdims=True)
        dout = err * (1.0 / d_model)
        dy = dout * gain
        dh = r * (dy - xh * jnp.mean(dy * xh, axis=-1, keepdims=True))
        return dh, (0.5 * dh).astype(BF16), _colsum(dout * xh), jnp.broadcast_to(loss, (1, V7X_LANES))

    dh3, dyb2, g_final, loss_part = _rowk(head, [h3, tgt2], [row(final_norm)],
                                          [("row", d_model, F32), ("row", d_model, BF16), ("sum", d_model), ("sum", V7X_LANES)],
                                          name="loss_head")

    grads = {}
    shard_rows = {n: (n_pool * pool_rows if n == "w_pool" else shard[n].shape[0]) for n in sharded}

    def to_sibling(group, name):
        got = _comm_only(_SiblingSwap([grads[n] for n in group], [shard_rows[n] for n in group]), name=name)
        for n, g in zip(group, got):
            chip_parts[n] = _chip_sum(grads[n], g, my_core, name=f"chip_sum_{n}")
        return [chip_parts[n] for n in group]

    my_core = lax.axis_index("c").astype(jnp.int32).reshape(1)
    my_chip = (2 * lax.axis_index("x") + lax.axis_index("y")).astype(jnp.int32).reshape(1)
    chip_parts, from_chips = {}, {}

    def to_chips(group, parts):
        return _ChipScatter(parts), group

    def take(group, got):
        for n, g in zip(group, got):
            from_chips[n] = g

    def ffn_backward(tag, dh, dyb, saved, norm_g, first, last=None):
        gate, up, down = f"{tag}_gate", f"{tag}_up", f"{tag}_down"
        n, a = saved[1], saved[5]
        plan, group = first
        grads[down], got = _with_comm(_matmul([(a, dyb)], [BF16], name=f"{tag}_dwdown", ta=True, tn=1024, comm=plan), plan)
        take(group, got)
        plan, group = to_chips([down], to_sibling([down], f"grads_to_sibling_{down}"))
        (dg, du), got = _ffn_bwd_act(dyb, saved, full[down], tag, plan)
        take(group, got)
        grads[gate] = _matmul([(dg, n)], [BF16], name=f"{tag}_dwgate", ta=True, tn=1024)
        plan, group = to_chips([gate], to_sibling([gate], f"grads_to_sibling_{gate}"))
        grads[up], got = _matmul([(du, n)], [BF16], name=f"{tag}_dwup", ta=True, tn=1024, comm=plan)
        take(group, got)
        plan, group = to_chips([up], to_sibling([up], f"grads_to_sibling_{up}"))
        if last is not None:
            plan = _Both(plan, last)
        dx, _, dx_b, grads[f"{tag}_norm"], got = _ffn_bwd_input(dh, dg, du, saved, norm_g, full[gate], full[up], tag, plan)
        take(group, got)
        return dx, dx_b, got[len(group):]

    dh2, dh2b, _ = ffn_backward("ffn2", dh3, dyb2, saved2, row(ffn2_norm), (None, []))

    d_merged = _matmul([(dh2b, full["w_out"])], [F32], name="mix_out_dx", tb=True, tm=1024)
    grads["w_out"] = _matmul([(merged, dh2b)], [BF16], name="mix_out_dw", ta=True, tn=1024)

    def merge_bwd(dm, yp_lin, ys, rp, rs, scale, gp, gs):
        yp = yp_lin * scale
        d_yp, d_gp = _rms_bwd_rows(dm[:, :pool_w], yp, rp, gp)
        d_ys, d_gs = _rms_bwd_rows(dm[:, pool_w:], ys, rs, gs)
        return (d_yp * scale).astype(BF16), d_ys, _colsum(d_yp * yp_lin), d_gp, d_gs

    d_pool_lin, d_ys, grads["pool_scale"], grads["pool_out_norm"], grads["ssm_out_norm"] = _rowk(
        merge_bwd, [d_merged, y_pool_lin, y_s5, r_pool, r_ssm], [row(pool_scale), row(pool_out_norm), row(ssm_out_norm)],
        [("row", pool_w, BF16), ("row", ssm_w, F32), ("sum", pool_w), ("sum", pool_w), ("sum", ssm_w)], name="mix_merge_bwd")

    w_pool_t = jnp.swapaxes(full["w_pool"], 1, 2)
    dd_pool = _bd([(d_pool_lin, w_pool_t)], F32, name="pool_mix_dx")
    g_pool = _bd_wgrad(d_pool, d_pool_lin, pool_gw, pool_gw, n_pool, name="pool_mix_dw")
    grads["w_pool"] = g_pool.reshape(n_pool, N_DEV, pool_rows, pool_gw).transpose(1, 0, 2, 3).reshape(
        N_DEV * n_pool * pool_rows, pool_gw).astype(BF16)
    dz_pool = _pool_window(dd_pool, seq, pool_w, pool_gw, name="pool_window_bwd", transpose=True)

    def glu_bwd(d_ys, yg, q):
        sg = jax.nn.sigmoid(q)
        dq = d_ys * yg.astype(F32) * sg * (1.0 - sg)
        return dq.astype(BF16), d_ys * sg, _colsum(dq)

    dq, d_yg_direct, grads["b_glu"] = _rowk(glu_bwd, [d_ys, yg, q_glu], [],
                                            [("row", ssm_w, BF16), ("row", ssm_w, F32), ("sum", ssm_w)], name="s5_glu_bwd")
    d_yg_mm = _matmul([(dq, full["w_glu"])], [F32], name="s5_glu_dx", tb=True)
    grads["w_glu"] = _matmul([(yg, dq)], [BF16], name="s5_glu_dw", ta=True, tn=1024)

    def gelu_bwd(d1, d2, y, u, skip):
        dy = (d1 + d2) * _gelu_parts(y)[1]
        return dy, dy * skip, _colsum(dy * u)

    dy_ssm, du_skip, grads["d_skip"] = _rowk(gelu_bwd, [d_yg_direct, d_yg_mm, y_ssm, u_cols], [row(d_skip)],
                                             [("row", ssm_w, F32), ("row", ssm_w, F32), ("sum", ssm_w)], name="s5_gelu_bwd")
    wc_re_t, wc_im_t = jnp.swapaxes(wc_re, 1, 2), jnp.swapaxes(wc_im, 1, 2)
    gx_re = _bd([(dy_ssm, wc_re_t)], F32, name="s5_gx_re", operand_dtype=BF16)
    gx_im = _bd([(dy_ssm, -wc_im_t)], F32, name="s5_gx_im", operand_dtype=BF16)
    d_wc_re = _bd_wgrad(xs_re, dy_ssm, st_blk, ch_blk, n_blk, name="s5_dc_re", operand_dtype=BF16)
    d_wc_im = _bd_wgrad(xs_im, dy_ssm, st_blk, ch_blk, n_blk, name="s5_dc_im", operand_dtype=BF16, sign=-1.0)
    (lm_re, lm_im, da_re, da_im), _ = _scan_bwd(gx_re, gx_im, xs_re, xs_im, a_re_row, a_im_row, seq)
    d_wb_re = _bd_wgrad(z, lm_re, ch_blk, st_blk, n_blk, name="s5_db_re", off_a=ssm_off, operand_dtype=BF16)
    d_wb_im = _bd_wgrad(z, lm_im, ch_blk, st_blk, n_blk, name="s5_db_im", off_a=ssm_off, operand_dtype=BF16)
    wb_re_t, wb_im_t = jnp.swapaxes(wb_re, 1, 2), jnp.swapaxes(wb_im, 1, 2)
    du_lin = _bd([(lm_re, wb_re_t), (lm_im, wb_im_t)], F32, name="s5_du", operand_dtype=BF16)
    grads["c_re"] = jnp.swapaxes(_block_diag_t(d_wc_re, grp_per_blk, n_state, n_ch), 1, 2)
    grads["c_im"] = jnp.swapaxes(_block_diag_t(d_wc_im, grp_per_blk, n_state, n_ch), 1, 2)
    d_bb_re_t = _block_diag_t(d_wb_re, grp_per_blk, n_ch, n_state)
    d_bb_im_t = _block_diag_t(d_wb_im, grp_per_blk, n_ch, n_state)
    g_lam_re, g_lam_im, g_log_dt, g_b_re_t, g_b_im_t = _discretize_bwd(
        disc_params, (da_re.reshape(n_grp, n_state), da_im.reshape(n_grp, n_state), d_bb_re_t, d_bb_im_t))
    grads["lam_re"], grads["lam_im"], grads["log_dt"] = g_lam_re, g_lam_im, g_log_dt.reshape(n_grp)
    grads["b_re"], grads["b_im"] = jnp.swapaxes(g_b_re_t, 1, 2), jnp.swapaxes(g_b_im_t, 1, 2)

    def join(dzp, du1, du2):
        return (jnp.concatenate([dzp, du1 + du2], axis=-1).astype(BF16),)

    (dz,) = _rowk(join, [dz_pool, du_lin, du_skip], [], [("row", pool_w + ssm_w, BF16)], name="mix_in_join")
    dn2 = _matmul([(dz, full["w_in"])], [F32], name="mix_in_dx", tb=True, tm=1024)
    grads["w_in"] = _matmul([(n2, dz)], [BF16], name="mix_in_dw", ta=True, tn=1024)
    mixer = ["w_out", "w_glu", "w_pool", "w_in"]
    parts_mixer = to_sibling(mixer, "grads_to_sibling_mixer")

    def mix_rms_bwd(dh, dn, h, r, gain):
        dx, dgain = _rms_bwd_rows(dn, h, r, gain)
        dx = dh + dx
        return dx, (0.5 * dx).astype(BF16), dgain

    dh1, dyb1, grads["mix_norm"] = _rowk(mix_rms_bwd, [dh2, dn2, h1, r2], [row(mix_norm)],
                                         [("row", d_model, F32), ("row", d_model, BF16), ("sum", d_model)], name="mix_rms_bwd")

    grads["final_norm"] = g_final
    late = ["ffn1_norm"]
    early = [n for n in names if n not in sharded and n not in late]
    packed_early, layout = _pack([grads[n].reshape(weights[n].shape) for n in early] + [loss_part])
    dx, _, (gathered_early,) = ffn_backward("ffn1", dh1, dyb1, saved1, row(ffn1_norm), to_chips(mixer, parts_mixer),
                                            last=_Broadcast([packed_early]))
    packed_late, layout_late = _pack([grads[n].reshape(weights[n].shape) for n in late])
    (gathered_late,) = _comm_only(_Broadcast([packed_late]), name="small_all_gather_last")
    layout = layout + [(packed_early.shape[0] + r0, rows, n, shape) for r0, rows, n, shape in layout_late]
    small_sum = _sum_parts(jnp.concatenate([gathered_early, gathered_late], axis=1), name="sum_small")
    unpacked = _unpack(small_sum, layout)
    summed = dict(zip(early, unpacked[:len(early)]))
    summed.update(zip(late, unpacked[len(early) + 1:]))
    loss = unpacked[len(early)][0, 0]

    def packed(values, loss_slot):
        return _pack([values[n] for n in early] + [loss_slot] + [values[n] for n in late])[0]

    small_out = [_unpack(o, layout) for o in _adamw(
        packed(weights, jnp.zeros_like(loss_part)), small_sum, packed(moments_m, jnp.zeros_like(loss_part)),
        packed(moments_v, jnp.ones_like(loss_part)), name="adamw_small")]
    delta, new_m, new_v = {}, {}, {}
    for i, n in enumerate(early + [None] + late):
        if n is not None:
            delta[n], new_m[n], new_v[n] = small_out[0][i], small_out[1][i], small_out[2][i]
    for n in sharded:
        shape = weights[n].shape
        if n in transposed:
            to2d, back = (lambda a: a.T), (lambda a: a.T)
        else:
            to2d, back = (lambda a: a.reshape(-1, shape[-1])), (lambda a: a.reshape(shape))
        g, d, nm, nv = _sum_adamw(chip_parts[n], from_chips[n], my_chip, to2d(weights[n]), to2d(moments_m[n]),
                                  to2d(moments_v[n]), name=f"adamw_{n}")
        summed[n], delta[n], new_m[n], new_v[n] = back(g), back(d), back(nm), back(nv)

    return (loss, dx.reshape(x.shape), *[summed[n] for n in names], *[delta[n] for n in names],
            *[new_m[n] for n in names], *[new_v[n] for n in names])
```
